```python
import math
import jax, jax.numpy as jnp
from jax import lax
import numpy as np

D_MODEL = 1024
BATCH = 32
SEQ = 256
DEPTH = 4
DEC_BATCH = 8
DEC_SEQ = 1024
PAST_LEN = 512

GRID_W = 64
Q_BLOCK = 128
ROPE_BASE = 10000.0
EPS = 1e-6
A_HEADS = 4
A_KV_HEADS = 2
A_HEAD_DIM = 64
B_HEADS = 4
B_KEY_DIM = 32
B_VAL_DIM = 64
B_GATE_RANK = 16
B_GATE_NORM = 16.0
B_CHUNK = 64
C_HEADS = 4
C_HEAD_DIM = 32
D_HEADS = 4
D_NOPE_DIM = 64
D_ROPE_DIM = 32
D_V_DIM = 64
D_Q_RANK = 192
D_KV_RANK = 128
N_BRANCH = 4
BRANCH_WIDTH = 256
D_FF = 2816
CONV_W = 3

IN_SPLITS = (
    ('a_q', A_HEADS * A_HEAD_DIM), ('a_k', A_KV_HEADS * A_HEAD_DIM), ('a_v', A_KV_HEADS * A_HEAD_DIM),
    ('b_q', B_HEADS * B_KEY_DIM), ('b_k', B_HEADS * B_KEY_DIM), ('b_v', B_HEADS * B_VAL_DIM),
    ('b_r', B_HEADS * B_VAL_DIM), ('b_gf', B_GATE_RANK), ('b_gb', B_GATE_RANK),
    ('c_q', C_HEADS * 2 * C_HEAD_DIM), ('c_k', C_HEADS * 2 * C_HEAD_DIM), ('c_v', C_HEADS * 2 * C_HEAD_DIM),
    ('d_q', D_Q_RANK), ('d_kv', D_KV_RANK), ('d_kr', D_ROPE_DIM),
    ('gates', N_BRANCH * D_MODEL),
)

kernel_name = 'hybrid_diffusion_prefix_trunk_step'


def split_cols(z):
    out = {}
    off = 0
    for name, w in IN_SPLITS:
        out[name] = z[..., off:off + w]
        off += w
    return out


def rmsnorm(x, g):
    xf = x.astype(jnp.float32)
    y = xf * lax.rsqrt(jnp.mean(xf * xf, axis=-1, keepdims=True) + EPS)
    return (y * g.astype(jnp.float32)).astype(x.dtype)


def grid_positions(n):
    rows = n // GRID_W
    t = jnp.arange(rows * GRID_W)
    return (t // GRID_W).astype(jnp.float32), (t % GRID_W).astype(jnp.float32)


def rope_1d(x, pos):
    half = x.shape[-1] // 2
    inv = ROPE_BASE ** (-jnp.arange(half, dtype=jnp.float32) / half)
    ang = pos[:, None] * inv[None, :]
    shape = (1, pos.shape[0]) + (1,) * (x.ndim - 3) + (half,)
    cos = jnp.cos(ang).reshape(shape)
    sin = jnp.sin(ang).reshape(shape)
    xf = x.astype(jnp.float32)
    x1, x2 = xf[..., :half], xf[..., half:]
    return jnp.concatenate([x1 * cos - x2 * sin, x2 * cos + x1 * sin], axis=-1).astype(x.dtype)


def rope_2d(x, row, col):
    r = x.shape[-1] // 2
    return jnp.concatenate([rope_1d(x[..., :r], row), rope_1d(x[..., r:], col)], axis=-1)


def sweep_q_blocks(fn, qs):
    b, sq = qs[0].shape[:2]
    nb = sq // Q_BLOCK
    blocks = tuple(jnp.moveaxis(q.reshape((b, nb, Q_BLOCK) + q.shape[2:]), 1, 0) for q in qs)
    out = jnp.moveaxis(lax.map(fn, blocks), 0, 1)
    return out.reshape((b, sq) + out.shape[3:])


def gqa_attention(q, k, v, scale):
    def block(qb):
        (qb,) = qb
        s = jnp.einsum('bqhgd,bkhd->bhgqk', qb, k).astype(jnp.float32) * scale
        p = jax.nn.softmax(s, axis=-1).astype(v.dtype)
        return jnp.einsum('bhgqk,bkhd->bqhgd', p, v)
    return sweep_q_blocks(block, (q,))


def diff_attention(q, k, v, lam, scale):
    def block(qb):
        (qb,) = qb
        s = jnp.einsum('bqhjd,bkhjd->bhjqk', qb, k).astype(jnp.float32) * scale
        p = jax.nn.softmax(s, axis=-1)
        a = (p[:, :, 0] - lam * p[:, :, 1]).astype(v.dtype)
        return jnp.einsum('bhqk,bkhe->bqhe', a, v)
    return sweep_q_blocks(block, (q,))


def gla_scan(q, k, v, logg, s0):
    f32 = jnp.float32
    bsz, n, h, dk = q.shape
    dv = v.shape[-1]
    nc = n // B_CHUNK
    ch = lambda t: t.reshape((bsz, nc, B_CHUNK) + t.shape[2:]).astype(f32)
    qc, kc, vc = ch(q), ch(k), ch(v)
    bc = jnp.cumsum(ch(logg), axis=2)
    b_last = bc[:, :, -1]
    causal = jnp.tril(jnp.ones((B_CHUNK, B_CHUNK), bool))[None, None, :, :, None, None]
    expo = bc[:, :, :, None] - bc[:, :, None, :]
    decay = jnp.exp(jnp.where(causal, expo, -jnp.inf))
    attn = jnp.einsum('bcthd,bcshd,bctshd->bchts', qc, kc, decay)
    o_intra = jnp.einsum('bchts,bcshe->bcthe', attn, vc)
    q_dec = qc * jnp.exp(bc)
    k_dec = kc * jnp.exp(b_last[:, :, None] - bc)
    ds = jnp.einsum('bcshd,bcshe->bchde', k_dec, vc)
    g_last = jnp.exp(b_last)

    def step(s, inp):
        g, d = inp
        return g[..., None] * s + d, s

    s_fin, s_in = lax.scan(step, s0.astype(f32), (jnp.moveaxis(g_last, 1, 0), jnp.moveaxis(ds, 1, 0)))
    s_in = jnp.moveaxis(s_in, 0, 1)
    o_inter = jnp.einsum('bcthd,bchde->bcthe', q_dec, s_in)
    o = (o_intra + o_inter).reshape(bsz, n, h, dv)
    return o.astype(v.dtype), s_fin


def gla_bidir(q, k, v, lg_f, lg_b, s0_f, s0_b):
    o_f, s_f = gla_scan(q, k, v, lg_f, s0_f)
    flip = lambda t: jnp.flip(t, axis=1)
    o_b, s_b = gla_scan(flip(q), flip(k), flip(v), flip(lg_b), s0_b)
    return o_f + flip(o_b), s_f, s_b


def mixing_block(h, p, lam_init, pos, ctx):
    f32 = jnp.float32
    bsz, n, _ = h.shape
    z = split_cols(h @ p['w_in'])
    aq = rmsnorm(z['a_q'].reshape(bsz, n, A_KV_HEADS, A_HEADS // A_KV_HEADS, A_HEAD_DIM), p['a_qnorm_g'])
    ak = rmsnorm(z['a_k'].reshape(bsz, n, A_KV_HEADS, A_HEAD_DIM), p['a_knorm_g'])
    av = z['a_v'].reshape(bsz, n, A_KV_HEADS, A_HEAD_DIM)
    bq = z['b_q'].reshape(bsz, n, B_HEADS, B_KEY_DIM) * (B_KEY_DIM ** -0.5)
    bk = z['b_k'].reshape(bsz, n, B_HEADS, B_KEY_DIM)
    bv = z['b_v'].reshape(bsz, n, B_HEADS, B_VAL_DIM)
    lg_f = (jax.nn.log_sigmoid((z['b_gf'] @ p['b_gate_w_fwd'] + p['b_gate_b_fwd']).astype(f32)) / B_GATE_NORM).reshape(bsz, n, B_HEADS, B_KEY_DIM)
    lg_b = (jax.nn.log_sigmoid((z['b_gb'] @ p['b_gate_w_bwd'] + p['b_gate_b_bwd']).astype(f32)) / B_GATE_NORM).reshape(bsz, n, B_HEADS, B_KEY_DIM)
    cq = z['c_q'].reshape(bsz, n, C_HEADS, 2, C_HEAD_DIM)
    ck = z['c_k'].reshape(bsz, n, C_HEADS, 2, C_HEAD_DIM)
    cv = z['c_v'].reshape(bsz, n, C_HEADS, 2 * C_HEAD_DIM)
    dq = (rmsnorm(z['d_q'], p['d_qnorm_g']) @ p['d_w_uq']).reshape(bsz, n, D_HEADS, D_NOPE_DIM + D_ROPE_DIM)
    dckv = rmsnorm(z['d_kv'], p['d_kvnorm_g'])
    dkr = z['d_kr']
    if pos is not None:
        row, col = pos
        aq = rope_2d(aq, row, col)
        ak = rope_2d(ak, row, col)
        cq = rope_2d(cq, row, col)
        ck = rope_2d(ck, row, col)
        dq = jnp.concatenate([dq[..., :D_NOPE_DIM], rope_2d(dq[..., D_NOPE_DIM:], row, col)], axis=-1)
        dkr = rope_2d(dkr, row, col)
    if ctx is None:
        ak_all, av_all, ck_all, cv_all, ckv_all, kr_all = ak, av, ck, cv, dckv, dkr
        s0_f = jnp.zeros((bsz, B_HEADS, B_KEY_DIM, B_VAL_DIM), f32)
        s0_b = s0_f
    else:
        c_ak, c_av, s0_f, s0_b, c_ck, c_cv, c_ckv, c_kr = ctx
        cat = lambda a, b: jnp.concatenate([a.astype(b.dtype), b], axis=1)
        ak_all, av_all = cat(c_ak, ak), cat(c_av, av)
        ck_all, cv_all = cat(c_ck, ck), cat(c_cv, cv)
        ckv_all, kr_all = cat(c_ckv, dckv), cat(c_kr, dkr)
    oa = gqa_attention(aq, ak_all, av_all, A_HEAD_DIM ** -0.5).reshape(bsz, n, BRANCH_WIDTH)
    ob_raw, s_f, s_b = gla_bidir(bq, bk, bv, lg_f, lg_b, s0_f, s0_b)
    ob = (rmsnorm(ob_raw, p['b_onorm_g']) * jax.nn.silu(z['b_r'].reshape(bsz, n, B_HEADS, B_VAL_DIM))).reshape(bsz, n, BRANCH_WIDTH)
    lam = (jnp.exp(jnp.sum(p['c_lq1'].astype(f32) * p['c_lk1'].astype(f32)))
           - jnp.exp(jnp.sum(p['c_lq2'].astype(f32) * p['c_lk2'].astype(f32))) + lam_init)
    oc_raw = diff_attention(cq, ck_all, cv_all, lam, C_HEAD_DIM ** -0.5)
    oc = (rmsnorm(oc_raw, p['c_onorm_g']) * (1.0 - lam_init)).reshape(bsz, n, BRANCH_WIDTH)
    kv_up = (ckv_all @ p['d_w_ukv']).reshape(bsz, -1, D_HEADS, D_NOPE_DIM + D_V_DIM)
    sk = kv_up.shape[1]
    dk_full = jnp.concatenate([kv_up[..., :D_NOPE_DIM],
                               jnp.broadcast_to(kr_all[:, :, None].astype(kv_up.dtype), (bsz, sk, D_HEADS, D_ROPE_DIM))], axis=-1)
    dv_all = kv_up[..., D_NOPE_DIM:]
    od = gqa_attention(dq[:, :, :, None], dk_full, dv_all, (D_NOPE_DIM + D_ROPE_DIM) ** -0.5).reshape(bsz, n, BRANCH_WIDTH)
    branches = jnp.stack([oa, ob, oc, od], axis=2)
    proj = jnp.einsum('bnjw,jwd->bnjd', branches, p['w_branch'])
    gates = jax.nn.sigmoid(z['gates'].reshape(bsz, n, N_BRANCH, D_MODEL).astype(f32)).astype(proj.dtype)
    out = jnp.sum(gates * proj, axis=2) @ p['w_out']
    return out, (ak, av, s_f, s_b, ck, cv, dckv, dkr)


def conv_ffn(h, p):
    u = h @ p['w_ffu']
    g = h @ p['w_ffg']
    up = jnp.pad(u, ((0, 0), (1, 1), (0, 0)))
    w = p['conv_w']
    u = up[:, :-2] * w[0] + up[:, 1:-1] * w[1] + up[:, 2:] * w[2] + p['conv_b']
    return (jax.nn.gelu(u) * g) @ p['w_ffd']


def trunk_layer(x, cvec, p, lam_init, pos, ctx):
    mod = jax.nn.silu(cvec) @ p['w_mod'] + p['b_mod']
    sh1, sc1, g1, sh2, sc2, g2 = jnp.split(mod[:, None, :], 6, axis=-1)
    h = rmsnorm(x, p['norm1_g']) * (1.0 + sc1) + sh1
    mix, cache = mixing_block(h, p, lam_init, pos, ctx)
    x = x + g1 * mix
    h = rmsnorm(x, p['norm2_g']) * (1.0 + sc2) + sh2
    x = x + g2 * conv_ffn(h, p)
    return x, cache


def setup_inputs(seed: int = 0) -> dict:
    key = jax.random.key(seed)
    keys = jax.random.split(key, 64)
    cnt = [0]

    def nrm(shape, scale=1.0):
        k = keys[cnt[0]]
        cnt[0] += 1
        return jax.random.normal(k, shape, jnp.float32) * scale

    def gain(shape):
        return 1.0 + nrm(shape, 0.02)

    in_width = sum(w for _, w in IN_SPLITS)
    L = DEPTH
    return {
        'x_prompt': nrm((BATCH, SEQ, D_MODEL)),
        'x_sample': nrm((DEC_BATCH, DEC_SEQ, D_MODEL)),
        'c': nrm((DEC_BATCH, D_MODEL)),
        'cache_a_k': nrm((DEC_BATCH, L, PAST_LEN, A_KV_HEADS, A_HEAD_DIM)),
        'cache_a_v': nrm((DEC_BATCH, L, PAST_LEN, A_KV_HEADS, A_HEAD_DIM)),
        'state_b_fwd': nrm((DEC_BATCH, L, B_HEADS, B_KEY_DIM, B_VAL_DIM), 0.5),
        'state_b_bwd': nrm((DEC_BATCH, L, B_HEADS, B_KEY_DIM, B_VAL_DIM), 0.5),
        'cache_c_k': nrm((DEC_BATCH, L, PAST_LEN, C_HEADS, 2, C_HEAD_DIM)),
        'cache_c_v': nrm((DEC_BATCH, L, PAST_LEN, C_HEADS, 2 * C_HEAD_DIM)),
        'cache_d_ckv': nrm((DEC_BATCH, L, PAST_LEN, D_KV_RANK)),
        'cache_d_krope': nrm((DEC_BATCH, L, PAST_LEN, D_ROPE_DIM)),
        'c_ctx': nrm((D_MODEL,)),
        'w_mod': nrm((L, D_MODEL, 6 * D_MODEL), 0.5 * D_MODEL ** -0.5),
        'b_mod': nrm((L, 6 * D_MODEL), 0.02),
        'norm1_g': gain((L, D_MODEL)),
        'norm2_g': gain((L, D_MODEL)),
        'w_in': nrm((L, D_MODEL, in_width), D_MODEL ** -0.5),
        'a_qnorm_g': gain((L, A_HEAD_DIM)),
        'a_knorm_g': gain((L, A_HEAD_DIM)),
        'b_gate_w_fwd': nrm((L, B_GATE_RANK, B_HEADS * B_KEY_DIM), B_GATE_RANK ** -0.5),
        'b_gate_b_fwd': nrm((L, B_HEADS * B_KEY_DIM), 0.1),
        'b_gate_w_bwd': nrm((L, B_GATE_RANK, B_HEADS * B_KEY_DIM), B_GATE_RANK ** -0.5),
        'b_gate_b_bwd': nrm((L, B_HEADS * B_KEY_DIM), 0.1),
        'b_onorm_g': gain((L, B_VAL_DIM)),
        'c_lq1': nrm((L, C_HEAD_DIM), 0.1),
        'c_lk1': nrm((L, C_HEAD_DIM), 0.1),
        'c_lq2': nrm((L, C_HEAD_DIM), 0.1),
        'c_lk2': nrm((L, C_HEAD_DIM), 0.1),
        'c_onorm_g': gain((L, 2 * C_HEAD_DIM)),
        'd_qnorm_g': gain((L, D_Q_RANK)),
        'd_w_uq': nrm((L, D_Q_RANK, D_HEADS * (D_NOPE_DIM + D_ROPE_DIM)), D_Q_RANK ** -0.5),
        'd_kvnorm_g': gain((L, D_KV_RANK)),
        'd_w_ukv': nrm((L, D_KV_RANK, D_HEADS * (D_NOPE_DIM + D_V_DIM)), D_KV_RANK ** -0.5),
        'w_branch': nrm((L, N_BRANCH, BRANCH_WIDTH, D_MODEL), BRANCH_WIDTH ** -0.5),
        'w_out': nrm((L, D_MODEL, D_MODEL), D_MODEL ** -0.5),
        'w_ffu': nrm((L, D_MODEL, D_FF), D_MODEL ** -0.5),
        'w_ffg': nrm((L, D_MODEL, D_FF), D_MODEL ** -0.5),
        'conv_w': nrm((L, CONV_W, D_FF), CONV_W ** -0.5),
        'conv_b': nrm((L, D_FF), 0.02),
        'w_ffd': nrm((L, D_FF, D_MODEL), D_FF ** -0.5),
        'final_g': gain((D_MODEL,)),
    }


def reference(x_prompt, x_sample, c, cache_a_k, cache_a_v, state_b_fwd, state_b_bwd, cache_c_k, cache_c_v,
              cache_d_ckv, cache_d_krope, c_ctx, w_mod, b_mod, norm1_g, norm2_g, w_in, a_qnorm_g, a_knorm_g,
              b_gate_w_fwd, b_gate_b_fwd, b_gate_w_bwd, b_gate_b_bwd, b_onorm_g, c_lq1, c_lk1, c_lq2, c_lk2,
              c_onorm_g, d_qnorm_g, d_w_uq, d_kvnorm_g, d_w_ukv, w_branch, w_out, w_ffu, w_ffg, conv_w, conv_b,
              w_ffd, final_g):
    pos = grid_positions(x_sample.shape[1])
    xp, xs = x_prompt, x_sample
    ctx_vec = c_ctx[None, :]
    caches = []
    for l in range(DEPTH):
        p = {
            'w_mod': w_mod[l], 'b_mod': b_mod[l], 'norm1_g': norm1_g[l], 'norm2_g': norm2_g[l],
            'w_in': w_in[l], 'a_qnorm_g': a_qnorm_g[l], 'a_knorm_g': a_knorm_g[l],
            'b_gate_w_fwd': b_gate_w_fwd[l], 'b_gate_b_fwd': b_gate_b_fwd[l],
            'b_gate_w_bwd': b_gate_w_bwd[l], 'b_gate_b_bwd': b_gate_b_bwd[l], 'b_onorm_g': b_onorm_g[l],
            'c_lq1': c_lq1[l], 'c_lk1': c_lk1[l], 'c_lq2': c_lq2[l], 'c_lk2': c_lk2[l], 'c_onorm_g': c_onorm_g[l],
            'd_qnorm_g': d_qnorm_g[l], 'd_w_uq': d_w_uq[l], 'd_kvnorm_g': d_kvnorm_g[l], 'd_w_ukv': d_w_ukv[l],
            'w_branch': w_branch[l], 'w_out': w_out[l], 'w_ffu': w_ffu[l], 'w_ffg': w_ffg[l],
            'conv_w': conv_w[l], 'conv_b': conv_b[l], 'w_ffd': w_ffd[l],
        }
        lam_init = 0.8 - 0.6 * math.exp(-0.3 * l)
        xp, cache_l = trunk_layer(xp, ctx_vec, p, lam_init, None, None)
        caches.append(cache_l)
        ctx_l = (cache_a_k[:, l], cache_a_v[:, l], state_b_fwd[:, l], state_b_bwd[:, l],
                 cache_c_k[:, l], cache_c_v[:, l], cache_d_ckv[:, l], cache_d_krope[:, l])
        xs, _ = trunk_layer(xs, c, p, lam_init, pos, ctx_l)
    y_prompt = rmsnorm(xp, final_g)
    y_sample = rmsnorm(xs, final_g)
    stack = lambda i: jnp.stack([cl[i] for cl in caches], axis=1)
    new_a_k = stack(0)
    new_a_v = stack(1)
    new_b_fwd = stack(2)
    new_b_bwd = stack(3)
    new_c_k = stack(4)
    new_c_v = stack(5)
    new_d_ckv = stack(6)
    new_d_krope = stack(7)
    return (y_prompt, y_sample, new_a_k, new_a_v, new_b_fwd, new_b_bwd, new_c_k, new_c_v, new_d_ckv, new_d_krope)
```

```python
import functools
import math

import numpy as np
import jax
import jax.numpy as jnp
from jax import lax
from jax.experimental import pallas as pl
from jax.experimental.pallas import tpu as pltpu

F32 = jnp.float32
BF16 = jnp.bfloat16

D_MODEL = 1024
SEQ = 256
DEC_SEQ = 1024
PAST_LEN = 512
GRID_W = 64
ROPE_BASE = 10000.0
EPS = 1e-6
HEAD_W = 64
BRANCH_WIDTH = 256
GLA_KEY_W = 32
GLA_CHUNK = 64
GLA_SUB = 16
GLA_GATE_NORM = 16.0
D_Q_RANK = 192
D_FF = 2816
FF_CHUNK = 256
N_MOD_ROWS = 16

ROW_TILE = 512
Q_TILE = 256
HALO = 8
VMEM_LIMIT = 56 * 1024 * 1024

NZ = 2560
Z_AQ, Z_AK, Z_AV, Z_BQ, Z_BK, Z_BV, Z_BR = 0, 256, 384, 512, 640, 768, 1024
Z_CQ, Z_CK, Z_CV, Z_DKV, Z_KR, Z_DQ, Z_TAIL = 1280, 1536, 1792, 2048, 2176, 2304, 2432
F_W = 2048
F_AK, F_AV, F_BQ, F_BK, F_BV, F_BR, F_LGF, F_LGB = 0, 128, 256, 384, 512, 768, 1024, 1152
F_CK, F_CV, F_DKV, F_KR = 1280, 1536, 1792, 1920
Q_W = 1152
Q_AQ, Q_CQ, Q_DLAT, Q_DQR = 0, 256, 512, 1024
ROPE_W = 768


def _dot(a, b):
    return jnp.dot(a, b, preferred_element_type=F32)


def _dot_nt(a, b):
    return lax.dot_general(a, b, (((1,), (1,)), ((), ())), preferred_element_type=F32)


def _dot_tn(a, b):
    return lax.dot_general(a, b, (((0,), (0,)), ((), ())), preferred_element_type=F32)


def _split_dot(a, b01):
    hi = a.astype(BF16)
    r1 = a - hi.astype(F32)
    mid = r1.astype(BF16)
    lo = (r1 - mid.astype(F32)).astype(BF16)
    return _dot(hi, b01) + _dot(mid, b01) + _dot(lo, b01)


def _split_dot_tn(b01, a):
    hi = a.astype(BF16)
    r1 = a - hi.astype(F32)
    mid = r1.astype(BF16)
    lo = (r1 - mid.astype(F32)).astype(BF16)
    return _dot(b01, hi) + _dot(b01, mid) + _dot(b01, lo)


def _group_rms(v, bd, width, gain):
    ss = _split_dot(v * v, bd)
    return v * lax.rsqrt(ss * (1.0 / width) + EPS) * gain


def _rope128(v, cos, sin_a, sin_b, shift):
    return v * cos + pltpu.roll(v, 128 - shift, 1) * sin_a + pltpu.roll(v, shift, 1) * sin_b


def _modulated_norm(x, gain, shift, scale):
    ms = jnp.mean(x * x, axis=-1, keepdims=True)
    return (x * lax.rsqrt(ms + EPS) * gain) * (1.0 + scale) + shift


def _lane_mask(width, lo, hi):
    lane = lax.broadcasted_iota(jnp.int32, (1, width), 1)
    return (lane >= lo) & (lane < hi)


def _mod_kernel(cv_ref, w_ref, b_ref, o_ref):
    cv = cv_ref[...]
    a = cv * jax.nn.sigmoid(cv)
    o_ref[...] = _dot(a.astype(BF16), w_ref[...].astype(BF16)) + b_ref[...]


def _mod_call(cvec, w_mod, b_mod):
    depth = w_mod.shape[0]
    nblk = 4
    bw = 6 * D_MODEL // nblk
    return pl.pallas_call(
        _mod_kernel,
        grid=(depth, nblk),
        in_specs=[
            pl.BlockSpec((N_MOD_ROWS, D_MODEL), lambda l, j: (0, 0)),
            pl.BlockSpec((None, D_MODEL, bw), lambda l, j: (l, 0, j)),
            pl.BlockSpec((None, 1, bw), lambda l, j: (l, 0, j)),
        ],
        out_specs=pl.BlockSpec((None, N_MOD_ROWS, bw), lambda l, j: (l, 0, j)),
        out_shape=jax.ShapeDtypeStruct((depth, N_MOD_ROWS, 6 * D_MODEL), F32),
        compiler_params=pltpu.CompilerParams(vmem_limit_bytes=VMEM_LIMIT),
        name="mod_vectors",
    )(cvec, w_mod, b_mod.reshape(depth, 1, 6 * D_MODEL))


def _inproj_kernel(x_ref, mod_ref, n1g_ref, w_ref, rope_ref, aqg_ref, akg_ref, bd_ref, wgate_ref,
                   bgate_ref, dqg_ref, wuq_ref, wuk_ref, dkvg_ref, f_ref, q_ref):
    x = x_ref[...]
    mod = mod_ref[...]
    h = _modulated_norm(x, n1g_ref[...], mod[:, 0:D_MODEL], mod[:, D_MODEL:2 * D_MODEL])
    z = _dot(h.astype(BF16), w_ref[...])
    bd = bd_ref[...]
    bd128 = bd[0:128, 0:128]
    rope = rope_ref[...]
    cos64, sa64, sb64 = rope[:, 0:128], rope[:, 128:256], rope[:, 256:384]
    cos32, sa32, sb32 = rope[:, 384:512], rope[:, 512:640], rope[:, 640:768]
    rope64 = lambda v: _rope128(v, cos64, sa64, sb64, 16)
    rope32 = lambda v: _rope128(v, cos32, sa32, sb32, 8)

    aq = _group_rms(z[:, Z_AQ:Z_AQ + 256], bd, HEAD_W, aqg_ref[...])
    q_ref[:, Q_AQ:Q_AQ + 128] = rope64(aq[:, 0:128]).astype(BF16)
    q_ref[:, Q_AQ + 128:Q_AQ + 256] = rope64(aq[:, 128:256]).astype(BF16)
    ak = _group_rms(z[:, Z_AK:Z_AK + 128], bd128, HEAD_W, akg_ref[...])
    f_ref[:, F_AK:F_AK + 128] = rope64(ak)
    f_ref[:, F_AV:F_AV + 128] = z[:, Z_AV:Z_AV + 128]

    f_ref[:, F_BQ:F_BQ + 128] = z[:, Z_BQ:Z_BQ + 128] * (GLA_KEY_W ** -0.5)
    f_ref[:, F_BK:F_BV] = z[:, Z_BK:Z_BV]
    f_ref[:, F_BV:F_BR] = z[:, Z_BV:Z_BR]
    f_ref[:, F_BR:F_LGF] = z[:, Z_BR:Z_CQ]
    tail = z[:, Z_TAIL:NZ]
    pre = _dot(tail.astype(BF16), wgate_ref[...]) + bgate_ref[...]
    logsig = jnp.minimum(pre, 0.0) - jnp.log1p(jnp.exp(-jnp.abs(pre)))
    f_ref[:, F_LGF:F_CK] = logsig / GLA_GATE_NORM

    q_ref[:, Q_CQ:Q_CQ + 128] = rope32(z[:, Z_CQ:Z_CQ + 128]).astype(BF16)
    q_ref[:, Q_CQ + 128:Q_CQ + 256] = rope32(z[:, Z_CQ + 128:Z_CQ + 256]).astype(BF16)
    f_ref[:, F_CK:F_CK + 128] = rope32(z[:, Z_CK:Z_CK + 128])
    f_ref[:, F_CK + 128:F_CK + 256] = rope32(z[:, Z_CK + 128:Z_CK + 256])
    f_ref[:, F_CV:F_DKV] = z[:, Z_CV:Z_DKV]

    dkv = z[:, Z_DKV:Z_DKV + 128]
    ms = jnp.mean(dkv * dkv, axis=-1, keepdims=True)
    f_ref[:, F_DKV:F_DKV + 128] = dkv * lax.rsqrt(ms + EPS) * dkvg_ref[...]
    f_ref[:, F_KR:F_KR + 128] = rope32(z[:, Z_KR:Z_KR + 128])
    zq = z[:, Z_DQ:NZ]
    zq = jnp.where(_lane_mask(NZ - Z_DQ, 0, D_Q_RANK), zq, 0.0)
    msq = jnp.sum(zq * zq, axis=-1, keepdims=True) * (1.0 / D_Q_RANK)
    qn = zq * lax.rsqrt(msq + EPS) * dqg_ref[...]
    dq = _dot(qn.astype(BF16), wuq_ref[...])
    q_ref[:, Q_DLAT:Q_DQR] = _dot(dq[:, 0:256].astype(BF16), wuk_ref[...]).astype(BF16)
    q_ref[:, Q_DQR:Q_W] = rope32(dq[:, 256:384]).astype(BF16)


def _inproj_call(x, mod4, n1g, w_mix, rope_tab, aqg, akg, bd, wgate, bgate, dqg, wuq, wuk, dkvg,
                 layer, n_ctx_rows):
    n = x.shape[0]
    tm = ROW_TILE
    ctx_tiles = n_ctx_rows // tm
    seq_tiles = DEC_SEQ // tm

    def mod_map(i):
        return (layer, jnp.where(i < ctx_tiles, 0, 1 + (i - ctx_tiles) // seq_tiles), 0, 0)

    def rope_map(i):
        return (jnp.where(i < ctx_tiles, seq_tiles, (i - ctx_tiles) % seq_tiles), 0)

    const = lambda shape: pl.BlockSpec(shape, lambda i: tuple(0 for _ in shape))
    return pl.pallas_call(
        _inproj_kernel,
        grid=(n // tm,),
        in_specs=[
            pl.BlockSpec((tm, D_MODEL), lambda i: (i, 0)),
            pl.BlockSpec((None, None, 1, 6 * D_MODEL), mod_map),
            const((1, D_MODEL)),
            const((D_MODEL, NZ)),
            pl.BlockSpec((tm, ROPE_W), rope_map),
            const((1, 256)), const((1, 128)), const((256, 256)), const((128, 256)), const((1, 256)),
            const((1, 256)), const((256, 384)), const((256, 512)), const((1, 128)),
        ],
        out_specs=[pl.BlockSpec((tm, F_W), lambda i: (i, 0)), pl.BlockSpec((tm, Q_W), lambda i: (i, 0))],
        out_shape=[jax.ShapeDtypeStruct((n, F_W), F32), jax.ShapeDtypeStruct((n, Q_W), BF16)],
        compiler_params=pltpu.CompilerParams(vmem_limit_bytes=VMEM_LIMIT),
        name="in_projection",
    )(x, mod4, n1g, w_mix, rope_tab, aqg, akg, bd, wgate, bgate, dqg, wuq, wuk, dkvg)


def _softmax_parts(parts, scale):
    parts = [s * scale for s in parts]
    m = parts[0].max(axis=-1, keepdims=True)
    for s in parts[1:]:
        m = jnp.maximum(m, s.max(axis=-1, keepdims=True))
    ps = [jnp.exp(s - m) for s in parts]
    tot = ps[0].sum(axis=-1, keepdims=True)
    for p in ps[1:]:
        tot = tot + p.sum(axis=-1, keepdims=True)
    return ps, 1.0 / tot


def _attn_body(q_ref, kva_ref, ck_ref, cv_ref, dkv_ref, cache_refs, lam_ref, cong_ref, wuv_ref, bd_ref,
               tile_ref, o_ref, lam_init):
    q = q_ref[...]
    has_cache = cache_refs is not None
    kva = kva_ref[...]
    ak = [kva[:, 0:128].astype(BF16)]
    av = [kva[:, 128:256].astype(BF16)]
    ck = [ck_ref[...].astype(BF16)]
    cv = [cv_ref[...].astype(BF16)]
    dk = [dkv_ref[...].astype(BF16)]
    if has_cache:
        cak_ref, cav_ref, cck_ref, ccv_ref, cckv_ref, ckr_ref = cache_refs
        ak.insert(0, cak_ref[...].astype(BF16))
        av.insert(0, cav_ref[...].astype(BF16))
        ck.insert(0, cck_ref[...].astype(BF16))
        cv.insert(0, ccv_ref[...].astype(BF16))
        kr4 = _dot(ckr_ref[...].astype(BF16), tile_ref[...])
        dk.insert(0, jnp.concatenate([cckv_ref[...].astype(BF16), kr4.astype(BF16)], axis=-1))
    zero = jnp.zeros((), BF16)

    oa = []
    for g in range(2):
        qg = q[:, Q_AQ + g * 128:Q_AQ + (g + 1) * 128]
        acc = None
        for hh in range(2):
            m = _lane_mask(128, hh * HEAD_W, (hh + 1) * HEAD_W)
            qm = jnp.where(m, qg, zero)
            ps, inv = _softmax_parts([_dot_nt(qm, k) for k in ak], HEAD_W ** -0.5)
            o = sum(_dot(p.astype(BF16), v) for p, v in zip(ps, av)) * inv
            o = jnp.where(m, o, 0.0)
            acc = o if acc is None else acc + o
        oa.append(acc)
    o_ref[:, 0:128] = oa[0]
    o_ref[:, 128:256] = oa[1]

    lv = lam_ref[...]
    lam = (jnp.exp(jnp.sum(lv[0:1] * lv[1:2], axis=-1, keepdims=True))
           - jnp.exp(jnp.sum(lv[2:3] * lv[3:4], axis=-1, keepdims=True)) + lam_init)
    oc = None
    for hh in range(4):
        prob = []
        for j in range(2):
            u = hh * 2 + j
            c0 = (u // 4) * 128
            m = _lane_mask(128, (u % 4) * 32, (u % 4 + 1) * 32)
            qm = jnp.where(m, q[:, Q_CQ + c0:Q_CQ + c0 + 128], zero)
            ps, inv = _softmax_parts([_dot_nt(qm, k[:, c0:c0 + 128]) for k in ck], 32 ** -0.5)
            prob.append([p * inv for p in ps])
        o = sum(_dot((p0 - lam * p1).astype(BF16), v) for p0, p1, v in zip(prob[0], prob[1], cv))
        o = jnp.where(_lane_mask(256, hh * HEAD_W, (hh + 1) * HEAD_W), o, 0.0)
        oc = o if oc is None else oc + o
    o_ref[:, 256:512] = _group_rms(oc, bd_ref[...], HEAD_W, cong_ref[...]) * (1.0 - lam_init)

    lat = []
    for hh in range(4):
        m = _lane_mask(128, hh * 32, (hh + 1) * 32)
        qh = jnp.concatenate([q[:, Q_DLAT + hh * 128:Q_DLAT + (hh + 1) * 128],
                              jnp.where(m, q[:, Q_DQR:Q_W], zero)], axis=-1)
        ps, inv = _softmax_parts([_dot_nt(qh, k) for k in dk], 96 ** -0.5)
        lat.append((sum(_dot(p.astype(BF16), k[:, 0:128]) for p, k in zip(ps, dk)) * inv).astype(BF16))
    o_ref[:, 512:768] = _dot(jnp.concatenate(lat, axis=-1), wuv_ref[...])


def _attn_ctx_kernel(q_ref, kva_ref, ck_ref, cv_ref, dkv_ref, lam_ref, cong_ref, wuv_ref, bd_ref,
                     tile_ref, o_ref, *, lam_init):
    _attn_body(q_ref, kva_ref, ck_ref, cv_ref, dkv_ref, None, lam_ref, cong_ref, wuv_ref, bd_ref,
               tile_ref, o_ref, lam_init)


def _attn_lat_kernel(q_ref, kva_ref, ck_ref, cv_ref, dkv_ref, cak_ref, cav_ref, cck_ref, ccv_ref,
                     cckv_ref, ckr_ref, lam_ref, cong_ref, wuv_ref, bd_ref, tile_ref, prev_ref, o_ref,
                     *, lam_init):
    del prev_ref
    _attn_body(q_ref, kva_ref, ck_ref, cv_ref, dkv_ref,
               (cak_ref, cav_ref, cck_ref, ccv_ref, cckv_ref, ckr_ref), lam_ref, cong_ref, wuv_ref,
               bd_ref, tile_ref, o_ref, lam_init)


def _attn_calls(fa, qa, caches, lamv, cong, wuv, bd, tile, layer, lam_init, n_ctx_rows):
    n = fa.shape[0]
    n_ctx_seq = n_ctx_rows // SEQ
    n_lat_seq = (n - n_ctx_rows) // DEC_SEQ
    const = lambda shape, nd: pl.BlockSpec(shape, (lambda *a: tuple(0 for _ in shape)))
    params = [const((4, 32), 1), const((1, 256), 1), const((512, 256), 1), const((256, 256), 1),
              const((32, 128), 1)]
    out_shape = jax.ShapeDtypeStruct((n, 3 * BRANCH_WIDTH), F32)
    cparams = pltpu.CompilerParams(vmem_limit_bytes=VMEM_LIMIT)

    o_ctx = pl.pallas_call(
        functools.partial(_attn_ctx_kernel, lam_init=lam_init),
        grid=(n_ctx_seq,),
        in_specs=[
            pl.BlockSpec((SEQ, Q_W), lambda b: (b, 0)),
            pl.BlockSpec((SEQ, 256), lambda b: (b, F_AK // 256)),
            pl.BlockSpec((SEQ, 256), lambda b: (b, F_CK // 256)),
            pl.BlockSpec((SEQ, 256), lambda b: (b, F_CV // 256)),
            pl.BlockSpec((SEQ, 256), lambda b: (b, F_DKV // 256)),
        ] + params,
        out_specs=pl.BlockSpec((SEQ, 3 * BRANCH_WIDTH), lambda b: (b, 0)),
        out_shape=out_shape,
        compiler_params=cparams,
        name="attention_context",
    )(qa, fa, fa, fa, fa, lamv, cong, wuv, bd, tile)

    qt = Q_TILE
    q_per_seq = DEC_SEQ // qt
    q0 = n_ctx_rows // qt
    s0 = n_ctx_rows // DEC_SEQ
    cak, cav, cck, ccv, cckv, ckr = caches
    cache_spec = lambda w: pl.BlockSpec((None, None, PAST_LEN, w), lambda b, i: (b, layer, 0, 0))
    o_all = pl.pallas_call(
        functools.partial(_attn_lat_kernel, lam_init=lam_init),
        grid=(n_lat_seq, q_per_seq),
        in_specs=[
            pl.BlockSpec((qt, Q_W), lambda b, i: (q0 + b * q_per_seq + i, 0)),
            pl.BlockSpec((DEC_SEQ, 256), lambda b, i: (s0 + b, F_AK // 256)),
            pl.BlockSpec((DEC_SEQ, 256), lambda b, i: (s0 + b, F_CK // 256)),
            pl.BlockSpec((DEC_SEQ, 256), lambda b, i: (s0 + b, F_CV // 256)),
            pl.BlockSpec((DEC_SEQ, 256), lambda b, i: (s0 + b, F_DKV // 256)),
            cache_spec(128), cache_spec(128), cache_spec(256), cache_spec(256), cache_spec(128),
            cache_spec(32),
        ] + [pl.BlockSpec(p.block_shape, lambda b, i, _s=p.block_shape: tuple(0 for _ in _s))
             for p in params] + [pl.BlockSpec(memory_space=pl.ANY)],
        out_specs=pl.BlockSpec((qt, 3 * BRANCH_WIDTH), lambda b, i: (q0 + b * q_per_seq + i, 0)),
        out_shape=out_shape,
        input_output_aliases={16: 0},
        compiler_params=cparams,
        name="attention_latent",
    )(qa, fa, fa, fa, fa, cak, cav, cck, ccv, cckv, ckr, lamv, cong, wuv, bd, tile, o_ctx)
    return o_all


def _gla_chunk(q, k, v, lg, st, tri, reverse):
    c, sub, nsub = GLA_CHUNK, GLA_SUB, GLA_CHUNK // GLA_SUB
    bc = _split_dot_tn(tri, lg)
    edge = bc[0:1] if reverse else bc[c - 1:c]
    row_t = lax.broadcasted_iota(jnp.int32, (4 * sub, c), 0) % sub
    col_s = lax.broadcasted_iota(jnp.int32, (4 * sub, c), 1)
    head_of_row = lax.broadcasted_iota(jnp.int32, (4 * sub, 128), 0) // sub
    head_of_lane = lax.broadcasted_iota(jnp.int32, (4 * sub, 128), 1) // GLA_KEY_W
    qmask = head_of_row == head_of_lane
    src_row = lax.broadcasted_iota(jnp.int32, (c, 1), 0)
    vb = v.astype(BF16)
    out_blocks = []
    for i in range(nsub):
        r0 = i * sub
        ref = bc[r0 + sub - 1:r0 + sub] if reverse else bc[r0:r0 + 1]
        qd = q[r0:r0 + sub] * jnp.exp(bc[r0:r0 + sub] - ref)
        in_range = (src_row >= r0) if reverse else (src_row < r0 + sub)
        kd = k * jnp.exp(jnp.where(in_range, ref - bc, 0.0))
        qs = jnp.where(qmask, jnp.concatenate([qd] * 4, axis=0), 0.0)
        att = _dot_nt(qs.astype(BF16), kd.astype(BF16))
        keep = (col_s >= row_t + r0) if reverse else (col_s <= row_t + r0)
        att = jnp.where(keep, att, 0.0)
        pv = _dot(att.astype(BF16), vb)
        o = None
        for hh in range(4):
            part = jnp.where(_lane_mask(256, hh * HEAD_W, (hh + 1) * HEAD_W), pv[hh * sub:(hh + 1) * sub], 0.0)
            o = part if o is None else o + part
        out_blocks.append(o)
    o_intra = jnp.concatenate(out_blocks, axis=0)
    q_dec = q * jnp.exp(bc)
    k_dec = k * jnp.exp(edge - bc)
    o_inter = _dot_nt(q_dec.astype(BF16), st.astype(BF16))
    upd = _dot_tn(vb, k_dec.astype(BF16))
    blk = (lax.broadcasted_iota(jnp.int32, (256, 128), 0) // HEAD_W
           == lax.broadcasted_iota(jnp.int32, (256, 128), 1) // GLA_KEY_W)
    st_new = st * jnp.exp(edge) + jnp.where(blk, upd, 0.0)
    return o_intra + o_inter, st_new


def _gla_body(qk_ref, v_ref, r_ref, lg_ref, s0f, s0b, g_ref, bd_ref, tri_ref, o_ref, acc_ref, seq_len):
    nc = seq_len // GLA_CHUNK
    tri_f = tri_ref[0]
    tri_b = tri_ref[1]

    def fwd_step(ci, st):
        rows = pl.ds(pl.multiple_of(ci * GLA_CHUNK, GLA_CHUNK), GLA_CHUNK)
        o, st = _gla_chunk(qk_ref[rows, 0:128], qk_ref[rows, 128:256], v_ref[rows, :], lg_ref[rows, 0:128],
                           st, tri_f, False)
        acc_ref[rows, :] = o
        return st

    def bwd_step(cj, st):
        ci = nc - 1 - cj
        rows = pl.ds(pl.multiple_of(ci * GLA_CHUNK, GLA_CHUNK), GLA_CHUNK)
        o, st = _gla_chunk(qk_ref[rows, 0:128], qk_ref[rows, 128:256], v_ref[rows, :], lg_ref[rows, 128:256],
                           st, tri_b, True)
        acc_ref[rows, :] = acc_ref[rows, :] + o
        return st

    sf = lax.fori_loop(0, nc, fwd_step, s0f)
    sb = lax.fori_loop(0, nc, bwd_step, s0b)
    r = r_ref[...]
    o_ref[...] = _group_rms(acc_ref[...], bd_ref[...], HEAD_W, g_ref[...]) * (r * jax.nn.sigmoid(r))
    return sf, sb


def _gla_ctx_kernel(qk_ref, v_ref, r_ref, lg_ref, g_ref, bd_ref, tri_ref, o_ref, sf_ref, sb_ref, acc_ref):
    zero = jnp.zeros((256, 128), F32)
    sf, sb = _gla_body(qk_ref, v_ref, r_ref, lg_ref, zero, zero, g_ref, bd_ref, tri_ref, o_ref, acc_ref, SEQ)
    sf_ref[...] = sf
    sb_ref[...] = sb


def _gla_lat_kernel(qk_ref, v_ref, r_ref, lg_ref, s0f_ref, s0b_ref, g_ref, bd_ref, tri_ref, prev_ref,
                    o_ref, acc_ref):
    del prev_ref
    _gla_body(qk_ref, v_ref, r_ref, lg_ref, s0f_ref[...], s0b_ref[...], g_ref, bd_ref, tri_ref, o_ref,
              acc_ref, DEC_SEQ)


def _gla_calls(fa, s0f_t, s0b_t, bong, bd, tri, n_ctx_rows):
    n = fa.shape[0]
    n_ctx_seq = n_ctx_rows // SEQ
    n_lat_seq = (n - n_ctx_rows) // DEC_SEQ
    cparams = pltpu.CompilerParams(vmem_limit_bytes=VMEM_LIMIT)
    const = lambda shape: pl.BlockSpec(shape, lambda b: tuple(0 for _ in shape))
    out_shape = jax.ShapeDtypeStruct((n, BRANCH_WIDTH), F32)
    st_shape = jax.ShapeDtypeStruct((n_ctx_seq, 256, 128), F32)
    o_ctx, sf, sb = pl.pallas_call(
        _gla_ctx_kernel,
        grid=(n_ctx_seq,),
        in_specs=[
            pl.BlockSpec((SEQ, 256), lambda b: (b, F_BQ // 256)),
            pl.BlockSpec((SEQ, 256), lambda b: (b, F_BV // 256)),
            pl.BlockSpec((SEQ, 256), lambda b: (b, F_BR // 256)),
            pl.BlockSpec((SEQ, 256), lambda b: (b, F_LGF // 256)),
            const((1, 256)), const((256, 256)), const((2, GLA_CHUNK, GLA_CHUNK)),
        ],
        out_specs=[pl.BlockSpec((SEQ, BRANCH_WIDTH), lambda b: (b, 0)),
                   pl.BlockSpec((None, 256, 128), lambda b: (b, 0, 0)),
                   pl.BlockSpec((None, 256, 128), lambda b: (b, 0, 0))],
        out_shape=[out_shape, st_shape, st_shape],
        scratch_shapes=[pltpu.VMEM((SEQ, BRANCH_WIDTH), F32)],
        compiler_params=cparams,
        name="gla_context",
    )(fa, fa, fa, fa, bong, bd, tri)

    s0 = n_ctx_rows // DEC_SEQ
    o_all = pl.pallas_call(
        _gla_lat_kernel,
        grid=(n_lat_seq,),
        in_specs=[
            pl.BlockSpec((DEC_SEQ, 256), lambda b: (s0 + b, F_BQ // 256)),
            pl.BlockSpec((DEC_SEQ, 256), lambda b: (s0 + b, F_BV // 256)),
            pl.BlockSpec((DEC_SEQ, 256), lambda b: (s0 + b, F_BR // 256)),
            pl.BlockSpec((DEC_SEQ, 256), lambda b: (s0 + b, F_LGF // 256)),
            pl.BlockSpec((None, 256, 128), lambda b: (b, 0, 0)),
            pl.BlockSpec((None, 256, 128), lambda b: (b, 0, 0)),
            const((1, 256)), const((256, 256)), const((2, GLA_CHUNK, GLA_CHUNK)),
            pl.BlockSpec(memory_space=pl.ANY),
        ],
        out_specs=pl.BlockSpec((DEC_SEQ, BRANCH_WIDTH), lambda b: (s0 + b, 0)),
        out_shape=out_shape,
        scratch_shapes=[pltpu.VMEM((DEC_SEQ, BRANCH_WIDTH), F32)],
        input_output_aliases={9: 0},
        compiler_params=cparams,
        name="gla_latent",
    )(fa, fa, fa, fa, s0f_t, s0b_t, bong, bd, tri, o_ctx)
    return o_all, sf, sb


def _merge_kernel(x_ref, mod_ref, n1g_ref, wg_ref, oacd_ref, ob_ref, wb_ref, wout_ref, o_ref):
    x = x_ref[...]
    mod = mod_ref[...]
    h = _modulated_norm(x, n1g_ref[...], mod[:, 0:D_MODEL], mod[:, D_MODEL:2 * D_MODEL]).astype(BF16)
    oacd = oacd_ref[...]
    branches = [oacd[:, 0:256], ob_ref[...], oacd[:, 256:512], oacd[:, 512:768]]
    acc = None
    for j in range(4):
        gate = jax.nn.sigmoid(_dot(h, wg_ref[:, j * D_MODEL:(j + 1) * D_MODEL]))
        term = gate * _dot(branches[j].astype(BF16), wb_ref[j])
        acc = term if acc is None else acc + term
    mix = _dot(acc.astype(BF16), wout_ref[...])
    o_ref[...] = x + mod[:, 2 * D_MODEL:3 * D_MODEL] * mix


def _merge_call(x, mod4, n1g, wgates, oacd, ob, wb, wout, layer, n_ctx_rows):
    n = x.shape[0]
    tm = ROW_TILE
    ctx_tiles = n_ctx_rows // tm
    seq_tiles = DEC_SEQ // tm

    def mod_map(i):
        return (layer, jnp.where(i < ctx_tiles, 0, 1 + (i - ctx_tiles) // seq_tiles), 0, 0)

    const = lambda shape: pl.BlockSpec(shape, lambda i: tuple(0 for _ in shape))
    return pl.pallas_call(
        _merge_kernel,
        grid=(n // tm,),
        in_specs=[
            pl.BlockSpec((tm, D_MODEL), lambda i: (i, 0)),
            pl.BlockSpec((None, None, 1, 6 * D_MODEL), mod_map),
            const((1, D_MODEL)),
            const((D_MODEL, 4 * D_MODEL)),
            pl.BlockSpec((tm, 3 * BRANCH_WIDTH), lambda i: (i, 0)),
            pl.BlockSpec((tm, BRANCH_WIDTH), lambda i: (i, 0)),
            const((4, BRANCH_WIDTH, D_MODEL)),
            const((D_MODEL, D_MODEL)),
        ],
        out_specs=pl.BlockSpec((tm, D_MODEL), lambda i: (i, 0)),
        out_shape=jax.ShapeDtypeStruct((n, D_MODEL), F32),
        compiler_params=pltpu.CompilerParams(vmem_limit_bytes=VMEM_LIMIT),
        name="merge",
    )(x, mod4, n1g, wgates, oacd, ob, wb, wout)


def _ffn_kernel(x_ref, xp_ref, xn_ref, mod_ref, n2g_ref, wu_ref, wg_ref, cw_ref, cb_ref, wd_ref, fg_ref,
                o_ref, y_ref, *, ctx_tiles, final):
    tm = x_ref.shape[0]
    i = pl.program_id(0)
    x = x_ref[...]
    mod = mod_ref[...]
    sh2, sc2, g2 = mod[:, 3 * D_MODEL:4 * D_MODEL], mod[:, 4 * D_MODEL:5 * D_MODEL], mod[:, 5 * D_MODEL:]
    n2g = n2g_ref[...]
    h = _modulated_norm(x, n2g, sh2, sc2).astype(BF16)
    hp = _modulated_norm(xp_ref[...], n2g, sh2, sc2).astype(BF16)
    hn = _modulated_norm(xn_ref[...], n2g, sh2, sc2).astype(BF16)
    seq_len = jnp.where(i < ctx_tiles, SEQ, DEC_SEQ)
    row = lax.broadcasted_iota(jnp.int32, (tm, 1), 0)
    pos = (row + i * tm) & (seq_len - 1)
    first = pos == 0
    last = pos == seq_len - 1
    acc = jnp.zeros((tm, D_MODEL), F32)
    for c0 in range(0, D_FF, FF_CHUNK):
        wu = wu_ref[:, c0:c0 + FF_CHUNK]
        u = _dot(h, wu)
        up = _dot(hp, wu)[HALO - 1:HALO]
        un = _dot(hn, wu)[0:1]
        u_dn = jnp.where(row == 0, up, pltpu.roll(u, 1, 0))
        u_dn = jnp.where(first, 0.0, u_dn)
        u_up = jnp.where(row == tm - 1, un, pltpu.roll(u, tm - 1, 0))
        u_up = jnp.where(last, 0.0, u_up)
        cw = cw_ref[:, c0:c0 + FF_CHUNK]
        uc = u_dn * cw[0:1] + u * cw[1:2] + u_up * cw[2:3] + cb_ref[:, c0:c0 + FF_CHUNK]
        g = _dot(h, wg_ref[:, c0:c0 + FF_CHUNK])
        act = jax.nn.gelu(uc) * g
        acc = acc + _dot(act.astype(BF16), wd_ref[c0:c0 + FF_CHUNK, :])
    out = x + g2 * acc
    o_ref[...] = out
    if final:
        ms = jnp.mean(out * out, axis=-1, keepdims=True)
        y_ref[...] = out * lax.rsqrt(ms + EPS) * fg_ref[...]
    else:
        y_ref[...] = jnp.zeros(y_ref.shape, F32)


def _ffn_call(x, mod4, n2g, wu, wg, cw, cb, wd, fg, layer, n_ctx_rows, final):
    n = x.shape[0]
    tm = ROW_TILE
    ctx_tiles = n_ctx_rows // tm
    seq_tiles = DEC_SEQ // tm
    hb = tm // HALO
    last_halo = n // HALO - 1

    def mod_map(i):
        return (layer, jnp.where(i < ctx_tiles, 0, 1 + (i - ctx_tiles) // seq_tiles), 0, 0)

    const = lambda shape: pl.BlockSpec(shape, lambda i: tuple(0 for _ in shape))
    y_spec = (pl.BlockSpec((tm, D_MODEL), lambda i: (i, 0)) if final
              else pl.BlockSpec((8, 128), lambda i: (0, 0)))
    y_shape = jax.ShapeDtypeStruct((n, D_MODEL) if final else (8, 128), F32)
    return pl.pallas_call(
        functools.partial(_ffn_kernel, ctx_tiles=ctx_tiles, final=final),
        grid=(n // tm,),
        in_specs=[
            pl.BlockSpec((tm, D_MODEL), lambda i: (i, 0)),
            pl.BlockSpec((HALO, D_MODEL), lambda i: (jnp.maximum(i * hb - 1, 0), 0)),
            pl.BlockSpec((HALO, D_MODEL), lambda i: (jnp.minimum((i + 1) * hb, last_halo), 0)),
            pl.BlockSpec((None, None, 1, 6 * D_MODEL), mod_map),
            const((1, D_MODEL)),
            const((D_MODEL, D_FF)), const((D_MODEL, D_FF)), const((8, D_FF)), const((1, D_FF)),
            const((D_FF, D_MODEL)), const((1, D_MODEL)),
        ],
        out_specs=[pl.BlockSpec((tm, D_MODEL), lambda i: (i, 0)), y_spec],
        out_shape=[jax.ShapeDtypeStruct((n, D_MODEL), F32), y_shape],
        compiler_params=pltpu.CompilerParams(vmem_limit_bytes=VMEM_LIMIT),
        name="conv_ffn",
    )(x, x, x, mod4, n2g, wu, wg, cw, cb, wd, fg)


def _rope_tables(tm):
    t = np.arange(DEC_SEQ)
    row = (t // GRID_W).astype(np.float32)
    col = (t % GRID_W).astype(np.float32)

    def tables(head_dim):
        r = head_dim // 2
        half = r // 2
        inv = (np.float32(ROPE_BASE) ** (-np.arange(half, dtype=np.float32) / np.float32(half))).astype(np.float32)
        lane = np.arange(128) % head_dim
        use_col = lane >= r
        e = lane % r
        first_half = e < half
        freq = inv[e % half]
        pos = np.where(use_col[None, :], col[:, None], row[:, None]).astype(np.float32)
        ang = (pos * freq[None, :]).astype(np.float32)
        cos = np.cos(ang).astype(np.float32)
        sin = np.sin(ang).astype(np.float32)
        sin_a = np.where(first_half[None, :], -sin, 0.0).astype(np.float32)
        sin_b = np.where(first_half[None, :], 0.0, sin).astype(np.float32)
        return cos, sin_a, sin_b

    tab = np.concatenate(list(tables(64)) + list(tables(32)), axis=1)
    ident = np.zeros((tm, ROPE_W), np.float32)
    ident[:, 0:128] = 1.0
    ident[:, 384:512] = 1.0
    return jnp.asarray(np.concatenate([tab, ident], axis=0))


def _in_segments():
    a_q, a_k, a_v, b_q, b_k, b_v, b_r, b_gf, b_gb = 0, 256, 384, 512, 640, 768, 1024, 1280, 1296
    c_q, c_k, c_v, d_q, d_kv, d_kr = 1312, 1568, 1824, 2080, 2272, 2400
    segs = [(a_q + (hh * 2 + g) * 64, 64) for g in range(2) for hh in range(2)]
    segs += [(a_k, 128), (a_v, 128), (b_q, 128), (b_k, 128), (b_v, 256), (b_r, 256)]
    segs += [(c_q, 256), (c_k, 256), (c_v, 256), (d_kv, 128)] + [(d_kr, 32)] * 4
    segs += [(d_q, 192), (b_gf, 16), (b_gb, 16)]
    return segs


GATES_OFF = 2432


def kernel(x_prompt, x_sample, c, cache_a_k, cache_a_v, state_b_fwd, state_b_bwd, cache_c_k, cache_c_v, cache_d_ckv, cache_d_krope, c_ctx, w_mod, b_mod, norm1_g, norm2_g, w_in, a_qnorm_g, a_knorm_g, b_gate_w_fwd, b_gate_b_fwd, b_gate_w_bwd, b_gate_b_bwd, b_onorm_g, c_lq1, c_lk1, c_lq2, c_lk2, c_onorm_g, d_qnorm_g, d_w_uq, d_kvnorm_g, d_w_ukv, w_branch, w_out, w_ffu, w_ffg, conv_w, conv_b, w_ffd, final_g):
    depth = w_in.shape[0]
    nb, nd = x_prompt.shape[0], x_sample.shape[0]
    n_ctx = nb * SEQ
    n = n_ctx + nd * DEC_SEQ
    assert n_ctx % DEC_SEQ == 0 and nd + 1 <= N_MOD_ROWS

    x = jnp.concatenate([x_prompt.reshape(n_ctx, D_MODEL), x_sample.reshape(nd * DEC_SEQ, D_MODEL)], axis=0)
    cvec = jnp.zeros((N_MOD_ROWS, D_MODEL), F32).at[0].set(c_ctx).at[1:1 + nd].set(c)
    mod4 = _mod_call(cvec, w_mod, b_mod).reshape(depth, N_MOD_ROWS, 1, 6 * D_MODEL)

    rope_tab = _rope_tables(ROW_TILE)
    lane = np.arange(256)
    bd = jnp.asarray(lane[:, None] // HEAD_W == lane[None, :] // HEAD_W, BF16)
    tile = jnp.asarray(np.arange(32)[:, None] == np.arange(128)[None, :] % 32, BF16)
    t64 = np.arange(GLA_CHUNK)
    tri = jnp.asarray(np.stack([t64[:, None] >= t64[None, :], t64[:, None] <= t64[None, :]]), BF16)

    segs = [w_in[:, :, s0:s0 + w] for s0, w in _in_segments()]
    segs.append(jnp.zeros((depth, D_MODEL, NZ - sum(w for _, w in _in_segments())), F32))
    w_mix = jnp.concatenate(segs, axis=-1).astype(BF16)
    w_gates = w_in[:, :, GATES_OFF:].astype(BF16)
    wgate = jnp.zeros((depth, 128, 256), F32)
    wgate = wgate.at[:, 64:80, 0:128].set(b_gate_w_fwd).at[:, 80:96, 128:256].set(b_gate_w_bwd).astype(BF16)
    bgate = jnp.concatenate([b_gate_b_fwd, b_gate_b_bwd], axis=-1)[:, None, :]
    dqg = jnp.pad(d_qnorm_g, ((0, 0), (0, 256 - D_Q_RANK)))[:, None, :]
    uq = d_w_uq.reshape(depth, D_Q_RANK, 4, 96)
    wuq = jnp.concatenate([uq[..., :64].reshape(depth, D_Q_RANK, 256), uq[..., 64:].reshape(depth, D_Q_RANK, 128)], axis=-1)
    wuq = jnp.pad(wuq, ((0, 0), (0, 256 - D_Q_RANK), (0, 0))).astype(BF16)
    ukv = d_w_ukv.reshape(depth, 128, 4, 128)
    eye4 = jnp.eye(4, dtype=F32)
    wuk = (ukv[..., :64].transpose(0, 2, 3, 1)[:, :, :, None, :] * eye4[None, :, None, :, None])
    wuk = wuk.reshape(depth, 256, 512).astype(BF16)
    wuv = (ukv[..., 64:].transpose(0, 2, 1, 3)[:, :, :, None, :] * eye4[None, :, None, :, None])
    wuv = wuv.reshape(depth, 512, 256).astype(BF16)
    wb0 = w_branch[:, 0].reshape(depth, 2, 2, 64, D_MODEL).transpose(0, 2, 1, 3, 4).reshape(depth, 256, D_MODEL)
    wb = jnp.concatenate([wb0[:, None], w_branch[:, 1:]], axis=1).astype(BF16)
    wout = w_out.astype(BF16)
    wu, wg, wd = w_ffu.astype(BF16), w_ffg.astype(BF16), w_ffd.astype(BF16)
    cw = jnp.pad(conv_w, ((0, 0), (0, 8 - conv_w.shape[1]), (0, 0)))
    aqg = jnp.tile(a_qnorm_g, (1, 4))[:, None, :]
    akg = jnp.tile(a_knorm_g, (1, 2))[:, None, :]
    bong = jnp.tile(b_onorm_g, (1, 4))[:, None, :]
    cong = jnp.tile(c_onorm_g, (1, 4))[:, None, :]
    lamv = jnp.stack([c_lq1, c_lk1, c_lq2, c_lk2], axis=1)
    caches = (cache_a_k.reshape(nd, depth, PAST_LEN, 128), cache_a_v.reshape(nd, depth, PAST_LEN, 128),
              cache_c_k.reshape(nd, depth, PAST_LEN, 256), cache_c_v.reshape(nd, depth, PAST_LEN, 256),
              cache_d_ckv, cache_d_krope)
    expand = lambda s: (s.transpose(0, 1, 2, 4, 3)[:, :, :, :, None, :]
                        * eye4[None, None, :, None, :, None]).reshape(nd, depth, 256, 128)
    s0f_all, s0b_all = expand(state_b_fwd), expand(state_b_bwd)

    outs = {k: [] for k in ('ak', 'av', 'sf', 'sb', 'ck', 'cv', 'ckv', 'kr')}
    y = None
    for l in range(depth):
        lam_init = 0.8 - 0.6 * math.exp(-0.3 * l)
        fa, qa = _inproj_call(x, mod4, norm1_g[l][None], w_mix[l], rope_tab, aqg[l], akg[l], bd, wgate[l],
                              bgate[l], dqg[l], wuq[l], wuk[l], d_kvnorm_g[l][None], l, n_ctx)
        oacd = _attn_calls(fa, qa, caches, lamv[l], cong[l], wuv[l], bd, tile, l, lam_init, n_ctx)
        ob, sf, sb = _gla_calls(fa, s0f_all[:, l], s0b_all[:, l], bong[l], bd, tri, n_ctx)
        x1 = _merge_call(x, mod4, norm1_g[l][None], w_gates[l], oacd, ob, wb[l], wout[l], l, n_ctx)
        x, y = _ffn_call(x1, mod4, norm2_g[l][None], wu[l], wg[l], cw[l], conv_b[l][None], wd[l],
                         final_g[None], l, n_ctx, l == depth - 1)
        fc = fa[:n_ctx]
        outs['ak'].append(fc[:, F_AK:F_AK + 128].reshape(nb, SEQ, 2, 64))
        outs['av'].append(fc[:, F_AV:F_AV + 128].reshape(nb, SEQ, 2, 64))
        outs['ck'].append(fc[:, F_CK:F_CK + 256].reshape(nb, SEQ, 4, 2, 32))
        outs['cv'].append(fc[:, F_CV:F_CV + 256].reshape(nb, SEQ, 4, 64))
        outs['ckv'].append(fc[:, F_DKV:F_DKV + 128].reshape(nb, SEQ, 128))
        outs['kr'].append(fc[:, F_KR:F_KR + 32].reshape(nb, SEQ, 32))
        diag = lambda s: jnp.stack([s[:, hh * 64:(hh + 1) * 64, hh * 32:(hh + 1) * 32] for hh in range(4)],
                                   axis=1).transpose(0, 1, 3, 2)
        outs['sf'].append(diag(sf))
        outs['sb'].append(diag(sb))

    stack = lambda k: jnp.stack(outs[k], axis=1)
    y_prompt = y[:n_ctx].reshape(nb, SEQ, D_MODEL)
    y_sample = y[n_ctx:].reshape(nd, DEC_SEQ, D_MODEL)
    return (y_prompt, y_sample, stack('ak'), stack('av'), stack('sf'), stack('sb'), stack('ck'), stack('cv'),
            stack('ckv'), stack('kr'))
```

```python
import functools
import math

import numpy as np
import jax
import jax.numpy as jnp
from jax import lax
from jax.experimental import pallas as pl
from jax.experimental.pallas import tpu as pltpu

F32 = jnp.float32
BF16 = jnp.bfloat16

D_MODEL = 1024
SEQ = 256
DEC_SEQ = 1024
PAST_LEN = 512
GRID_W = 64
ROPE_BASE = 10000.0
EPS = 1e-6
LOG2E = math.log2(math.e)
HEAD_W = 64
BRANCH_WIDTH = 256
GLA_KEY_W = 32
GLA_CHUNK = 64
GLA_SUB = 16
GLA_BLOCK = 256
GLA_GATE_NORM = 16.0
D_Q_RANK = 192
D_FF = 2816
FF_CHUNK = 256
N_MOD_ROWS = 16

ROW_TILE = 512
Q_TILE = 256
HALO = 16
VMEM_LIMIT = 56 * 1024 * 1024

NZ = 2560
Z_AQ, Z_AK, Z_AV, Z_BQ, Z_BK, Z_BV, Z_BR = 0, 256, 384, 512, 640, 768, 1024
Z_CQ, Z_CK, Z_CV, Z_DKV, Z_KR, Z_DQ, Z_TAIL = 1280, 1536, 1792, 2048, 2176, 2304, 2432
F_W = 1024
F_BQ, F_BK, F_BV, F_BR, F_LGF, F_LGB = 0, 128, 256, 512, 768, 896
KV_W = 1024
KV_AK, KV_AV, KV_CK, KV_CV, KV_DKV, KV_KR = 0, 128, 256, 512, 768, 896
Q_W = 1152
Q_AQ, Q_CQ, Q_DLAT, Q_DQR = 0, 256, 512, 1024
ROPE_W = 768


def _dot(a, b):
    return jnp.dot(a, b, preferred_element_type=F32)


def _dot_nt(a, b):
    return lax.dot_general(a, b, (((1,), (1,)), ((), ())), preferred_element_type=F32)


def _dot_tn(a, b):
    return lax.dot_general(a, b, (((0,), (0,)), ((), ())), preferred_element_type=F32)


def _split3(a):
    hi = a.astype(BF16)
    r1 = a - hi.astype(F32)
    mid = r1.astype(BF16)
    lo = (r1 - mid.astype(F32)).astype(BF16)
    return hi, mid, lo


def _split_dot(a, b01):
    hi, mid, lo = _split3(a)
    return _dot(hi, b01) + _dot(mid, b01) + _dot(lo, b01)


def _split_dot_tn(b01, a):
    hi, mid, lo = _split3(a)
    return _dot(b01, hi) + _dot(b01, mid) + _dot(b01, lo)


def _group_rms(v, bd, width, gain):
    ss = _split_dot(v * v, bd)
    return v * lax.rsqrt(ss * (1.0 / width) + EPS) * gain


def _rope128(v, cos, sin_a, sin_b, shift):
    return v * cos + pltpu.roll(v, 128 - shift, 1) * sin_a + pltpu.roll(v, shift, 1) * sin_b


def _modulated_norm(x, gain, shift, scale):
    ms = jnp.mean(x * x, axis=-1, keepdims=True)
    return (x * lax.rsqrt(ms + EPS) * gain) * (1.0 + scale) + shift


def _lane_mask(width, lo, hi):
    lane = lax.broadcasted_iota(jnp.int32, (1, width), 1)
    return (lane >= lo) & (lane < hi)


def _const_spec(shape, grid_rank):
    zeros = tuple(0 for _ in shape)
    if grid_rank == 1:
        return pl.BlockSpec(shape, lambda i: zeros)
    return pl.BlockSpec(shape, lambda b, i: zeros)


def _mod_kernel(cv_ref, w_ref, b_ref, o_ref):
    cv = cv_ref[...]
    a = cv * jax.nn.sigmoid(cv)
    o_ref[...] = _dot(a.astype(BF16), w_ref[...].astype(BF16)) + b_ref[...]


def _mod_call(cvec, w_mod, b_mod):
    depth = w_mod.shape[0]
    nblk = 4
    bw = 6 * D_MODEL // nblk
    return pl.pallas_call(
        _mod_kernel,
        grid=(depth, nblk),
        in_specs=[
            pl.BlockSpec((N_MOD_ROWS, D_MODEL), lambda l, j: (0, 0)),
            pl.BlockSpec((None, D_MODEL, bw), lambda l, j: (l, 0, j)),
            pl.BlockSpec((None, 1, bw), lambda l, j: (l, 0, j)),
        ],
        out_specs=pl.BlockSpec((None, N_MOD_ROWS, bw), lambda l, j: (l, 0, j)),
        out_shape=jax.ShapeDtypeStruct((depth, N_MOD_ROWS, 6 * D_MODEL), F32),
        compiler_params=pltpu.CompilerParams(vmem_limit_bytes=VMEM_LIMIT),
        name="mod_vectors",
    )(cvec, w_mod, b_mod.reshape(depth, 1, 6 * D_MODEL))


def _inproj_body(x_ref, mod_ref, n1g_ref, w_ref, rope_ref, aqg_ref, akg_ref, bd_ref, wgate_ref, bgate_ref,
                 dqg_ref, wuq_ref, wuk_ref, dkvg_ref, f_ref, kv_ref, q_ref, cache_refs):
    x = x_ref[...]
    mod = mod_ref[...]
    h = _modulated_norm(x, n1g_ref[...], mod[:, 0:D_MODEL], mod[:, D_MODEL:2 * D_MODEL])
    z = _dot(h.astype(BF16), w_ref[...])
    bd = bd_ref[...]
    bd128 = bd[0:128, 0:128]
    if rope_ref is None:
        rope64 = rope32 = lambda v: v
    else:
        rope = rope_ref[...]
        cos64, sa64, sb64 = rope[:, 0:128], rope[:, 128:256], rope[:, 256:384]
        cos32, sa32, sb32 = rope[:, 384:512], rope[:, 512:640], rope[:, 640:768]
        rope64 = lambda v: _rope128(v, cos64, sa64, sb64, 16)
        rope32 = lambda v: _rope128(v, cos32, sa32, sb32, 8)

    def emit_kv(col, cache_idx, val):
        kv_ref[:, col:col + val.shape[1]] = val.astype(BF16)
        if cache_refs is not None:
            ref = cache_refs[cache_idx]
            for j in range(val.shape[0] // SEQ):
                ref[j] = val[j * SEQ:(j + 1) * SEQ, 0:ref.shape[-1]]

    aq = _group_rms(z[:, Z_AQ:Z_AQ + 256], bd, HEAD_W, aqg_ref[...])
    a_scale = HEAD_W ** -0.5 * LOG2E
    q_ref[:, Q_AQ:Q_AQ + 128] = (rope64(aq[:, 0:128]) * a_scale).astype(BF16)
    q_ref[:, Q_AQ + 128:Q_AQ + 256] = (rope64(aq[:, 128:256]) * a_scale).astype(BF16)
    emit_kv(KV_AK, 0, rope64(_group_rms(z[:, Z_AK:Z_AK + 128], bd128, HEAD_W, akg_ref[...])))
    emit_kv(KV_AV, 1, z[:, Z_AV:Z_AV + 128])

    f_ref[:, F_BQ:F_BK] = z[:, Z_BQ:Z_BK] * (GLA_KEY_W ** -0.5)
    f_ref[:, F_BK:F_BV] = z[:, Z_BK:Z_BV]
    f_ref[:, F_BV:F_BR] = z[:, Z_BV:Z_BR]
    f_ref[:, F_BR:F_LGF] = z[:, Z_BR:Z_CQ]
    pre = _dot(z[:, Z_TAIL:NZ].astype(BF16), wgate_ref[...]) + bgate_ref[...]
    logsig = jnp.minimum(pre, 0.0) - jnp.log1p(jnp.exp(-jnp.abs(pre)))
    f_ref[:, F_LGF:F_W] = logsig / GLA_GATE_NORM

    c_scale = 32 ** -0.5 * LOG2E
    q_ref[:, Q_CQ:Q_CQ + 128] = (rope32(z[:, Z_CQ:Z_CQ + 128]) * c_scale).astype(BF16)
    q_ref[:, Q_CQ + 128:Q_CQ + 256] = (rope32(z[:, Z_CQ + 128:Z_CQ + 256]) * c_scale).astype(BF16)
    emit_kv(KV_CK, 2, jnp.concatenate([rope32(z[:, Z_CK:Z_CK + 128]), rope32(z[:, Z_CK + 128:Z_CK + 256])],
                                      axis=-1))
    emit_kv(KV_CV, 3, z[:, Z_CV:Z_DKV])

    dkv = z[:, Z_DKV:Z_DKV + 128]
    ms = jnp.mean(dkv * dkv, axis=-1, keepdims=True)
    emit_kv(KV_DKV, 4, dkv * lax.rsqrt(ms + EPS) * dkvg_ref[...])
    emit_kv(KV_KR, 5, rope32(z[:, Z_KR:Z_KR + 128]))
    zq = z[:, Z_DQ:NZ]
    zq = jnp.where(_lane_mask(NZ - Z_DQ, 0, D_Q_RANK), zq, 0.0)
    msq = jnp.sum(zq * zq, axis=-1, keepdims=True) * (1.0 / D_Q_RANK)
    qn = zq * lax.rsqrt(msq + EPS) * dqg_ref[...]
    dq = _dot(qn.astype(BF16), wuq_ref[...])
    d_scale = 96 ** -0.5 * LOG2E
    q_ref[:, Q_DLAT:Q_DQR] = (_dot(dq[:, 0:256].astype(BF16), wuk_ref[...]) * d_scale).astype(BF16)
    q_ref[:, Q_DQR:Q_W] = (rope32(dq[:, 256:384]) * d_scale).astype(BF16)


def _inproj_ctx_kernel(x_ref, mod_ref, n1g_ref, w_ref, aqg_ref, akg_ref, bd_ref, wgate_ref, bgate_ref,
                       dqg_ref, wuq_ref, wuk_ref, dkvg_ref, *rest):
    outs = rest[-9:]
    _inproj_body(x_ref, mod_ref, n1g_ref, w_ref, None, aqg_ref, akg_ref, bd_ref, wgate_ref, bgate_ref,
                 dqg_ref, wuq_ref, wuk_ref, dkvg_ref, outs[0], outs[1], outs[2], outs[3:])


def _inproj_lat_kernel(x_ref, mod_ref, n1g_ref, w_ref, rope_ref, aqg_ref, akg_ref, bd_ref, wgate_ref,
                       bgate_ref, dqg_ref, wuq_ref, wuk_ref, dkvg_ref, pf_ref, pkv_ref, pq_ref,
                       f_ref, kv_ref, q_ref):
    del pf_ref, pkv_ref, pq_ref
    _inproj_body(x_ref, mod_ref, n1g_ref, w_ref, rope_ref, aqg_ref, akg_ref, bd_ref, wgate_ref, bgate_ref,
                 dqg_ref, wuq_ref, wuk_ref, dkvg_ref, f_ref, kv_ref, q_ref, None)


CACHE_WIDTHS = (128, 128, 256, 256, 128, 32)


def _inproj_calls(x, mod4, n1g, w_mix, rope_tab, small, prev_caches, layer, depth, n_ctx_rows):
    n = x.shape[0]
    tm = ROW_TILE
    ctx_tiles = n_ctx_rows // tm
    seq_tiles = DEC_SEQ // tm
    spt = tm // SEQ
    n_ctx_seq = n_ctx_rows // SEQ
    const = lambda shape: _const_spec(shape, 1)
    small_specs = [const((1, 256)), const((1, 128)), const((256, 256)), const((128, 256)), const((1, 256)),
                   const((1, 256)), const((256, 384)), const((256, 512)), const((1, 128))]
    act_shapes = [jax.ShapeDtypeStruct((n, F_W), F32), jax.ShapeDtypeStruct((n, KV_W), BF16),
                  jax.ShapeDtypeStruct((n, Q_W), BF16)]
    act_specs = lambda off: [pl.BlockSpec((tm, w), lambda i: (i + off, 0)) for w in (F_W, KV_W, Q_W)]
    cparams = pltpu.CompilerParams(vmem_limit_bytes=VMEM_LIMIT)
    any_spec = pl.BlockSpec(memory_space=pl.ANY)

    n_in = 4 + len(small)
    alias = {} if prev_caches is None else {n_in + k: 3 + k for k in range(6)}
    res = pl.pallas_call(
        _inproj_ctx_kernel,
        grid=(ctx_tiles,),
        in_specs=[
            pl.BlockSpec((tm, D_MODEL), lambda i: (i, 0)),
            pl.BlockSpec((None, None, 1, 6 * D_MODEL), lambda i: (layer, 0, 0, 0)),
            const((1, D_MODEL)),
            const((D_MODEL, NZ)),
        ] + small_specs + ([] if prev_caches is None else [any_spec] * 6),
        out_specs=act_specs(0) + [pl.BlockSpec((spt, None, SEQ, w), lambda i: (i, layer, 0, 0))
                                  for w in CACHE_WIDTHS],
        out_shape=act_shapes + [jax.ShapeDtypeStruct((n_ctx_seq, depth, SEQ, w), F32) for w in CACHE_WIDTHS],
        input_output_aliases=alias,
        compiler_params=cparams,
        name="in_projection_context",
    )(x, mod4, n1g, w_mix, *small, *([] if prev_caches is None else prev_caches))
    f_c, kv_c, q_c, caches = res[0], res[1], res[2], tuple(res[3:])

    fa, kva, qa = pl.pallas_call(
        _inproj_lat_kernel,
        grid=((n - n_ctx_rows) // tm,),
        in_specs=[
            pl.BlockSpec((tm, D_MODEL), lambda i: (i + ctx_tiles, 0)),
            pl.BlockSpec((None, None, 1, 6 * D_MODEL), lambda i: (layer, 1 + i // seq_tiles, 0, 0)),
            const((1, D_MODEL)),
            const((D_MODEL, NZ)),
            pl.BlockSpec((tm, ROPE_W), lambda i: (i % seq_tiles, 0)),
        ] + small_specs + [any_spec] * 3,
        out_specs=act_specs(ctx_tiles),
        out_shape=act_shapes,
        input_output_aliases={n_in + 1: 0, n_in + 2: 1, n_in + 3: 2},
        compiler_params=cparams,
        name="in_projection_latent",
    )(x, mod4, n1g, w_mix, rope_tab, *small, f_c, kv_c, q_c)
    return fa, kva, qa, caches


def _exp2_parts(parts):
    m = parts[0].max(axis=-1, keepdims=True)
    for s in parts[1:]:
        m = jnp.maximum(m, s.max(axis=-1, keepdims=True))
    return [jnp.exp2(s - m) for s in parts]


def _attn_body(q_ref, kva_ref, ck_ref, cv_ref, dkv_ref, cache_refs, lam_ref, cong_ref, wuv_ref, bd_ref,
               tile_ref, o_ref, lam_init):
    q = q_ref[...]
    kva = kva_ref[...]
    ak, av = [kva[:, 0:128]], [kva[:, 128:256]]
    ck, cv, dk = [ck_ref[...]], [cv_ref[...]], [dkv_ref[...]]
    if cache_refs is not None:
        cak_ref, cav_ref, cck_ref, ccv_ref, cckv_ref, ckr_ref = cache_refs
        ak.insert(0, cak_ref[...].astype(BF16))
        av.insert(0, cav_ref[...].astype(BF16))
        ck.insert(0, cck_ref[...].astype(BF16))
        cv.insert(0, ccv_ref[...].astype(BF16))
        kr4 = _dot(ckr_ref[...].astype(BF16), tile_ref[...])
        dk.insert(0, jnp.concatenate([cckv_ref[...].astype(BF16), kr4.astype(BF16)], axis=-1))
    zero = jnp.zeros((), BF16)
    one = jnp.ones((), BF16)

    oa = [None, None]
    for hh in range(2):
        m = _lane_mask(128, hh * HEAD_W, (hh + 1) * HEAD_W)
        v1 = [jnp.where(m, v, one) for v in av]
        for g in range(2):
            qm = jnp.where(m, q[:, Q_AQ + g * 128:Q_AQ + (g + 1) * 128], zero)
            ps = _exp2_parts([_dot_nt(qm, k) for k in ak])
            o = sum(_dot(p.astype(BF16), v) for p, v in zip(ps, v1))
            o = jnp.where(m, o / pltpu.roll(o, HEAD_W, 1), 0.0)
            oa[g] = o if oa[g] is None else oa[g] + o
    o_ref[:, 0:128] = oa[0]
    o_ref[:, 128:256] = oa[1]

    lv = lam_ref[...]
    lam = (jnp.exp(jnp.sum(lv[0:1] * lv[1:2], axis=-1, keepdims=True))
           - jnp.exp(jnp.sum(lv[2:3] * lv[3:4], axis=-1, keepdims=True)) + lam_init)
    oc = [None, None]
    for hh in range(4):
        v1 = [jnp.where(_lane_mask(256, hh * HEAD_W, (hh + 1) * HEAD_W), v, one) for v in cv]
        half = (hh // 2) * 128
        res = []
        for j in range(2):
            u = hh * 2 + j
            c0 = (u // 4) * 128
            m = _lane_mask(128, (u % 4) * 32, (u % 4 + 1) * 32)
            qm = jnp.where(m, q[:, Q_CQ + c0:Q_CQ + c0 + 128], zero)
            ps = _exp2_parts([_dot_nt(qm, k[:, c0:c0 + 128]) for k in ck])
            o = sum(_dot(p.astype(BF16), v) for p, v in zip(ps, v1))
            res.append(o[:, half:half + 128] / o[:, 128 - half:256 - half])
        o = jnp.where(_lane_mask(128, (hh % 2) * HEAD_W, (hh % 2 + 1) * HEAD_W), res[0] - lam * res[1], 0.0)
        oc[hh // 2] = o if oc[hh // 2] is None else oc[hh // 2] + o
    oc = jnp.concatenate(oc, axis=-1)
    o_ref[:, 256:512] = _group_rms(oc, bd_ref[...], HEAD_W, cong_ref[...]) * (1.0 - lam_init)

    vlat = [jnp.where(_lane_mask(256, 0, 128), k, one) for k in dk]
    lat = []
    for hh in range(4):
        m = _lane_mask(128, hh * 32, (hh + 1) * 32)
        qh = jnp.concatenate([q[:, Q_DLAT + hh * 128:Q_DLAT + (hh + 1) * 128],
                              jnp.where(m, q[:, Q_DQR:Q_W], zero)], axis=-1)
        ps = _exp2_parts([_dot_nt(qh, k) for k in dk])
        o = sum(_dot(p.astype(BF16), v) for p, v in zip(ps, vlat))
        lat.append((o[:, 0:128] / o[:, 128:256]).astype(BF16))
    o_ref[:, 512:768] = _dot(jnp.concatenate(lat, axis=-1), wuv_ref[...])


def _attn_ctx_kernel(q_ref, kva_ref, ck_ref, cv_ref, dkv_ref, lam_ref, cong_ref, wuv_ref, bd_ref,
                     tile_ref, o_ref, *, lam_init):
    _attn_body(q_ref, kva_ref, ck_ref, cv_ref, dkv_ref, None, lam_ref, cong_ref, wuv_ref, bd_ref,
               tile_ref, o_ref, lam_init)


def _attn_lat_kernel(q_ref, kva_ref, ck_ref, cv_ref, dkv_ref, cak_ref, cav_ref, cck_ref, ccv_ref,
                     cckv_ref, ckr_ref, lam_ref, cong_ref, wuv_ref, bd_ref, tile_ref, prev_ref, o_ref,
                     *, lam_init):
    del prev_ref
    _attn_body(q_ref, kva_ref, ck_ref, cv_ref, dkv_ref,
               (cak_ref, cav_ref, cck_ref, ccv_ref, cckv_ref, ckr_ref), lam_ref, cong_ref, wuv_ref,
               bd_ref, tile_ref, o_ref, lam_init)


def _attn_calls(kva, qa, caches, lamv, cong, wuv, bd, tile, layer, lam_init, n_ctx_rows):
    n = kva.shape[0]
    n_ctx_seq = n_ctx_rows // SEQ
    n_lat_seq = (n - n_ctx_rows) // DEC_SEQ
    param_shapes = [(4, 32), (1, 256), (512, 256), (256, 256), (32, 128)]
    out_shape = jax.ShapeDtypeStruct((n, 3 * BRANCH_WIDTH), F32)
    cparams = pltpu.CompilerParams(vmem_limit_bytes=VMEM_LIMIT)

    o_ctx = pl.pallas_call(
        functools.partial(_attn_ctx_kernel, lam_init=lam_init),
        grid=(n_ctx_seq,),
        in_specs=[
            pl.BlockSpec((SEQ, Q_W), lambda b: (b, 0)),
            pl.BlockSpec((SEQ, 256), lambda b: (b, KV_AK // 256)),
            pl.BlockSpec((SEQ, 256), lambda b: (b, KV_CK // 256)),
            pl.BlockSpec((SEQ, 256), lambda b: (b, KV_CV // 256)),
            pl.BlockSpec((SEQ, 256), lambda b: (b, KV_DKV // 256)),
        ] + [_const_spec(s, 1) for s in param_shapes],
        out_specs=pl.BlockSpec((SEQ, 3 * BRANCH_WIDTH), lambda b: (b, 0)),
        out_shape=out_shape,
        compiler_params=cparams,
        name="attention_context",
    )(qa, kva, kva, kva, kva, lamv, cong, wuv, bd, tile)

    qt = Q_TILE
    q_per_seq = DEC_SEQ // qt
    q0 = n_ctx_rows // qt
    s0 = n_ctx_rows // DEC_SEQ
    cak, cav, cck, ccv, cckv, ckr = caches
    cache_spec = lambda w: pl.BlockSpec((None, None, PAST_LEN, w), lambda b, i: (b, layer, 0, 0))
    o_all = pl.pallas_call(
        functools.partial(_attn_lat_kernel, lam_init=lam_init),
        grid=(n_lat_seq, q_per_seq),
        in_specs=[
            pl.BlockSpec((qt, Q_W), lambda b, i: (q0 + b * q_per_seq + i, 0)),
            pl.BlockSpec((DEC_SEQ, 256), lambda b, i: (s0 + b, KV_AK // 256)),
            pl.BlockSpec((DEC_SEQ, 256), lambda b, i: (s0 + b, KV_CK // 256)),
            pl.BlockSpec((DEC_SEQ, 256), lambda b, i: (s0 + b, KV_CV // 256)),
            pl.BlockSpec((DEC_SEQ, 256), lambda b, i: (s0 + b, KV_DKV // 256)),
            cache_spec(128), cache_spec(128), cache_spec(256), cache_spec(256), cache_spec(128),
            cache_spec(32),
        ] + [_const_spec(s, 2) for s in param_shapes] + [pl.BlockSpec(memory_space=pl.ANY)],
        out_specs=pl.BlockSpec((qt, 3 * BRANCH_WIDTH), lambda b, i: (q0 + b * q_per_seq + i, 0)),
        out_shape=out_shape,
        input_output_aliases={16: 0},
        compiler_params=cparams,
        name="attention_latent",
    )(qa, kva, kva, kva, kva, cak, cav, cck, ccv, cckv, ckr, lamv, cong, wuv, bd, tile, o_ctx)
    return o_all


def _gla_chunk(q, k, v, bc, st, reverse, masks):
    c, sub, nsub = GLA_CHUNK, GLA_SUB, GLA_CHUNK // GLA_SUB
    keep, qmask, src_row, blk = masks
    edge = bc[0:1] if reverse else bc[c - 1:c]
    qs, kd = [], []
    for i in range(nsub):
        r0 = i * sub
        ref = bc[r0 + sub - 1:r0 + sub] if reverse else bc[r0:r0 + 1]
        qd = q[r0:r0 + sub] * jnp.exp(bc[r0:r0 + sub] - ref)
        qs.append(jnp.where(qmask, jnp.concatenate([qd] * 4, axis=0), 0.0))
        in_range = (src_row >= r0) if reverse else (src_row < r0 + sub)
        kd.append(k * jnp.exp(jnp.where(in_range, ref - bc, 0.0)))
    att = _dot_nt(jnp.concatenate(qs, axis=0).astype(BF16), jnp.concatenate(kd, axis=0).astype(BF16))
    att = jnp.where(keep, att, 0.0).astype(BF16)
    vb = v.astype(BF16)
    pv = _dot(att, jnp.concatenate([vb] * nsub, axis=0))
    out_blocks = []
    for i in range(nsub):
        o = None
        for hh in range(4):
            r0 = i * c + hh * sub
            part = jnp.where(_lane_mask(256, hh * HEAD_W, (hh + 1) * HEAD_W), pv[r0:r0 + sub], 0.0)
            o = part if o is None else o + part
        out_blocks.append(o)
    o_intra = jnp.concatenate(out_blocks, axis=0)
    q_dec = q * jnp.exp(bc)
    k_dec = k * jnp.exp(edge - bc)
    o_inter = _dot_nt(q_dec.astype(BF16), st.astype(BF16))
    upd = _dot_tn(vb, k_dec.astype(BF16))
    st_new = st * jnp.exp(edge) + jnp.where(blk, upd, 0.0)
    return o_intra + o_inter, st_new


def _gla_masks(reverse):
    c, sub = GLA_CHUNK, GLA_SUB
    row = lax.broadcasted_iota(jnp.int32, (4 * c, 4 * c), 0)
    col = lax.broadcasted_iota(jnp.int32, (4 * c, 4 * c), 1)
    t = (row // c) * sub + row % sub
    s = col % c
    keep = (row // c == col // c) & ((s >= t) if reverse else (s <= t))
    qmask = (lax.broadcasted_iota(jnp.int32, (4 * sub, 128), 0) // sub
             == lax.broadcasted_iota(jnp.int32, (4 * sub, 128), 1) // GLA_KEY_W)
    src_row = lax.broadcasted_iota(jnp.int32, (c, 1), 0)
    blk = (lax.broadcasted_iota(jnp.int32, (256, 128), 0) // HEAD_W
           == lax.broadcasted_iota(jnp.int32, (256, 128), 1) // GLA_KEY_W)
    return keep, qmask, src_row, blk


def _gla_body(qk_ref, v_ref, r_ref, lg_ref, s0f, s0b, g_ref, bd_ref, tri_ref, o_ref, accf_ref, accb_ref,
              bcf_ref, bcb_ref, seq_len, unroll):
    nc = seq_len // GLA_CHUNK
    for r0 in range(0, seq_len, GLA_BLOCK):
        lg = lg_ref[r0:r0 + GLA_BLOCK, :]
        for d, bc_ref in ((0, bcf_ref), (1, bcb_ref)):
            res = _dot(tri_ref[d], jnp.concatenate(_split3(lg[:, d * 128:(d + 1) * 128]), axis=-1))
            bc_ref[r0:r0 + GLA_BLOCK, :] = res[:, 0:128] + res[:, 128:256] + res[:, 256:384]
    masks_f = _gla_masks(False)
    masks_b = _gla_masks(True)

    def step(ci, carry):
        sf, sb = carry
        rows = pl.ds(pl.multiple_of(ci * GLA_CHUNK, GLA_CHUNK), GLA_CHUNK)
        o, sf = _gla_chunk(qk_ref[rows, 0:128], qk_ref[rows, 128:256], v_ref[rows, :], bcf_ref[rows, :],
                           sf, False, masks_f)
        accf_ref[rows, :] = o
        rows = pl.ds(pl.multiple_of((nc - 1 - ci) * GLA_CHUNK, GLA_CHUNK), GLA_CHUNK)
        o, sb = _gla_chunk(qk_ref[rows, 0:128], qk_ref[rows, 128:256], v_ref[rows, :], bcb_ref[rows, :],
                           sb, True, masks_b)
        accb_ref[rows, :] = o
        return sf, sb

    sf, sb = lax.fori_loop(0, nc, step, (s0f, s0b), unroll=unroll)
    r = r_ref[...]
    o_ref[...] = (_group_rms(accf_ref[...] + accb_ref[...], bd_ref[...], HEAD_W, g_ref[...])
                  * (r * jax.nn.sigmoid(r)))
    return sf, sb


def _gla_ctx_kernel(qk_ref, v_ref, r_ref, lg_ref, g_ref, bd_ref, tri_ref, o_ref, sf_ref, sb_ref, *scratch):
    zero = jnp.zeros((256, 128), F32)
    sf, sb = _gla_body(qk_ref, v_ref, r_ref, lg_ref, zero, zero, g_ref, bd_ref, tri_ref, o_ref, *scratch,
                       SEQ, True)
    sf_ref[...] = sf
    sb_ref[...] = sb


def _gla_lat_kernel(qk_ref, v_ref, r_ref, lg_ref, s0f_ref, s0b_ref, g_ref, bd_ref, tri_ref, prev_ref,
                    o_ref, *scratch):
    del prev_ref
    _gla_body(qk_ref, v_ref, r_ref, lg_ref, s0f_ref[...], s0b_ref[...], g_ref, bd_ref, tri_ref, o_ref,
              *scratch, DEC_SEQ, 2)


def _gla_calls(fa, s0f_t, s0b_t, bong, bd, tri, n_ctx_rows):
    n = fa.shape[0]
    n_ctx_seq = n_ctx_rows // SEQ
    n_lat_seq = (n - n_ctx_rows) // DEC_SEQ
    cparams = pltpu.CompilerParams(vmem_limit_bytes=VMEM_LIMIT)
    const = lambda shape: _const_spec(shape, 1)
    out_shape = jax.ShapeDtypeStruct((n, BRANCH_WIDTH), F32)
    st_shape = jax.ShapeDtypeStruct((n_ctx_seq, 256, 128), F32)
    o_ctx, sf, sb = pl.pallas_call(
        _gla_ctx_kernel,
        grid=(n_ctx_seq,),
        in_specs=[
            pl.BlockSpec((SEQ, 256), lambda b: (b, F_BQ // 256)),
            pl.BlockSpec((SEQ, 256), lambda b: (b, F_BV // 256)),
            pl.BlockSpec((SEQ, 256), lambda b: (b, F_BR // 256)),
            pl.BlockSpec((SEQ, 256), lambda b: (b, F_LGF // 256)),
            const((1, 256)), const((256, 256)), const((2, GLA_BLOCK, GLA_BLOCK)),
        ],
        out_specs=[pl.BlockSpec((SEQ, BRANCH_WIDTH), lambda b: (b, 0)),
                   pl.BlockSpec((None, 256, 128), lambda b: (b, 0, 0)),
                   pl.BlockSpec((None, 256, 128), lambda b: (b, 0, 0))],
        out_shape=[out_shape, st_shape, st_shape],
        scratch_shapes=[pltpu.VMEM((SEQ, BRANCH_WIDTH), F32), pltpu.VMEM((SEQ, BRANCH_WIDTH), F32),
                        pltpu.VMEM((SEQ, 128), F32), pltpu.VMEM((SEQ, 128), F32)],
        compiler_params=cparams,
        name="gla_context",
    )(fa, fa, fa, fa, bong, bd, tri)

    s0 = n_ctx_rows // DEC_SEQ
    o_all = pl.pallas_call(
        _gla_lat_kernel,
        grid=(n_lat_seq,),
        in_specs=[
            pl.BlockSpec((DEC_SEQ, 256), lambda b: (s0 + b, F_BQ // 256)),
            pl.BlockSpec((DEC_SEQ, 256), lambda b: (s0 + b, F_BV // 256)),
            pl.BlockSpec((DEC_SEQ, 256), lambda b: (s0 + b, F_BR // 256)),
            pl.BlockSpec((DEC_SEQ, 256), lambda b: (s0 + b, F_LGF // 256)),
            pl.BlockSpec((None, 256, 128), lambda b: (b, 0, 0)),
            pl.BlockSpec((None, 256, 128), lambda b: (b, 0, 0)),
            const((1, 256)), const((256, 256)), const((2, GLA_BLOCK, GLA_BLOCK)),
            pl.BlockSpec(memory_space=pl.ANY),
        ],
        out_specs=pl.BlockSpec((DEC_SEQ, BRANCH_WIDTH), lambda b: (s0 + b, 0)),
        out_shape=out_shape,
        scratch_shapes=[pltpu.VMEM((DEC_SEQ, BRANCH_WIDTH), F32), pltpu.VMEM((DEC_SEQ, BRANCH_WIDTH), F32),
                        pltpu.VMEM((DEC_SEQ, 128), F32), pltpu.VMEM((DEC_SEQ, 128), F32)],
        input_output_aliases={9: 0},
        compiler_params=cparams,
        name="gla_latent",
    )(fa, fa, fa, fa, s0f_t, s0b_t, bong, bd, tri, o_ctx)
    return o_all, sf, sb


def _merge_kernel(x_ref, mod_ref, n1g_ref, wg_ref, oacd_ref, ob_ref, wb_ref, wout_ref, o_ref):
    x = x_ref[...]
    mod = mod_ref[...]
    h = _modulated_norm(x, n1g_ref[...], mod[:, 0:D_MODEL], mod[:, D_MODEL:2 * D_MODEL]).astype(BF16)
    oacd = oacd_ref[...]
    branches = [oacd[:, 0:256], ob_ref[...], oacd[:, 256:512], oacd[:, 512:768]]
    acc = None
    for j in range(4):
        gate = jax.nn.sigmoid(_dot(h, wg_ref[:, j * D_MODEL:(j + 1) * D_MODEL]))
        term = gate * _dot(branches[j].astype(BF16), wb_ref[j])
        acc = term if acc is None else acc + term
    mix = _dot(acc.astype(BF16), wout_ref[...])
    o_ref[...] = x + mod[:, 2 * D_MODEL:3 * D_MODEL] * mix


def _mod_map(layer, ctx_tiles, seq_tiles):
    return lambda i: (layer, jnp.where(i < ctx_tiles, 0, 1 + (i - ctx_tiles) // seq_tiles), 0, 0)


def _merge_call(x, mod4, n1g, wgates, oacd, ob, wb, wout, layer, n_ctx_rows):
    n = x.shape[0]
    tm = ROW_TILE
    const = lambda shape: _const_spec(shape, 1)
    return pl.pallas_call(
        _merge_kernel,
        grid=(n // tm,),
        in_specs=[
            pl.BlockSpec((tm, D_MODEL), lambda i: (i, 0)),
            pl.BlockSpec((None, None, 1, 6 * D_MODEL), _mod_map(layer, n_ctx_rows // tm, DEC_SEQ // tm)),
            const((1, D_MODEL)),
            const((D_MODEL, 4 * D_MODEL)),
            pl.BlockSpec((tm, 3 * BRANCH_WIDTH), lambda i: (i, 0)),
            pl.BlockSpec((tm, BRANCH_WIDTH), lambda i: (i, 0)),
            const((4, BRANCH_WIDTH, D_MODEL)),
            const((D_MODEL, D_MODEL)),
        ],
        out_specs=pl.BlockSpec((tm, D_MODEL), lambda i: (i, 0)),
        out_shape=jax.ShapeDtypeStruct((n, D_MODEL), F32),
        compiler_params=pltpu.CompilerParams(vmem_limit_bytes=VMEM_LIMIT),
        name="merge",
    )(x, mod4, n1g, wgates, oacd, ob, wb, wout)


def _ffn_kernel(x_ref, xp_ref, xn_ref, mod_ref, n2g_ref, wu_ref, wg_ref, cw_ref, cb_ref, wd_ref, fg_ref,
                *rest, ctx_tiles, final):
    out_refs, (hext_ref, act_ref) = rest[:-2], rest[-2:]
    tm = x_ref.shape[0]
    i = pl.program_id(0)
    x = x_ref[...]
    mod = mod_ref[...]
    sh2, sc2, g2 = mod[:, 3 * D_MODEL:4 * D_MODEL], mod[:, 4 * D_MODEL:5 * D_MODEL], mod[:, 5 * D_MODEL:]
    n2g = n2g_ref[...]
    hext_ref[0:HALO] = _modulated_norm(xp_ref[...], n2g, sh2, sc2).astype(BF16)
    hext_ref[HALO:HALO + tm] = _modulated_norm(x, n2g, sh2, sc2).astype(BF16)
    hext_ref[HALO + tm:] = _modulated_norm(xn_ref[...], n2g, sh2, sc2).astype(BF16)
    seq_len = jnp.where(i < ctx_tiles, SEQ, DEC_SEQ)
    row = lax.broadcasted_iota(jnp.int32, (tm, 1), 0)
    pos = (row + i * tm) & (seq_len - 1)
    first = pos == 0
    last = pos == seq_len - 1
    for c0 in range(0, D_FF, FF_CHUNK):
        u_ext = _dot(hext_ref[...], wu_ref[:, c0:c0 + FF_CHUNK])
        u = u_ext[HALO:HALO + tm]
        u_dn = jnp.where(first, 0.0, u_ext[HALO - 1:HALO - 1 + tm])
        u_up = jnp.where(last, 0.0, u_ext[HALO + 1:HALO + 1 + tm])
        cw = cw_ref[:, c0:c0 + FF_CHUNK]
        uc = u_dn * cw[0:1] + u * cw[1:2] + u_up * cw[2:3] + cb_ref[:, c0:c0 + FF_CHUNK]
        g = _dot(hext_ref[HALO:HALO + tm], wg_ref[:, c0:c0 + FF_CHUNK])
        act_ref[:, c0:c0 + FF_CHUNK] = (jax.nn.gelu(uc) * g).astype(BF16)
    out = x + g2 * _dot(act_ref[...], wd_ref[...])
    if final:
        yp_ref, ys_ref = out_refs
        ms = jnp.mean(out * out, axis=-1, keepdims=True)
        y = out * lax.rsqrt(ms + EPS) * fg_ref[...]

        @pl.when(i < ctx_tiles)
        def _():
            for j in range(tm // SEQ):
                yp_ref[j] = y[j * SEQ:(j + 1) * SEQ]

        @pl.when(i >= ctx_tiles)
        def _():
            ys_ref[...] = y
    else:
        out_refs[0][...] = out


def _ffn_call(x, mod4, n2g, wu, wg, cw, cb, wd, fg, layer, n_ctx_rows, final):
    n = x.shape[0]
    tm = ROW_TILE
    ctx_tiles = n_ctx_rows // tm
    seq_tiles = DEC_SEQ // tm
    hb = tm // HALO
    last_halo = n // HALO - 1
    const = lambda shape: _const_spec(shape, 1)
    if final:
        spt = tm // SEQ
        out_specs = [
            pl.BlockSpec((spt, SEQ, D_MODEL), lambda i: (jnp.minimum(i, ctx_tiles - 1), 0, 0)),
            pl.BlockSpec((None, tm, D_MODEL), lambda i: (jnp.maximum(i - ctx_tiles, 0) // seq_tiles,
                                                         jnp.maximum(i - ctx_tiles, 0) % seq_tiles, 0)),
        ]
        out_shape = [jax.ShapeDtypeStruct((n_ctx_rows // SEQ, SEQ, D_MODEL), F32),
                     jax.ShapeDtypeStruct(((n - n_ctx_rows) // DEC_SEQ, DEC_SEQ, D_MODEL), F32)]
    else:
        out_specs = [pl.BlockSpec((tm, D_MODEL), lambda i: (i, 0))]
        out_shape = [jax.ShapeDtypeStruct((n, D_MODEL), F32)]
    return pl.pallas_call(
        functools.partial(_ffn_kernel, ctx_tiles=ctx_tiles, final=final),
        grid=(n // tm,),
        in_specs=[
            pl.BlockSpec((tm, D_MODEL), lambda i: (i, 0)),
            pl.BlockSpec((HALO, D_MODEL), lambda i: (jnp.maximum(i * hb - 1, 0), 0)),
            pl.BlockSpec((HALO, D_MODEL), lambda i: (jnp.minimum((i + 1) * hb, last_halo), 0)),
            pl.BlockSpec((None, None, 1, 6 * D_MODEL), _mod_map(layer, ctx_tiles, seq_tiles)),
            const((1, D_MODEL)),
            const((D_MODEL, D_FF)), const((D_MODEL, D_FF)), const((8, D_FF)), const((1, D_FF)),
            const((D_FF, D_MODEL)), const((1, D_MODEL)),
        ],
        out_specs=out_specs,
        out_shape=out_shape,
        scratch_shapes=[pltpu.VMEM((tm + 2 * HALO, D_MODEL), BF16), pltpu.VMEM((tm, D_FF), BF16)],
        compiler_params=pltpu.CompilerParams(vmem_limit_bytes=VMEM_LIMIT),
        name="conv_ffn",
    )(x, x, x, mod4, n2g, wu, wg, cw, cb, wd, fg)


def _rope_tables():
    t = np.arange(DEC_SEQ)
    row = (t // GRID_W).astype(np.float32)
    col = (t % GRID_W).astype(np.float32)

    def tables(head_dim):
        r = head_dim // 2
        half = r // 2
        inv = (np.float32(ROPE_BASE) ** (-np.arange(half, dtype=np.float32) / np.float32(half))).astype(np.float32)
        lane = np.arange(128) % head_dim
        use_col = lane >= r
        e = lane % r
        first_half = e < half
        freq = inv[e % half]
        pos = np.where(use_col[None, :], col[:, None], row[:, None]).astype(np.float32)
        ang = (pos * freq[None, :]).astype(np.float32)
        cos = np.cos(ang).astype(np.float32)
        sin = np.sin(ang).astype(np.float32)
        sin_a = np.where(first_half[None, :], -sin, 0.0).astype(np.float32)
        sin_b = np.where(first_half[None, :], 0.0, sin).astype(np.float32)
        return cos, sin_a, sin_b

    return jnp.asarray(np.concatenate(list(tables(64)) + list(tables(32)), axis=1))


def _in_segments():
    a_q, a_k, a_v, b_q, b_k, b_v, b_r, b_gf, b_gb = 0, 256, 384, 512, 640, 768, 1024, 1280, 1296
    c_q, c_k, c_v, d_q, d_kv, d_kr = 1312, 1568, 1824, 2080, 2272, 2400
    segs = [(a_q + (hh * 2 + g) * 64, 64) for g in range(2) for hh in range(2)]
    segs += [(a_k, 128), (a_v, 128), (b_q, 128), (b_k, 128), (b_v, 256), (b_r, 256)]
    segs += [(c_q, 256), (c_k, 256), (c_v, 256), (d_kv, 128)] + [(d_kr, 32)] * 4
    segs += [(d_q, 192), (b_gf, 16), (b_gb, 16)]
    return segs


GATES_OFF = 2432


def kernel(x_prompt, x_sample, c, cache_a_k, cache_a_v, state_b_fwd, state_b_bwd, cache_c_k, cache_c_v, cache_d_ckv, cache_d_krope, c_ctx, w_mod, b_mod, norm1_g, norm2_g, w_in, a_qnorm_g, a_knorm_g, b_gate_w_fwd, b_gate_b_fwd, b_gate_w_bwd, b_gate_b_bwd, b_onorm_g, c_lq1, c_lk1, c_lq2, c_lk2, c_onorm_g, d_qnorm_g, d_w_uq, d_kvnorm_g, d_w_ukv, w_branch, w_out, w_ffu, w_ffg, conv_w, conv_b, w_ffd, final_g):
    depth = w_in.shape[0]
    nb, nd = x_prompt.shape[0], x_sample.shape[0]
    n_ctx = nb * SEQ
    assert n_ctx % DEC_SEQ == 0 and nd + 1 <= N_MOD_ROWS

    x = jnp.concatenate([x_prompt.reshape(n_ctx, D_MODEL), x_sample.reshape(nd * DEC_SEQ, D_MODEL)], axis=0)
    cvec = jnp.zeros((N_MOD_ROWS, D_MODEL), F32).at[0].set(c_ctx).at[1:1 + nd].set(c)
    mod4 = _mod_call(cvec, w_mod, b_mod).reshape(depth, N_MOD_ROWS, 1, 6 * D_MODEL)

    rope_tab = _rope_tables()
    lane = np.arange(256)
    bd = jnp.asarray(lane[:, None] // HEAD_W == lane[None, :] // HEAD_W, BF16)
    tile = jnp.asarray(np.arange(32)[:, None] == np.arange(128)[None, :] % 32, BF16)
    tt = np.arange(GLA_BLOCK)
    same_chunk = tt[:, None] // GLA_CHUNK == tt[None, :] // GLA_CHUNK
    tri = jnp.asarray(np.stack([same_chunk & (tt[:, None] >= tt[None, :]),
                                same_chunk & (tt[:, None] <= tt[None, :])]), BF16)

    segs = [w_in[:, :, s0:s0 + w] for s0, w in _in_segments()]
    segs.append(jnp.zeros((depth, D_MODEL, NZ - sum(w for _, w in _in_segments())), F32))
    w_mix = jnp.concatenate(segs, axis=-1).astype(BF16)
    w_gates = w_in[:, :, GATES_OFF:].astype(BF16)
    wgate = jnp.zeros((depth, 128, 256), F32)
    wgate = wgate.at[:, 64:80, 0:128].set(b_gate_w_fwd).at[:, 80:96, 128:256].set(b_gate_w_bwd).astype(BF16)
    bgate = jnp.concatenate([b_gate_b_fwd, b_gate_b_bwd], axis=-1)[:, None, :]
    dqg = jnp.pad(d_qnorm_g, ((0, 0), (0, 256 - D_Q_RANK)))[:, None, :]
    uq = d_w_uq.reshape(depth, D_Q_RANK, 4, 96)
    wuq = jnp.concatenate([uq[..., :64].reshape(depth, D_Q_RANK, 256), uq[..., 64:].reshape(depth, D_Q_RANK, 128)], axis=-1)
    wuq = jnp.pad(wuq, ((0, 0), (0, 256 - D_Q_RANK), (0, 0))).astype(BF16)
    ukv = d_w_ukv.reshape(depth, 128, 4, 128)
    eye4 = jnp.eye(4, dtype=F32)
    wuk = (ukv[..., :64].transpose(0, 2, 3, 1)[:, :, :, None, :] * eye4[None, :, None, :, None])
    wuk = wuk.reshape(depth, 256, 512).astype(BF16)
    wuv = (ukv[..., 64:].transpose(0, 2, 1, 3)[:, :, :, None, :] * eye4[None, :, None, :, None])
    wuv = wuv.reshape(depth, 512, 256).astype(BF16)
    wb0 = w_branch[:, 0].reshape(depth, 2, 2, 64, D_MODEL).transpose(0, 2, 1, 3, 4).reshape(depth, 256, D_MODEL)
    wb = jnp.concatenate([wb0[:, None], w_branch[:, 1:]], axis=1).astype(BF16)
    wout = w_out.astype(BF16)
    wu, wg, wd = w_ffu.astype(BF16), w_ffg.astype(BF16), w_ffd.astype(BF16)
    cw = jnp.pad(conv_w, ((0, 0), (0, 8 - conv_w.shape[1]), (0, 0)))
    aqg = jnp.tile(a_qnorm_g, (1, 4))[:, None, :]
    akg = jnp.tile(a_knorm_g, (1, 2))[:, None, :]
    bong = jnp.tile(b_onorm_g, (1, 4))[:, None, :]
    cong = jnp.tile(c_onorm_g, (1, 4))[:, None, :]
    lamv = jnp.stack([c_lq1, c_lk1, c_lq2, c_lk2], axis=1)
    caches = (cache_a_k.reshape(nd, depth, PAST_LEN, 128), cache_a_v.reshape(nd, depth, PAST_LEN, 128),
              cache_c_k.reshape(nd, depth, PAST_LEN, 256), cache_c_v.reshape(nd, depth, PAST_LEN, 256),
              cache_d_ckv, cache_d_krope)
    expand = lambda s: (s.transpose(0, 1, 2, 4, 3)[:, :, :, :, None, :]
                        * eye4[None, None, :, None, :, None]).reshape(nd, depth, 256, 128)
    s0f_all, s0b_all = expand(state_b_fwd), expand(state_b_bwd)

    new_caches = None
    sfs, sbs = [], []
    for l in range(depth):
        lam_init = 0.8 - 0.6 * math.exp(-0.3 * l)
        small = (aqg[l], akg[l], bd, wgate[l], bgate[l], dqg[l], wuq[l], wuk[l], d_kvnorm_g[l][None])
        fa, kva, qa, new_caches = _inproj_calls(x, mod4, norm1_g[l][None], w_mix[l], rope_tab, small,
                                                new_caches, l, depth, n_ctx)
        oacd = _attn_calls(kva, qa, caches, lamv[l], cong[l], wuv[l], bd, tile, l, lam_init, n_ctx)
        ob, sf, sb = _gla_calls(fa, s0f_all[:, l], s0b_all[:, l], bong[l], bd, tri, n_ctx)
        x1 = _merge_call(x, mod4, norm1_g[l][None], w_gates[l], oacd, ob, wb[l], wout[l], l, n_ctx)
        res = _ffn_call(x1, mod4, norm2_g[l][None], wu[l], wg[l], cw[l], conv_b[l][None], wd[l],
                        final_g[None], l, n_ctx, l == depth - 1)
        x = res[0]
        diag = lambda s: jnp.stack([s[:, hh * 64:(hh + 1) * 64, hh * 32:(hh + 1) * 32] for hh in range(4)],
                                   axis=1).transpose(0, 1, 3, 2)
        sfs.append(diag(sf))
        sbs.append(diag(sb))

    y_prompt, y_sample = res
    nak, nav, nck, ncv, nckv, nkr = new_caches
    return (y_prompt, y_sample, nak.reshape(nb, depth, SEQ, 2, 64), nav.reshape(nb, depth, SEQ, 2, 64),
            jnp.stack(sfs, axis=1), jnp.stack(sbs, axis=1), nck.reshape(nb, depth, SEQ, 4, 2, 32),
            ncv.reshape(nb, depth, SEQ, 4, 64), nckv, nkr)
```

```python
import functools
import math

import numpy as np
import jax
import jax.numpy as jnp
from jax import lax
from jax.experimental import pallas as pl
from jax.experimental.pallas import tpu as pltpu

F32 = jnp.float32
BF16 = jnp.bfloat16

D_MODEL = 1024
SEQ = 256
DEC_SEQ = 1024
PAST_LEN = 512
GRID_W = 64
ROPE_BASE = 10000.0
EPS = 1e-6
LOG2E = math.log2(math.e)
HEAD_W = 64
BRANCH_WIDTH = 256
GLA_KEY_W = 32
GLA_CHUNK = 64
GLA_SUB = 16
GLA_BLOCK = 256
GLA_GATE_NORM = 16.0
D_Q_RANK = 192
D_FF = 2816
FF_CHUNK = 256
N_MOD_ROWS = 16

ROW_TILE = 512
Q_TILE = 256
HALO = 16
VMEM_LIMIT = 56 * 1024 * 1024

NZ = 2560
Z_AQ, Z_AK, Z_AV, Z_BQ, Z_BK, Z_BV, Z_BR = 0, 256, 384, 512, 640, 768, 1024
Z_CQ, Z_CK, Z_CV, Z_DKV, Z_KR, Z_DQ, Z_TAIL = 1280, 1536, 1792, 2048, 2176, 2304, 2432
F_W = 1024
F_BQ, F_BK, F_BV, F_BR, F_LGF, F_LGB = 0, 128, 256, 512, 768, 896
KV_W = 1024
KV_AK, KV_AV, KV_CK, KV_CV, KV_DKV, KV_KR = 0, 128, 256, 512, 768, 896
Q_W = 1152
Q_AQ, Q_CQ, Q_DLAT, Q_DQR = 0, 256, 512, 1024
ROPE_W = 768


def _dot(a, b):
    return jnp.dot(a, b, preferred_element_type=F32)


def _dot_nt(a, b):
    return lax.dot_general(a, b, (((1,), (1,)), ((), ())), preferred_element_type=F32)


def _dot_tn(a, b):
    return lax.dot_general(a, b, (((0,), (0,)), ((), ())), preferred_element_type=F32)


def _split3(a):
    hi = a.astype(BF16)
    r1 = a - hi.astype(F32)
    mid = r1.astype(BF16)
    lo = (r1 - mid.astype(F32)).astype(BF16)
    return hi, mid, lo


def _split_dot(a, b01):
    hi, mid, lo = _split3(a)
    return _dot(hi, b01) + _dot(mid, b01) + _dot(lo, b01)


def _split_dot_tn(b01, a):
    hi, mid, lo = _split3(a)
    return _dot(b01, hi) + _dot(b01, mid) + _dot(b01, lo)


def _group_rms(v, bd, width, gain):
    ss = _split_dot(v * v, bd)
    return v * lax.rsqrt(ss * (1.0 / width) + EPS) * gain


def _rope128(v, cos, sin_a, sin_b, shift):
    return v * cos + pltpu.roll(v, 128 - shift, 1) * sin_a + pltpu.roll(v, shift, 1) * sin_b


def _modulated_norm(x, gain, shift, scale):
    ms = jnp.mean(x * x, axis=-1, keepdims=True)
    return (x * lax.rsqrt(ms + EPS) * gain) * (1.0 + scale) + shift


def _lane_mask(width, lo, hi):
    lane = lax.broadcasted_iota(jnp.int32, (1, width), 1)
    return (lane >= lo) & (lane < hi)


def _const_spec(shape, grid_rank):
    zeros = tuple(0 for _ in shape)
    if grid_rank == 1:
        return pl.BlockSpec(shape, lambda i: zeros)
    return pl.BlockSpec(shape, lambda b, i: zeros)


def _layer_spec(shape, layer, grid_rank):
    idx = (layer,) + tuple(0 for _ in shape)
    if grid_rank == 1:
        return pl.BlockSpec((None,) + tuple(shape), lambda i: idx)
    return pl.BlockSpec((None,) + tuple(shape), lambda b, i: idx)


def _mod_kernel(cv_ref, w_ref, b_ref, o_ref):
    cv = cv_ref[...]
    a = cv * jax.nn.sigmoid(cv)
    o_ref[...] = _dot(a.astype(BF16), w_ref[...].astype(BF16)) + b_ref[...]


def _mod_call(cvec, w_mod, b_mod):
    depth = w_mod.shape[0]
    nblk = 4
    bw = 6 * D_MODEL // nblk
    return pl.pallas_call(
        _mod_kernel,
        grid=(depth, nblk),
        in_specs=[
            pl.BlockSpec((N_MOD_ROWS, D_MODEL), lambda l, j: (0, 0)),
            pl.BlockSpec((None, D_MODEL, bw), lambda l, j: (l, 0, j)),
            pl.BlockSpec((None, 1, bw), lambda l, j: (l, 0, j)),
        ],
        out_specs=pl.BlockSpec((None, N_MOD_ROWS, bw), lambda l, j: (l, 0, j)),
        out_shape=jax.ShapeDtypeStruct((depth, N_MOD_ROWS, 6 * D_MODEL), F32),
        compiler_params=pltpu.CompilerParams(vmem_limit_bytes=VMEM_LIMIT),
        name="mod_vectors",
    )(cvec, w_mod, b_mod.reshape(depth, 1, 6 * D_MODEL))


def _inproj_body(x_ref, mod_ref, n1g_ref, w_ref, rope_ref, aqg_ref, akg_ref, bd_ref, wgate_ref, bgate_ref,
                 dqg_ref, wuq_ref, wuk_ref, dkvg_ref, f_ref, kv_ref, q_ref, cache_refs):
    x = x_ref[...]
    mod = mod_ref[...]
    h = _modulated_norm(x, n1g_ref[...], mod[:, 0:D_MODEL], mod[:, D_MODEL:2 * D_MODEL])
    z = _dot(h.astype(BF16), w_ref[...])
    bd = bd_ref[...]
    bd128 = bd[0:128, 0:128]
    if rope_ref is None:
        rope64 = rope32 = lambda v: v
    else:
        rope = rope_ref[...]
        cos64, sa64, sb64 = rope[:, 0:128], rope[:, 128:256], rope[:, 256:384]
        cos32, sa32, sb32 = rope[:, 384:512], rope[:, 512:640], rope[:, 640:768]
        rope64 = lambda v: _rope128(v, cos64, sa64, sb64, 16)
        rope32 = lambda v: _rope128(v, cos32, sa32, sb32, 8)

    def emit_kv(col, cache_idx, val):
        kv_ref[:, col:col + val.shape[1]] = val.astype(BF16)
        if cache_refs is not None:
            ref = cache_refs[cache_idx]
            for j in range(val.shape[0] // SEQ):
                ref[j] = val[j * SEQ:(j + 1) * SEQ, 0:ref.shape[-1]]

    aq = _group_rms(z[:, Z_AQ:Z_AQ + 256], bd, HEAD_W, aqg_ref[...])
    a_scale = HEAD_W ** -0.5 * LOG2E
    q_ref[:, Q_AQ:Q_AQ + 128] = (rope64(aq[:, 0:128]) * a_scale).astype(BF16)
    q_ref[:, Q_AQ + 128:Q_AQ + 256] = (rope64(aq[:, 128:256]) * a_scale).astype(BF16)
    emit_kv(KV_AK, 0, rope64(_group_rms(z[:, Z_AK:Z_AK + 128], bd128, HEAD_W, akg_ref[...])))
    emit_kv(KV_AV, 1, z[:, Z_AV:Z_AV + 128])

    f_ref[:, F_BQ:F_BK] = z[:, Z_BQ:Z_BK] * (GLA_KEY_W ** -0.5)
    f_ref[:, F_BK:F_BV] = z[:, Z_BK:Z_BV]
    f_ref[:, F_BV:F_BR] = z[:, Z_BV:Z_BR]
    f_ref[:, F_BR:F_LGF] = z[:, Z_BR:Z_CQ]
    pre = _dot(z[:, Z_TAIL:NZ].astype(BF16), wgate_ref[...]) + bgate_ref[...]
    logsig = jnp.minimum(pre, 0.0) - jnp.log1p(jnp.exp(-jnp.abs(pre)))
    f_ref[:, F_LGF:F_W] = logsig / GLA_GATE_NORM

    c_scale = 32 ** -0.5 * LOG2E
    q_ref[:, Q_CQ:Q_CQ + 128] = (rope32(z[:, Z_CQ:Z_CQ + 128]) * c_scale).astype(BF16)
    q_ref[:, Q_CQ + 128:Q_CQ + 256] = (rope32(z[:, Z_CQ + 128:Z_CQ + 256]) * c_scale).astype(BF16)
    emit_kv(KV_CK, 2, jnp.concatenate([rope32(z[:, Z_CK:Z_CK + 128]), rope32(z[:, Z_CK + 128:Z_CK + 256])],
                                      axis=-1))
    emit_kv(KV_CV, 3, z[:, Z_CV:Z_DKV])

    dkv = z[:, Z_DKV:Z_DKV + 128]
    ms = jnp.mean(dkv * dkv, axis=-1, keepdims=True)
    emit_kv(KV_DKV, 4, dkv * lax.rsqrt(ms + EPS) * dkvg_ref[...])
    emit_kv(KV_KR, 5, rope32(z[:, Z_KR:Z_KR + 128]))
    zq = z[:, Z_DQ:NZ]
    zq = jnp.where(_lane_mask(NZ - Z_DQ, 0, D_Q_RANK), zq, 0.0)
    msq = jnp.sum(zq * zq, axis=-1, keepdims=True) * (1.0 / D_Q_RANK)
    qn = zq * lax.rsqrt(msq + EPS) * dqg_ref[...]
    dq = _dot(qn.astype(BF16), wuq_ref[...])
    d_scale = 96 ** -0.5 * LOG2E
    q_ref[:, Q_DLAT:Q_DQR] = (_dot(dq[:, 0:256].astype(BF16), wuk_ref[...]) * d_scale).astype(BF16)
    q_ref[:, Q_DQR:Q_W] = (rope32(dq[:, 256:384]) * d_scale).astype(BF16)


def _inproj_ctx_kernel(x_ref, mod_ref, n1g_ref, w_ref, aqg_ref, akg_ref, bd_ref, wgate_ref, bgate_ref,
                       dqg_ref, wuq_ref, wuk_ref, dkvg_ref, *rest):
    outs = rest[-9:]
    _inproj_body(x_ref, mod_ref, n1g_ref, w_ref, None, aqg_ref, akg_ref, bd_ref, wgate_ref, bgate_ref,
                 dqg_ref, wuq_ref, wuk_ref, dkvg_ref, outs[0], outs[1], outs[2], outs[3:])


def _inproj_lat_kernel(x_ref, mod_ref, n1g_ref, w_ref, rope_ref, aqg_ref, akg_ref, bd_ref, wgate_ref,
                       bgate_ref, dqg_ref, wuq_ref, wuk_ref, dkvg_ref, pf_ref, pkv_ref, pq_ref,
                       f_ref, kv_ref, q_ref):
    del pf_ref, pkv_ref, pq_ref
    _inproj_body(x_ref, mod_ref, n1g_ref, w_ref, rope_ref, aqg_ref, akg_ref, bd_ref, wgate_ref, bgate_ref,
                 dqg_ref, wuq_ref, wuk_ref, dkvg_ref, f_ref, kv_ref, q_ref, None)


CACHE_WIDTHS = (128, 128, 256, 256, 128, 32)


def _inproj_calls(x, mod4, n1g, w_mix, rope_tab, small, prev_caches, layer, depth, n_ctx_rows):
    n = x.shape[0]
    tm = ROW_TILE
    ctx_tiles = n_ctx_rows // tm
    seq_tiles = DEC_SEQ // tm
    spt = tm // SEQ
    n_ctx_seq = n_ctx_rows // SEQ
    const = lambda shape: _layer_spec(shape, layer, 1)
    small_specs = [const((1, 256)), const((1, 128)), _const_spec((256, 256), 1), const((128, 256)),
                   const((1, 256)), const((1, 256)), const((256, 384)), const((256, 512)), const((1, 128))]
    act_shapes = [jax.ShapeDtypeStruct((n, F_W), F32), jax.ShapeDtypeStruct((n, KV_W), BF16),
                  jax.ShapeDtypeStruct((n, Q_W), BF16)]
    act_specs = lambda off: [pl.BlockSpec((tm, w), lambda i: (i + off, 0)) for w in (F_W, KV_W, Q_W)]
    cparams = pltpu.CompilerParams(vmem_limit_bytes=VMEM_LIMIT)
    any_spec = pl.BlockSpec(memory_space=pl.ANY)

    n_in = 4 + len(small)
    alias = {} if prev_caches is None else {n_in + k: 3 + k for k in range(6)}
    res = pl.pallas_call(
        _inproj_ctx_kernel,
        grid=(ctx_tiles,),
        in_specs=[
            pl.BlockSpec((tm, D_MODEL), lambda i: (i, 0)),
            pl.BlockSpec((None, None, 1, 6 * D_MODEL), lambda i: (layer, 0, 0, 0)),
            const((1, D_MODEL)),
            const((D_MODEL, NZ)),
        ] + small_specs + ([] if prev_caches is None else [any_spec] * 6),
        out_specs=act_specs(0) + [pl.BlockSpec((spt, None, SEQ, w), lambda i: (i, layer, 0, 0))
                                  for w in CACHE_WIDTHS],
        out_shape=act_shapes + [jax.ShapeDtypeStruct((n_ctx_seq, depth, SEQ, w), F32) for w in CACHE_WIDTHS],
        input_output_aliases=alias,
        compiler_params=cparams,
        name="in_projection_context",
    )(x, mod4, n1g, w_mix, *small, *([] if prev_caches is None else prev_caches))
    f_c, kv_c, q_c, caches = res[0], res[1], res[2], tuple(res[3:])

    fa, kva, qa = pl.pallas_call(
        _inproj_lat_kernel,
        grid=((n - n_ctx_rows) // tm,),
        in_specs=[
            pl.BlockSpec((tm, D_MODEL), lambda i: (i + ctx_tiles, 0)),
            pl.BlockSpec((None, None, 1, 6 * D_MODEL), lambda i: (layer, 1 + i // seq_tiles, 0, 0)),
            const((1, D_MODEL)),
            const((D_MODEL, NZ)),
            pl.BlockSpec((tm, ROPE_W), lambda i: (i % seq_tiles, 0)),
        ] + small_specs + [any_spec] * 3,
        out_specs=act_specs(ctx_tiles),
        out_shape=act_shapes,
        input_output_aliases={n_in + 1: 0, n_in + 2: 1, n_in + 3: 2},
        compiler_params=cparams,
        name="in_projection_latent",
    )(x, mod4, n1g, w_mix, rope_tab, *small, f_c, kv_c, q_c)
    return fa, kva, qa, caches


def _exp2_parts(parts):
    m = parts[0].max(axis=-1, keepdims=True)
    for s in parts[1:]:
        m = jnp.maximum(m, s.max(axis=-1, keepdims=True))
    return [jnp.exp2(s - m) for s in parts]


def _attn_body(q_ref, kva_ref, ck_ref, cv_ref, dkv_ref, cache_refs, lam_ref, cong_ref, wuv_ref, bd_ref,
               tile_ref, o_ref, lam_init):
    q = q_ref[...]
    kva = kva_ref[...]
    ak, av = [kva[:, 0:128]], [kva[:, 128:256]]
    ck, cv, dk = [ck_ref[...]], [cv_ref[...]], [dkv_ref[...]]
    if cache_refs is not None:
        cak_ref, cav_ref, cck_ref, ccv_ref, cckv_ref, ckr_ref = cache_refs
        ak.insert(0, cak_ref[...].astype(BF16))
        av.insert(0, cav_ref[...].astype(BF16))
        ck.insert(0, cck_ref[...].astype(BF16))
        cv.insert(0, ccv_ref[...].astype(BF16))
        kr4 = _dot(ckr_ref[...].astype(BF16), tile_ref[...])
        dk.insert(0, jnp.concatenate([cckv_ref[...].astype(BF16), kr4.astype(BF16)], axis=-1))
    zero = jnp.zeros((), BF16)
    one = jnp.ones((), BF16)

    tq = q.shape[0]

    masks_a = [_lane_mask(128, hh * HEAD_W, (hh + 1) * HEAD_W) for hh in range(2)]
    qa = jnp.concatenate([jnp.where(masks_a[hh], q[:, Q_AQ + g * 128:Q_AQ + (g + 1) * 128], zero)
                          for hh in range(2) for g in range(2)], axis=0)
    qcs = [jnp.concatenate([jnp.where(_lane_mask(128, u * 32, (u + 1) * 32),
                                      q[:, Q_CQ + c * 128:Q_CQ + (c + 1) * 128], zero) for u in range(4)], axis=0)
           for c in range(2)]
    qd = jnp.concatenate(
        [jnp.concatenate([q[:, Q_DLAT + hh * 128:Q_DLAT + (hh + 1) * 128],
                          jnp.where(_lane_mask(128, hh * 32, (hh + 1) * 32), q[:, Q_DQR:Q_W], zero)], axis=-1)
         for hh in range(4)], axis=0)
    s_a = [_dot_nt(qa, k) for k in ak]
    s_c = [[_dot_nt(qcs[c], k[:, c * 128:(c + 1) * 128]) for k in ck] for c in range(2)]
    s_d = [_dot_nt(qd, k) for k in dk]

    ps = [p.astype(BF16) for p in _exp2_parts(s_a)]
    oa = [None, None]
    for hh in range(2):
        rows = slice(hh * 2 * tq, (hh + 1) * 2 * tq)
        o = sum(_dot(p[rows], jnp.where(masks_a[hh], v, one)) for p, v in zip(ps, av))
        o = jnp.where(masks_a[hh], o / pltpu.roll(o, HEAD_W, 1), 0.0)
        for g in range(2):
            part = o[g * tq:(g + 1) * tq]
            oa[g] = part if oa[g] is None else oa[g] + part
    o_ref[:, 0:128] = oa[0]
    o_ref[:, 128:256] = oa[1]

    lv = lam_ref[...]
    lam = (jnp.exp(jnp.sum(lv[0:1] * lv[1:2], axis=-1, keepdims=True))
           - jnp.exp(jnp.sum(lv[2:3] * lv[3:4], axis=-1, keepdims=True)) + lam_init)
    oc = []
    for c in range(2):
        c0 = c * 128
        ps = [p.astype(BF16) for p in _exp2_parts(s_c[c])]
        och = None
        for h2 in range(2):
            hh = c * 2 + h2
            rows = slice(h2 * 2 * tq, (h2 + 1) * 2 * tq)
            vmask = _lane_mask(256, hh * HEAD_W, (hh + 1) * HEAD_W)
            o = sum(_dot(p[rows], jnp.where(vmask, v, one)) for p, v in zip(ps, cv))
            res = o[:, c0:c0 + 128] / o[:, 128 - c0:256 - c0]
            o = jnp.where(_lane_mask(128, h2 * HEAD_W, (h2 + 1) * HEAD_W), res[0:tq] - lam * res[tq:2 * tq], 0.0)
            och = o if och is None else och + o
        oc.append(och)
    oc = jnp.concatenate(oc, axis=-1)
    o_ref[:, 256:512] = _group_rms(oc, bd_ref[...], HEAD_W, cong_ref[...]) * (1.0 - lam_init)

    ps = [p.astype(BF16) for p in _exp2_parts(s_d)]
    o = sum(_dot(p, jnp.where(_lane_mask(256, 0, 128), k, one)) for p, k in zip(ps, dk))
    lat = (o[:, 0:128] / o[:, 128:256]).astype(BF16)
    o_ref[:, 512:768] = _dot(jnp.concatenate([lat[hh * tq:(hh + 1) * tq] for hh in range(4)], axis=-1),
                             wuv_ref[...])


def _attn_ctx_kernel(q_ref, kva_ref, ck_ref, cv_ref, dkv_ref, lam_ref, cong_ref, wuv_ref, bd_ref,
                     tile_ref, o_ref, *, lam_init):
    _attn_body(q_ref, kva_ref, ck_ref, cv_ref, dkv_ref, None, lam_ref, cong_ref, wuv_ref, bd_ref,
               tile_ref, o_ref, lam_init)


def _attn_lat_kernel(q_ref, kva_ref, ck_ref, cv_ref, dkv_ref, cak_ref, cav_ref, cck_ref, ccv_ref,
                     cckv_ref, ckr_ref, lam_ref, cong_ref, wuv_ref, bd_ref, tile_ref, prev_ref, o_ref,
                     *, lam_init):
    del prev_ref
    _attn_body(q_ref, kva_ref, ck_ref, cv_ref, dkv_ref,
               (cak_ref, cav_ref, cck_ref, ccv_ref, cckv_ref, ckr_ref), lam_ref, cong_ref, wuv_ref,
               bd_ref, tile_ref, o_ref, lam_init)


def _attn_calls(kva, qa, caches, lamv, cong, wuv, bd, tile, layer, lam_init, n_ctx_rows):
    n = kva.shape[0]
    n_ctx_seq = n_ctx_rows // SEQ
    n_lat_seq = (n - n_ctx_rows) // DEC_SEQ
    param_specs = lambda rank: [_layer_spec((4, 32), layer, rank), _layer_spec((1, 256), layer, rank),
                                _layer_spec((512, 256), layer, rank), _const_spec((256, 256), rank),
                                _const_spec((32, 128), rank)]
    out_shape = jax.ShapeDtypeStruct((n, 3 * BRANCH_WIDTH), F32)
    cparams = pltpu.CompilerParams(vmem_limit_bytes=VMEM_LIMIT)

    o_ctx = pl.pallas_call(
        functools.partial(_attn_ctx_kernel, lam_init=lam_init),
        grid=(n_ctx_seq,),
        in_specs=[
            pl.BlockSpec((SEQ, Q_W), lambda b: (b, 0)),
            pl.BlockSpec((SEQ, 256), lambda b: (b, KV_AK // 256)),
            pl.BlockSpec((SEQ, 256), lambda b: (b, KV_CK // 256)),
            pl.BlockSpec((SEQ, 256), lambda b: (b, KV_CV // 256)),
            pl.BlockSpec((SEQ, 256), lambda b: (b, KV_DKV // 256)),
        ] + param_specs(1),
        out_specs=pl.BlockSpec((SEQ, 3 * BRANCH_WIDTH), lambda b: (b, 0)),
        out_shape=out_shape,
        compiler_params=cparams,
        name="attention_context",
    )(qa, kva, kva, kva, kva, lamv, cong, wuv, bd, tile)

    qt = Q_TILE
    q_per_seq = DEC_SEQ // qt
    q0 = n_ctx_rows // qt
    s0 = n_ctx_rows // DEC_SEQ
    cak, cav, cck, ccv, cckv, ckr = caches
    cache_spec = lambda w: pl.BlockSpec((None, None, PAST_LEN, w), lambda b, i: (b, layer, 0, 0))
    o_all = pl.pallas_call(
        functools.partial(_attn_lat_kernel, lam_init=lam_init),
        grid=(n_lat_seq, q_per_seq),
        in_specs=[
            pl.BlockSpec((qt, Q_W), lambda b, i: (q0 + b * q_per_seq + i, 0)),
            pl.BlockSpec((DEC_SEQ, 256), lambda b, i: (s0 + b, KV_AK // 256)),
            pl.BlockSpec((DEC_SEQ, 256), lambda b, i: (s0 + b, KV_CK // 256)),
            pl.BlockSpec((DEC_SEQ, 256), lambda b, i: (s0 + b, KV_CV // 256)),
            pl.BlockSpec((DEC_SEQ, 256), lambda b, i: (s0 + b, KV_DKV // 256)),
            cache_spec(128), cache_spec(128), cache_spec(256), cache_spec(256), cache_spec(128),
            cache_spec(32),
        ] + param_specs(2) + [pl.BlockSpec(memory_space=pl.ANY)],
        out_specs=pl.BlockSpec((qt, 3 * BRANCH_WIDTH), lambda b, i: (q0 + b * q_per_seq + i, 0)),
        out_shape=out_shape,
        input_output_aliases={16: 0},
        compiler_params=cparams,
        name="attention_latent",
    )(qa, kva, kva, kva, kva, cak, cav, cck, ccv, cckv, ckr, lamv, cong, wuv, bd, tile, o_ctx)
    return o_all


def _gla_chunk(q, k, v, bc, st, reverse, masks):
    c, sub, nsub = GLA_CHUNK, GLA_SUB, GLA_CHUNK // GLA_SUB
    keep, qmask, src_row, blk = masks
    edge = bc[0:1] if reverse else bc[c - 1:c]
    qs, kd = [], []
    for i in range(nsub):
        r0 = i * sub
        ref = bc[r0 + sub - 1:r0 + sub] if reverse else bc[r0:r0 + 1]
        qd = q[r0:r0 + sub] * jnp.exp(bc[r0:r0 + sub] - ref)
        qs.append(jnp.where(qmask, jnp.concatenate([qd] * 4, axis=0), 0.0))
        in_range = (src_row >= r0) if reverse else (src_row < r0 + sub)
        kd.append(k * jnp.exp(jnp.where(in_range, ref - bc, 0.0)))
    att = _dot_nt(jnp.concatenate(qs, axis=0).astype(BF16), jnp.concatenate(kd, axis=0).astype(BF16))
    att = jnp.where(keep, att, 0.0).astype(BF16)
    vb = v.astype(BF16)
    pv = _dot(att, jnp.concatenate([vb] * nsub, axis=0))
    out_blocks = []
    for i in range(nsub):
        o = None
        for hh in range(4):
            r0 = i * c + hh * sub
            part = jnp.where(_lane_mask(256, hh * HEAD_W, (hh + 1) * HEAD_W), pv[r0:r0 + sub], 0.0)
            o = part if o is None else o + part
        out_blocks.append(o)
    o_intra = jnp.concatenate(out_blocks, axis=0)
    q_dec = q * jnp.exp(bc)
    k_dec = k * jnp.exp(edge - bc)
    o_inter = _dot_nt(q_dec.astype(BF16), st.astype(BF16))
    upd = _dot_tn(vb, k_dec.astype(BF16))
    st_new = st * jnp.exp(edge) + jnp.where(blk, upd, 0.0)
    return o_intra + o_inter, st_new


def _gla_masks(reverse):
    c, sub = GLA_CHUNK, GLA_SUB
    row = lax.broadcasted_iota(jnp.int32, (4 * c, 4 * c), 0)
    col = lax.broadcasted_iota(jnp.int32, (4 * c, 4 * c), 1)
    t = (row // c) * sub + row % sub
    s = col % c
    keep = (row // c == col // c) & ((s >= t) if reverse else (s <= t))
    qmask = (lax.broadcasted_iota(jnp.int32, (4 * sub, 128), 0) // sub
             == lax.broadcasted_iota(jnp.int32, (4 * sub, 128), 1) // GLA_KEY_W)
    src_row = lax.broadcasted_iota(jnp.int32, (c, 1), 0)
    blk = (lax.broadcasted_iota(jnp.int32, (256, 128), 0) // HEAD_W
           == lax.broadcasted_iota(jnp.int32, (256, 128), 1) // GLA_KEY_W)
    return keep, qmask, src_row, blk


def _gla_body(qk_ref, v_ref, r_ref, lg_ref, s0f, s0b, g_ref, bd_ref, tri_ref, o_ref, accf_ref, accb_ref,
              bcf_ref, bcb_ref, seq_len, unroll):
    nc = seq_len // GLA_CHUNK
    for r0 in range(0, seq_len, GLA_BLOCK):
        lg = lg_ref[r0:r0 + GLA_BLOCK, :]
        for d, bc_ref in ((0, bcf_ref), (1, bcb_ref)):
            res = _dot(tri_ref[d], jnp.concatenate(_split3(lg[:, d * 128:(d + 1) * 128]), axis=-1))
            bc_ref[r0:r0 + GLA_BLOCK, :] = res[:, 0:128] + res[:, 128:256] + res[:, 256:384]
    masks_f = _gla_masks(False)
    masks_b = _gla_masks(True)

    def step(ci, carry):
        sf, sb = carry
        rows = pl.ds(pl.multiple_of(ci * GLA_CHUNK, GLA_CHUNK), GLA_CHUNK)
        o, sf = _gla_chunk(qk_ref[rows, 0:128], qk_ref[rows, 128:256], v_ref[rows, :], bcf_ref[rows, :],
                           sf, False, masks_f)
        accf_ref[rows, :] = o
        rows = pl.ds(pl.multiple_of((nc - 1 - ci) * GLA_CHUNK, GLA_CHUNK), GLA_CHUNK)
        o, sb = _gla_chunk(qk_ref[rows, 0:128], qk_ref[rows, 128:256], v_ref[rows, :], bcb_ref[rows, :],
                           sb, True, masks_b)
        accb_ref[rows, :] = o
        return sf, sb

    sf, sb = lax.fori_loop(0, nc, step, (s0f, s0b), unroll=unroll)
    r = r_ref[...]
    o_ref[...] = (_group_rms(accf_ref[...] + accb_ref[...], bd_ref[...], HEAD_W, g_ref[...])
                  * (r * jax.nn.sigmoid(r)))
    return sf, sb


def _gla_ctx_kernel(qk_ref, v_ref, r_ref, lg_ref, g_ref, bd_ref, tri_ref, o_ref, sf_ref, sb_ref, *scratch):
    zero = jnp.zeros((256, 128), F32)
    sf, sb = _gla_body(qk_ref, v_ref, r_ref, lg_ref, zero, zero, g_ref, bd_ref, tri_ref, o_ref, *scratch,
                       SEQ, True)
    sf_ref[...] = sf
    sb_ref[...] = sb


def _gla_lat_kernel(qk_ref, v_ref, r_ref, lg_ref, s0f_ref, s0b_ref, g_ref, bd_ref, tri_ref, prev_ref,
                    o_ref, *scratch):
    del prev_ref
    _gla_body(qk_ref, v_ref, r_ref, lg_ref, s0f_ref[...], s0b_ref[...], g_ref, bd_ref, tri_ref, o_ref,
              *scratch, DEC_SEQ, 2)


def _gla_calls(fa, s0f_t, s0b_t, bong, bd, tri, layer, n_ctx_rows):
    n = fa.shape[0]
    n_ctx_seq = n_ctx_rows // SEQ
    n_lat_seq = (n - n_ctx_rows) // DEC_SEQ
    cparams = pltpu.CompilerParams(vmem_limit_bytes=VMEM_LIMIT)
    const = lambda shape: _const_spec(shape, 1)
    gain_spec = _layer_spec((1, 256), layer, 1)
    state_spec = pl.BlockSpec((None, None, 256, 128), lambda b: (b, layer, 0, 0))
    out_shape = jax.ShapeDtypeStruct((n, BRANCH_WIDTH), F32)
    st_shape = jax.ShapeDtypeStruct((n_ctx_seq, 256, 128), F32)
    o_ctx, sf, sb = pl.pallas_call(
        _gla_ctx_kernel,
        grid=(n_ctx_seq,),
        in_specs=[
            pl.BlockSpec((SEQ, 256), lambda b: (b, F_BQ // 256)),
            pl.BlockSpec((SEQ, 256), lambda b: (b, F_BV // 256)),
            pl.BlockSpec((SEQ, 256), lambda b: (b, F_BR // 256)),
            pl.BlockSpec((SEQ, 256), lambda b: (b, F_LGF // 256)),
            gain_spec, const((256, 256)), const((2, GLA_BLOCK, GLA_BLOCK)),
        ],
        out_specs=[pl.BlockSpec((SEQ, BRANCH_WIDTH), lambda b: (b, 0)),
                   pl.BlockSpec((None, 256, 128), lambda b: (b, 0, 0)),
                   pl.BlockSpec((None, 256, 128), lambda b: (b, 0, 0))],
        out_shape=[out_shape, st_shape, st_shape],
        scratch_shapes=[pltpu.VMEM((SEQ, BRANCH_WIDTH), F32), pltpu.VMEM((SEQ, BRANCH_WIDTH), F32),
                        pltpu.VMEM((SEQ, 128), F32), pltpu.VMEM((SEQ, 128), F32)],
        compiler_params=cparams,
        name="gla_context",
    )(fa, fa, fa, fa, bong, bd, tri)

    s0 = n_ctx_rows // DEC_SEQ
    o_all = pl.pallas_call(
        _gla_lat_kernel,
        grid=(n_lat_seq,),
        in_specs=[
            pl.BlockSpec((DEC_SEQ, 256), lambda b: (s0 + b, F_BQ // 256)),
            pl.BlockSpec((DEC_SEQ, 256), lambda b: (s0 + b, F_BV // 256)),
            pl.BlockSpec((DEC_SEQ, 256), lambda b: (s0 + b, F_BR // 256)),
            pl.BlockSpec((DEC_SEQ, 256), lambda b: (s0 + b, F_LGF // 256)),
            state_spec, state_spec,
            gain_spec, const((256, 256)), const((2, GLA_BLOCK, GLA_BLOCK)),
            pl.BlockSpec(memory_space=pl.ANY),
        ],
        out_specs=pl.BlockSpec((DEC_SEQ, BRANCH_WIDTH), lambda b: (s0 + b, 0)),
        out_shape=out_shape,
        scratch_shapes=[pltpu.VMEM((DEC_SEQ, BRANCH_WIDTH), F32), pltpu.VMEM((DEC_SEQ, BRANCH_WIDTH), F32),
                        pltpu.VMEM((DEC_SEQ, 128), F32), pltpu.VMEM((DEC_SEQ, 128), F32)],
        input_output_aliases={9: 0},
        compiler_params=cparams,
        name="gla_latent",
    )(fa, fa, fa, fa, s0f_t, s0b_t, bong, bd, tri, o_ctx)
    return o_all, sf, sb


def _merge_kernel(x_ref, mod_ref, n1g_ref, wg_ref, oacd_ref, ob_ref, wb_ref, wout_ref, o_ref):
    x = x_ref[...]
    mod = mod_ref[...]
    h = _modulated_norm(x, n1g_ref[...], mod[:, 0:D_MODEL], mod[:, D_MODEL:2 * D_MODEL]).astype(BF16)
    oacd = oacd_ref[...]
    branches = [oacd[:, 0:256], ob_ref[...], oacd[:, 256:512], oacd[:, 512:768]]
    acc = None
    for j in range(4):
        gate = jax.nn.sigmoid(_dot(h, wg_ref[:, j * D_MODEL:(j + 1) * D_MODEL]))
        term = gate * _dot(branches[j].astype(BF16), wb_ref[j])
        acc = term if acc is None else acc + term
    mix = _dot(acc.astype(BF16), wout_ref[...])
    o_ref[...] = x + mod[:, 2 * D_MODEL:3 * D_MODEL] * mix


def _mod_map(layer, ctx_tiles, seq_tiles):
    return lambda i: (layer, jnp.where(i < ctx_tiles, 0, 1 + (i - ctx_tiles) // seq_tiles), 0, 0)


def _merge_call(x, mod4, n1g, wgates, oacd, ob, wb, wout, layer, n_ctx_rows):
    n = x.shape[0]
    tm = ROW_TILE
    const = lambda shape: _layer_spec(shape, layer, 1)
    return pl.pallas_call(
        _merge_kernel,
        grid=(n // tm,),
        in_specs=[
            pl.BlockSpec((tm, D_MODEL), lambda i: (i, 0)),
            pl.BlockSpec((None, None, 1, 6 * D_MODEL), _mod_map(layer, n_ctx_rows // tm, DEC_SEQ // tm)),
            const((1, D_MODEL)),
            const((D_MODEL, 4 * D_MODEL)),
            pl.BlockSpec((tm, 3 * BRANCH_WIDTH), lambda i: (i, 0)),
            pl.BlockSpec((tm, BRANCH_WIDTH), lambda i: (i, 0)),
            const((4, BRANCH_WIDTH, D_MODEL)),
            const((D_MODEL, D_MODEL)),
        ],
        out_specs=pl.BlockSpec((tm, D_MODEL), lambda i: (i, 0)),
        out_shape=jax.ShapeDtypeStruct((n, D_MODEL), F32),
        compiler_params=pltpu.CompilerParams(vmem_limit_bytes=VMEM_LIMIT),
        name="merge",
    )(x, mod4, n1g, wgates, oacd, ob, wb, wout)


def _ffn_kernel(x_ref, xp_ref, xn_ref, mod_ref, n2g_ref, wu_ref, wg_ref, cw_ref, cb_ref, wd_ref, fg_ref,
                *rest, ctx_tiles, final):
    out_refs, (hext_ref, act_ref) = rest[:-2], rest[-2:]
    tm = x_ref.shape[0]
    i = pl.program_id(0)
    x = x_ref[...]
    mod = mod_ref[...]
    sh2, sc2, g2 = mod[:, 3 * D_MODEL:4 * D_MODEL], mod[:, 4 * D_MODEL:5 * D_MODEL], mod[:, 5 * D_MODEL:]
    n2g = n2g_ref[...]
    hext_ref[0:HALO] = _modulated_norm(xp_ref[...], n2g, sh2, sc2).astype(BF16)
    hext_ref[HALO:HALO + tm] = _modulated_norm(x, n2g, sh2, sc2).astype(BF16)
    hext_ref[HALO + tm:] = _modulated_norm(xn_ref[...], n2g, sh2, sc2).astype(BF16)
    seq_len = jnp.where(i < ctx_tiles, SEQ, DEC_SEQ)
    row = lax.broadcasted_iota(jnp.int32, (tm, 1), 0)
    pos = (row + i * tm) & (seq_len - 1)
    first = pos == 0
    last = pos == seq_len - 1
    for c0 in range(0, D_FF, FF_CHUNK):
        u_ext = _dot(hext_ref[...], wu_ref[:, c0:c0 + FF_CHUNK])
        u = u_ext[HALO:HALO + tm]
        u_dn = jnp.where(first, 0.0, u_ext[HALO - 1:HALO - 1 + tm])
        u_up = jnp.where(last, 0.0, u_ext[HALO + 1:HALO + 1 + tm])
        cw = cw_ref[:, c0:c0 + FF_CHUNK]
        uc = u_dn * cw[0:1] + u * cw[1:2] + u_up * cw[2:3] + cb_ref[:, c0:c0 + FF_CHUNK]
        g = _dot(hext_ref[HALO:HALO + tm], wg_ref[:, c0:c0 + FF_CHUNK])
        act_ref[:, c0:c0 + FF_CHUNK] = (jax.nn.gelu(uc) * g).astype(BF16)
    out = x + g2 * _dot(act_ref[...], wd_ref[...])
    if final:
        yp_ref, ys_ref = out_refs
        ms = jnp.mean(out * out, axis=-1, keepdims=True)
        y = out * lax.rsqrt(ms + EPS) * fg_ref[...]

        @pl.when(i < ctx_tiles)
        def _():
            for j in range(tm // SEQ):
                yp_ref[j] = y[j * SEQ:(j + 1) * SEQ]

        @pl.when(i >= ctx_tiles)
        def _():
            ys_ref[...] = y
    else:
        out_refs[0][...] = out


def _ffn_call(x, mod4, n2g, wu, wg, cw, cb, wd, fg, layer, n_ctx_rows, final):
    n = x.shape[0]
    tm = ROW_TILE
    ctx_tiles = n_ctx_rows // tm
    seq_tiles = DEC_SEQ // tm
    hb = tm // HALO
    last_halo = n // HALO - 1
    lspec = lambda shape: _layer_spec(shape, layer, 1)
    if final:
        spt = tm // SEQ
        out_specs = [
            pl.BlockSpec((spt, SEQ, D_MODEL), lambda i: (jnp.minimum(i, ctx_tiles - 1), 0, 0)),
            pl.BlockSpec((None, tm, D_MODEL), lambda i: (jnp.maximum(i - ctx_tiles, 0) // seq_tiles,
                                                         jnp.maximum(i - ctx_tiles, 0) % seq_tiles, 0)),
        ]
        out_shape = [jax.ShapeDtypeStruct((n_ctx_rows // SEQ, SEQ, D_MODEL), F32),
                     jax.ShapeDtypeStruct(((n - n_ctx_rows) // DEC_SEQ, DEC_SEQ, D_MODEL), F32)]
    else:
        out_specs = [pl.BlockSpec((tm, D_MODEL), lambda i: (i, 0))]
        out_shape = [jax.ShapeDtypeStruct((n, D_MODEL), F32)]
    return pl.pallas_call(
        functools.partial(_ffn_kernel, ctx_tiles=ctx_tiles, final=final),
        grid=(n // tm,),
        in_specs=[
            pl.BlockSpec((tm, D_MODEL), lambda i: (i, 0)),
            pl.BlockSpec((HALO, D_MODEL), lambda i: (jnp.maximum(i * hb - 1, 0), 0)),
            pl.BlockSpec((HALO, D_MODEL), lambda i: (jnp.minimum((i + 1) * hb, last_halo), 0)),
            pl.BlockSpec((None, None, 1, 6 * D_MODEL), _mod_map(layer, ctx_tiles, seq_tiles)),
            lspec((1, D_MODEL)),
            lspec((D_MODEL, D_FF)), lspec((D_MODEL, D_FF)), lspec((8, D_FF)), lspec((1, D_FF)),
            lspec((D_FF, D_MODEL)), _const_spec((1, D_MODEL), 1),
        ],
        out_specs=out_specs,
        out_shape=out_shape,
        scratch_shapes=[pltpu.VMEM((tm + 2 * HALO, D_MODEL), BF16), pltpu.VMEM((tm, D_FF), BF16)],
        compiler_params=pltpu.CompilerParams(vmem_limit_bytes=VMEM_LIMIT),
        name="conv_ffn",
    )(x, x, x, mod4, n2g, wu, wg, cw, cb, wd, fg)


def _rope_tables():
    t = np.arange(DEC_SEQ)
    row = (t // GRID_W).astype(np.float32)
    col = (t % GRID_W).astype(np.float32)

    def tables(head_dim):
        r = head_dim // 2
        half = r // 2
        inv = (np.float32(ROPE_BASE) ** (-np.arange(half, dtype=np.float32) / np.float32(half))).astype(np.float32)
        lane = np.arange(128) % head_dim
        use_col = lane >= r
        e = lane % r
        first_half = e < half
        freq = inv[e % half]
        pos = np.where(use_col[None, :], col[:, None], row[:, None]).astype(np.float32)
        ang = (pos * freq[None, :]).astype(np.float32)
        cos = np.cos(ang).astype(np.float32)
        sin = np.sin(ang).astype(np.float32)
        sin_a = np.where(first_half[None, :], -sin, 0.0).astype(np.float32)
        sin_b = np.where(first_half[None, :], 0.0, sin).astype(np.float32)
        return cos, sin_a, sin_b

    return jnp.asarray(np.concatenate(list(tables(64)) + list(tables(32)), axis=1))


def _in_segments():
    a_q, a_k, a_v, b_q, b_k, b_v, b_r, b_gf, b_gb = 0, 256, 384, 512, 640, 768, 1024, 1280, 1296
    c_q, c_k, c_v, d_q, d_kv, d_kr = 1312, 1568, 1824, 2080, 2272, 2400
    segs = [(a_q + (hh * 2 + g) * 64, 64) for g in range(2) for hh in range(2)]
    segs += [(a_k, 128), (a_v, 128), (b_q, 128), (b_k, 128), (b_v, 256), (b_r, 256)]
    segs += [(c_q, 256), (c_k, 256), (c_v, 256), (d_kv, 128)] + [(d_kr, 32)] * 4
    segs += [(d_q, 192), (b_gf, 16), (b_gb, 16)]
    return segs


GATES_OFF = 2432


def kernel(x_prompt, x_sample, c, cache_a_k, cache_a_v, state_b_fwd, state_b_bwd, cache_c_k, cache_c_v, cache_d_ckv, cache_d_krope, c_ctx, w_mod, b_mod, norm1_g, norm2_g, w_in, a_qnorm_g, a_knorm_g, b_gate_w_fwd, b_gate_b_fwd, b_gate_w_bwd, b_gate_b_bwd, b_onorm_g, c_lq1, c_lk1, c_lq2, c_lk2, c_onorm_g, d_qnorm_g, d_w_uq, d_kvnorm_g, d_w_ukv, w_branch, w_out, w_ffu, w_ffg, conv_w, conv_b, w_ffd, final_g):
    depth = w_in.shape[0]
    nb, nd = x_prompt.shape[0], x_sample.shape[0]
    n_ctx = nb * SEQ
    assert n_ctx % DEC_SEQ == 0 and nd + 1 <= N_MOD_ROWS

    x = jnp.concatenate([x_prompt.reshape(n_ctx, D_MODEL), x_sample.reshape(nd * DEC_SEQ, D_MODEL)], axis=0)
    cvec = jnp.zeros((N_MOD_ROWS, D_MODEL), F32).at[0].set(c_ctx).at[1:1 + nd].set(c)
    mod4 = _mod_call(cvec, w_mod, b_mod).reshape(depth, N_MOD_ROWS, 1, 6 * D_MODEL)

    rope_tab = _rope_tables()
    lane = np.arange(256)
    bd = jnp.asarray(lane[:, None] // HEAD_W == lane[None, :] // HEAD_W, BF16)
    tile = jnp.asarray(np.arange(32)[:, None] == np.arange(128)[None, :] % 32, BF16)
    tt = np.arange(GLA_BLOCK)
    same_chunk = tt[:, None] // GLA_CHUNK == tt[None, :] // GLA_CHUNK
    tri = jnp.asarray(np.stack([same_chunk & (tt[:, None] >= tt[None, :]),
                                same_chunk & (tt[:, None] <= tt[None, :])]), BF16)

    segs = [w_in[:, :, s0:s0 + w] for s0, w in _in_segments()]
    segs.append(jnp.zeros((depth, D_MODEL, NZ - sum(w for _, w in _in_segments())), F32))
    w_mix = jnp.concatenate(segs, axis=-1).astype(BF16)
    w_gates = w_in[:, :, GATES_OFF:].astype(BF16)
    wgate = jnp.zeros((depth, 128, 256), F32)
    wgate = wgate.at[:, 64:80, 0:128].set(b_gate_w_fwd).at[:, 80:96, 128:256].set(b_gate_w_bwd).astype(BF16)
    bgate = jnp.concatenate([b_gate_b_fwd, b_gate_b_bwd], axis=-1)[:, None, :]
    dqg = jnp.pad(d_qnorm_g, ((0, 0), (0, 256 - D_Q_RANK)))[:, None, :]
    uq = d_w_uq.reshape(depth, D_Q_RANK, 4, 96)
    wuq = jnp.concatenate([uq[..., :64].reshape(depth, D_Q_RANK, 256), uq[..., 64:].reshape(depth, D_Q_RANK, 128)], axis=-1)
    wuq = jnp.pad(wuq, ((0, 0), (0, 256 - D_Q_RANK), (0, 0))).astype(BF16)
    ukv = d_w_ukv.reshape(depth, 128, 4, 128)
    eye4 = jnp.eye(4, dtype=F32)
    wuk = (ukv[..., :64].transpose(0, 2, 3, 1)[:, :, :, None, :] * eye4[None, :, None, :, None])
    wuk = wuk.reshape(depth, 256, 512).astype(BF16)
    wuv = (ukv[..., 64:].transpose(0, 2, 1, 3)[:, :, :, None, :] * eye4[None, :, None, :, None])
    wuv = wuv.reshape(depth, 512, 256).astype(BF16)
    wb0 = w_branch[:, 0].reshape(depth, 2, 2, 64, D_MODEL).transpose(0, 2, 1, 3, 4).reshape(depth, 256, D_MODEL)
    wb = jnp.concatenate([wb0[:, None], w_branch[:, 1:]], axis=1).astype(BF16)
    wout = w_out.astype(BF16)
    wu, wg, wd = w_ffu.astype(BF16), w_ffg.astype(BF16), w_ffd.astype(BF16)
    cw = jnp.pad(conv_w, ((0, 0), (0, 8 - conv_w.shape[1]), (0, 0)))
    aqg = jnp.tile(a_qnorm_g, (1, 4))[:, None, :]
    akg = jnp.tile(a_knorm_g, (1, 2))[:, None, :]
    bong = jnp.tile(b_onorm_g, (1, 4))[:, None, :]
    cong = jnp.tile(c_onorm_g, (1, 4))[:, None, :]
    lamv = jnp.stack([c_lq1, c_lk1, c_lq2, c_lk2], axis=1)
    caches = (cache_a_k.reshape(nd, depth, PAST_LEN, 128), cache_a_v.reshape(nd, depth, PAST_LEN, 128),
              cache_c_k.reshape(nd, depth, PAST_LEN, 256), cache_c_v.reshape(nd, depth, PAST_LEN, 256),
              cache_d_ckv, cache_d_krope)
    expand = lambda s: (s.transpose(0, 1, 2, 4, 3)[:, :, :, :, None, :]
                        * eye4[None, None, :, None, :, None]).reshape(nd, depth, 256, 128)
    s0f_all, s0b_all = expand(state_b_fwd), expand(state_b_bwd)

    n1g, n2g = norm1_g[:, None, :], norm2_g[:, None, :]
    small = (aqg, akg, bd, wgate, bgate, dqg, wuq, wuk, d_kvnorm_g[:, None, :])
    cb = conv_b[:, None, :]
    new_caches = None
    sfs, sbs = [], []
    for l in range(depth):
        lam_init = 0.8 - 0.6 * math.exp(-0.3 * l)
        fa, kva, qa, new_caches = _inproj_calls(x, mod4, n1g, w_mix, rope_tab, small, new_caches, l, depth, n_ctx)
        oacd = _attn_calls(kva, qa, caches, lamv, cong, wuv, bd, tile, l, lam_init, n_ctx)
        ob, sf, sb = _gla_calls(fa, s0f_all, s0b_all, bong, bd, tri, l, n_ctx)
        x1 = _merge_call(x, mod4, n1g, w_gates, oacd, ob, wb, wout, l, n_ctx)
        res = _ffn_call(x1, mod4, n2g, wu, wg, cw, cb, wd, final_g[None], l, n_ctx, l == depth - 1)
        x = res[0]
        sfs.append(sf)
        sbs.append(sb)

    diag = lambda s: jnp.stack([s[:, :, hh * 64:(hh + 1) * 64, hh * 32:(hh + 1) * 32] for hh in range(4)],
                               axis=2).transpose(0, 1, 2, 4, 3)
    y_prompt, y_sample = res
    nak, nav, nck, ncv, nckv, nkr = new_caches
    return (y_prompt, y_sample, nak.reshape(nb, depth, SEQ, 2, 64), nav.reshape(nb, depth, SEQ, 2, 64),
            diag(jnp.stack(sfs, axis=1)), diag(jnp.stack(sbs, axis=1)), nck.reshape(nb, depth, SEQ, 4, 2, 32),
            ncv.reshape(nb, depth, SEQ, 4, 64), nckv, nkr)
```

```python
import functools
import math

import numpy as np
import jax
import jax.numpy as jnp
from jax import lax
from jax.experimental import pallas as pl
from jax.experimental.pallas import tpu as pltpu

F32 = jnp.float32
BF16 = jnp.bfloat16

D_MODEL = 1024
SEQ = 256
DEC_SEQ = 1024
PAST_LEN = 512
GRID_W = 64
ROPE_BASE = 10000.0
EPS = 1e-6
LOG2E = math.log2(math.e)
HEAD_W = 64
BRANCH_WIDTH = 256
GLA_KEY_W = 32
GLA_CHUNK = 64
GLA_SUB = 16
GLA_BLOCK = 256
GLA_GATE_NORM = 16.0
D_Q_RANK = 192
D_FF = 2816
FF_CHUNK = 256
N_MOD_ROWS = 16

ROW_TILE = 512
Q_TILE = 256
HALO = 16
VMEM_LIMIT = 56 * 1024 * 1024

IN_AQ_W = 256
IN_GATE_IN = (1280, 1312)
NZ = 2176
Z_AK, Z_AV, Z_BQ, Z_BK, Z_BV, Z_BR, Z_CQ, Z_CK, Z_CV, Z_DQ = 0, 128, 256, 384, 512, 768, 1024, 1280, 1536, 1792
GATES_OFF = 2432
F_W = 1024
F_BQ, F_BK, F_BV, F_BR, F_LGF, F_LGB = 0, 128, 256, 512, 768, 896
KV_W = 1024
KV_AK, KV_AV, KV_CK, KV_CV, KV_DKV, KV_KR = 0, 128, 256, 512, 768, 896
Q_W = 1152
Q_AQ, Q_CQ, Q_DLAT, Q_DQR = 0, 256, 512, 1024
ROPE_W = 768


def _dot(a, b):
    return jnp.dot(a, b, preferred_element_type=F32)


def _dot_nt(a, b):
    return lax.dot_general(a, b, (((1,), (1,)), ((), ())), preferred_element_type=F32)


def _dot_tn(a, b):
    return lax.dot_general(a, b, (((0,), (0,)), ((), ())), preferred_element_type=F32)


def _split3(a):
    hi = a.astype(BF16)
    r1 = a - hi.astype(F32)
    mid = r1.astype(BF16)
    lo = (r1 - mid.astype(F32)).astype(BF16)
    return hi, mid, lo


def _split_dot(a, b01):
    hi, mid, lo = _split3(a)
    return _dot(hi, b01) + _dot(mid, b01) + _dot(lo, b01)


def _split_dot_tn(b01, a):
    hi, mid, lo = _split3(a)
    return _dot(b01, hi) + _dot(b01, mid) + _dot(b01, lo)


def _group_rms(v, bd, width, gain):
    ss = _split_dot(v * v, bd)
    return v * lax.rsqrt(ss * (1.0 / width) + EPS) * gain


def _rope128(v, cos, sin_a, sin_b, shift):
    return v * cos + pltpu.roll(v, 128 - shift, 1) * sin_a + pltpu.roll(v, shift, 1) * sin_b


def _modulated_norm(x, gain, shift, scale):
    ms = jnp.mean(x * x, axis=-1, keepdims=True)
    return (x * lax.rsqrt(ms + EPS) * gain) * (1.0 + scale) + shift


def _lane_mask(width, lo, hi):
    lane = lax.broadcasted_iota(jnp.int32, (1, width), 1)
    return (lane >= lo) & (lane < hi)


def _const_spec(shape, grid_rank):
    zeros = tuple(0 for _ in shape)
    if grid_rank == 1:
        return pl.BlockSpec(shape, lambda i: zeros)
    return pl.BlockSpec(shape, lambda b, i: zeros)


def _layer_spec(shape, layer, grid_rank):
    idx = (layer,) + tuple(0 for _ in shape)
    if grid_rank == 1:
        return pl.BlockSpec((None,) + tuple(shape), lambda i: idx)
    return pl.BlockSpec((None,) + tuple(shape), lambda b, i: idx)


def _mod_kernel(cv_ref, w_ref, b_ref, o_ref):
    cv = cv_ref[...]
    a = cv * jax.nn.sigmoid(cv)
    o_ref[...] = _dot(a.astype(BF16), w_ref[...].astype(BF16)) + b_ref[...]


def _mod_call(cvec, w_mod, b_mod):
    depth = w_mod.shape[0]
    nblk = 4
    bw = 6 * D_MODEL // nblk
    return pl.pallas_call(
        _mod_kernel,
        grid=(depth, nblk),
        in_specs=[
            pl.BlockSpec((N_MOD_ROWS, D_MODEL), lambda l, j: (0, 0)),
            pl.BlockSpec((None, D_MODEL, bw), lambda l, j: (l, 0, j)),
            pl.BlockSpec((None, 1, bw), lambda l, j: (l, 0, j)),
        ],
        out_specs=pl.BlockSpec((None, N_MOD_ROWS, bw), lambda l, j: (l, 0, j)),
        out_shape=jax.ShapeDtypeStruct((depth, N_MOD_ROWS, 6 * D_MODEL), F32),
        compiler_params=pltpu.CompilerParams(vmem_limit_bytes=VMEM_LIMIT),
        name="mod_vectors",
    )(cvec, w_mod, b_mod.reshape(depth, 1, 6 * D_MODEL))


def _inproj_body(x, mod_ref, n1g_ref, waq_ref, w_ref, rope_ref, aqg_ref, akg_ref, bd_ref, wgate_ref, bgate_ref,
                 dqg_ref, wuq_ref, wuk_ref, dkvg_ref, tile_ref, f_ref, kv_ref, q_ref, cache_refs):
    mod = mod_ref[...]
    h = _modulated_norm(x, n1g_ref[...], mod[:, 0:D_MODEL], mod[:, D_MODEL:2 * D_MODEL]).astype(BF16)
    z_aq = _dot(h, waq_ref[...])
    z = _dot(h, w_ref[...])
    zd = pltpu.roll(z[:, NZ - 256:NZ], 192, 1)
    bd = bd_ref[...]
    bd128 = bd[0:128, 0:128]
    if rope_ref is None:
        rope64 = rope32 = lambda v: v
    else:
        rope = rope_ref[...]
        cos64, sa64, sb64 = rope[:, 0:128], rope[:, 128:256], rope[:, 256:384]
        cos32, sa32, sb32 = rope[:, 384:512], rope[:, 512:640], rope[:, 640:768]
        rope64 = lambda v: _rope128(v, cos64, sa64, sb64, 16)
        rope32 = lambda v: _rope128(v, cos32, sa32, sb32, 8)

    def emit_kv(col, cache_idx, val, kv_val=None):
        kv_val = val if kv_val is None else kv_val
        kv_ref[:, col:col + kv_val.shape[1]] = kv_val.astype(BF16)
        if cache_refs is not None:
            ref = cache_refs[cache_idx]
            for j in range(val.shape[0] // SEQ):
                ref[j] = val[j * SEQ:(j + 1) * SEQ, 0:ref.shape[-1]]

    aq = _group_rms(z_aq, bd, HEAD_W, aqg_ref[...])
    a_scale = HEAD_W ** -0.5 * LOG2E
    q_ref[:, Q_AQ:Q_AQ + 128] = (rope64(aq[:, 0:128]) * a_scale).astype(BF16)
    q_ref[:, Q_AQ + 128:Q_AQ + 256] = (rope64(aq[:, 128:256]) * a_scale).astype(BF16)
    emit_kv(KV_AK, 0, rope64(_group_rms(z[:, Z_AK:Z_AK + 128], bd128, HEAD_W, akg_ref[...])))
    emit_kv(KV_AV, 1, z[:, Z_AV:Z_AV + 128])

    f_ref[:, F_BQ:F_BK] = z[:, Z_BQ:Z_BK] * (GLA_KEY_W ** -0.5)
    f_ref[:, F_BK:F_BV] = z[:, Z_BK:Z_BV]
    f_ref[:, F_BV:F_BR] = z[:, Z_BV:Z_BR]
    f_ref[:, F_BR:F_LGF] = z[:, Z_BR:Z_CQ]
    pre = _dot(zd[:, 128:256].astype(BF16), wgate_ref[...]) + bgate_ref[...]
    logsig = jnp.minimum(pre, 0.0) - jnp.log1p(jnp.exp(-jnp.abs(pre)))
    f_ref[:, F_LGF:F_W] = logsig / GLA_GATE_NORM

    c_scale = 32 ** -0.5 * LOG2E
    q_ref[:, Q_CQ:Q_CQ + 128] = (rope32(z[:, Z_CQ:Z_CQ + 128]) * c_scale).astype(BF16)
    q_ref[:, Q_CQ + 128:Q_CQ + 256] = (rope32(z[:, Z_CQ + 128:Z_CQ + 256]) * c_scale).astype(BF16)
    emit_kv(KV_CK, 2, jnp.concatenate([rope32(z[:, Z_CK:Z_CK + 128]), rope32(z[:, Z_CK + 128:Z_CK + 256])],
                                      axis=-1))
    emit_kv(KV_CV, 3, z[:, Z_CV:Z_DQ])

    dkv = zd[:, 0:128]
    ms = jnp.mean(dkv * dkv, axis=-1, keepdims=True)
    emit_kv(KV_DKV, 4, dkv * lax.rsqrt(ms + EPS) * dkvg_ref[...])
    kr = rope32(zd[:, 128:256])
    emit_kv(KV_KR, 5, kr, _dot(kr[:, 0:32].astype(BF16), tile_ref[...]))
    zq = jnp.where(_lane_mask(256, 0, D_Q_RANK), z[:, Z_DQ:Z_DQ + 256], 0.0)
    msq = jnp.sum(zq * zq, axis=-1, keepdims=True) * (1.0 / D_Q_RANK)
    qn = zq * lax.rsqrt(msq + EPS) * dqg_ref[...]
    dq = _dot(qn.astype(BF16), wuq_ref[...])
    d_scale = 96 ** -0.5 * LOG2E
    q_ref[:, Q_DLAT:Q_DQR] = (_dot(dq[:, 0:256].astype(BF16), wuk_ref[...]) * d_scale).astype(BF16)
    q_ref[:, Q_DQR:Q_W] = (rope32(dq[:, 256:384]) * d_scale).astype(BF16)


def _inproj_ctx_kernel(x_ref, mod_ref, n1g_ref, waq_ref, w_ref, aqg_ref, akg_ref, bd_ref, wgate_ref, bgate_ref,
                       dqg_ref, wuq_ref, wuk_ref, dkvg_ref, tile_ref, *rest):
    outs = rest[-9:]
    x = x_ref[...].reshape(-1, D_MODEL)
    _inproj_body(x, mod_ref, n1g_ref, waq_ref, w_ref, None, aqg_ref, akg_ref, bd_ref, wgate_ref, bgate_ref,
                 dqg_ref, wuq_ref, wuk_ref, dkvg_ref, tile_ref, outs[0], outs[1], outs[2], outs[3:])


def _inproj_lat_kernel(x_ref, mod_ref, n1g_ref, waq_ref, w_ref, rope_ref, aqg_ref, akg_ref, bd_ref, wgate_ref,
                       bgate_ref, dqg_ref, wuq_ref, wuk_ref, dkvg_ref, tile_ref, pf_ref, pkv_ref, pq_ref,
                       f_ref, kv_ref, q_ref):
    del pf_ref, pkv_ref, pq_ref
    _inproj_body(x_ref[...], mod_ref, n1g_ref, waq_ref, w_ref, rope_ref, aqg_ref, akg_ref, bd_ref, wgate_ref,
                 bgate_ref, dqg_ref, wuq_ref, wuk_ref, dkvg_ref, tile_ref, f_ref, kv_ref, q_ref, None)


CACHE_WIDTHS = (128, 128, 256, 256, 128, 32)


def _inproj_calls(x, n, mod4, n1g, w_aq, w_rest, rope_tab, small, prev_caches, layer, depth, n_ctx_rows):
    tm = ROW_TILE
    ctx_tiles = n_ctx_rows // tm
    seq_tiles = DEC_SEQ // tm
    spt = tm // SEQ
    n_ctx_seq = n_ctx_rows // SEQ
    if isinstance(x, tuple):
        x_ctx, x_lat = x
        x_ctx_spec = pl.BlockSpec((spt, SEQ, D_MODEL), lambda i: (i, 0, 0))
        x_lat_spec = pl.BlockSpec((None, tm, D_MODEL), lambda i: (i // seq_tiles, i % seq_tiles, 0))
    else:
        x_ctx = x_lat = x
        x_ctx_spec = pl.BlockSpec((tm, D_MODEL), lambda i: (i, 0))
        x_lat_spec = pl.BlockSpec((tm, D_MODEL), lambda i: (i + ctx_tiles, 0))
    const = lambda shape: _layer_spec(shape, layer, 1)
    small_specs = [const((1, 256)), const((1, 128)), _const_spec((256, 256), 1), const((128, 256)),
                   const((1, 256)), const((1, 256)), const((256, 384)), const((256, 512)), const((1, 128)),
                   _const_spec((32, 128), 1)]
    act_shapes = [jax.ShapeDtypeStruct((n, F_W), F32), jax.ShapeDtypeStruct((n, KV_W), BF16),
                  jax.ShapeDtypeStruct((n, Q_W), BF16)]
    act_specs = lambda off: [pl.BlockSpec((tm, w), lambda i: (i + off, 0)) for w in (F_W, KV_W, Q_W)]
    cparams = pltpu.CompilerParams(vmem_limit_bytes=VMEM_LIMIT)
    any_spec = pl.BlockSpec(memory_space=pl.ANY)

    n_in = 5 + len(small)
    alias = {} if prev_caches is None else {n_in + k: 3 + k for k in range(6)}
    res = pl.pallas_call(
        _inproj_ctx_kernel,
        grid=(ctx_tiles,),
        in_specs=[
            x_ctx_spec,
            pl.BlockSpec((None, None, 1, 6 * D_MODEL), lambda i: (layer, 0, 0, 0)),
            const((1, D_MODEL)),
            const((D_MODEL, IN_AQ_W)),
            const((D_MODEL, NZ)),
        ] + small_specs + ([] if prev_caches is None else [any_spec] * 6),
        out_specs=act_specs(0) + [pl.BlockSpec((spt, None, SEQ, w), lambda i: (i, layer, 0, 0))
                                  for w in CACHE_WIDTHS],
        out_shape=act_shapes + [jax.ShapeDtypeStruct((n_ctx_seq, depth, SEQ, w), F32) for w in CACHE_WIDTHS],
        input_output_aliases=alias,
        compiler_params=cparams,
        name="in_projection_context",
    )(x_ctx, mod4, n1g, w_aq, w_rest, *small, *([] if prev_caches is None else prev_caches))
    f_c, kv_c, q_c, caches = res[0], res[1], res[2], tuple(res[3:])

    fa, kva, qa = pl.pallas_call(
        _inproj_lat_kernel,
        grid=((n - n_ctx_rows) // tm,),
        in_specs=[
            x_lat_spec,
            pl.BlockSpec((None, None, 1, 6 * D_MODEL), lambda i: (layer, 1 + i // seq_tiles, 0, 0)),
            const((1, D_MODEL)),
            const((D_MODEL, IN_AQ_W)),
            const((D_MODEL, NZ)),
            pl.BlockSpec((tm, ROPE_W), lambda i: (i % seq_tiles, 0)),
        ] + small_specs + [any_spec] * 3,
        out_specs=act_specs(ctx_tiles),
        out_shape=act_shapes,
        input_output_aliases={n_in + 1: 0, n_in + 2: 1, n_in + 3: 2},
        compiler_params=cparams,
        name="in_projection_latent",
    )(x_lat, mod4, n1g, w_aq, w_rest, rope_tab, *small, f_c, kv_c, q_c)
    return fa, kva, qa, caches


def _exp2_parts(parts):
    m = parts[0].max(axis=-1, keepdims=True)
    for s in parts[1:]:
        m = jnp.maximum(m, s.max(axis=-1, keepdims=True))
    return [jnp.exp2(s - m) for s in parts]


def _value_variants(av, cv, dk):
    one = jnp.ones((), BF16)
    av1 = [jnp.where(_lane_mask(128, hh * HEAD_W, (hh + 1) * HEAD_W), av, one) for hh in range(2)]
    cv1 = [jnp.where(_lane_mask(256, hh * HEAD_W, (hh + 1) * HEAD_W), cv, one) for hh in range(4)]
    vlat = jnp.where(_lane_mask(256, 0, 128), dk, one)
    return av1, cv1, vlat


def _attn_core(q, ak, av1, ck, cv1, dk, vlat, lam_ref, cong_ref, wuv_ref, bd_ref, o_ref, lam_init):
    zero = jnp.zeros((), BF16)
    ak, ck, dk = [ak], [ck], [dk]
    tq = q.shape[0]

    masks_a = [_lane_mask(128, hh * HEAD_W, (hh + 1) * HEAD_W) for hh in range(2)]
    qa = jnp.concatenate([jnp.where(masks_a[hh], q[:, Q_AQ + g * 128:Q_AQ + (g + 1) * 128], zero)
                          for hh in range(2) for g in range(2)], axis=0)
    qcs = [jnp.concatenate([jnp.where(_lane_mask(128, u * 32, (u + 1) * 32),
                                      q[:, Q_CQ + c * 128:Q_CQ + (c + 1) * 128], zero) for u in range(4)], axis=0)
           for c in range(2)]
    qd = jnp.concatenate(
        [jnp.concatenate([q[:, Q_DLAT + hh * 128:Q_DLAT + (hh + 1) * 128],
                          jnp.where(_lane_mask(128, hh * 32, (hh + 1) * 32), q[:, Q_DQR:Q_W], zero)], axis=-1)
         for hh in range(4)], axis=0)
    s_a = [_dot_nt(qa, k) for k in ak]
    s_c = [[_dot_nt(qcs[c], k[:, c * 128:(c + 1) * 128]) for k in ck] for c in range(2)]
    s_d = [_dot_nt(qd, k) for k in dk]

    ps = [p.astype(BF16) for p in _exp2_parts(s_a)]
    oa = [None, None]
    for hh in range(2):
        rows = slice(hh * 2 * tq, (hh + 1) * 2 * tq)
        o = _dot(ps[0][rows], av1[hh])
        o = jnp.where(masks_a[hh], o / pltpu.roll(o, HEAD_W, 1), 0.0)
        for g in range(2):
            part = o[g * tq:(g + 1) * tq]
            oa[g] = part if oa[g] is None else oa[g] + part
    o_ref[:, 0:128] = oa[0]
    o_ref[:, 128:256] = oa[1]

    lv = lam_ref[...]
    lam = (jnp.exp(jnp.sum(lv[0:1] * lv[1:2], axis=-1, keepdims=True))
           - jnp.exp(jnp.sum(lv[2:3] * lv[3:4], axis=-1, keepdims=True)) + lam_init)
    oc = []
    for c in range(2):
        c0 = c * 128
        ps = [p.astype(BF16) for p in _exp2_parts(s_c[c])]
        och = None
        for h2 in range(2):
            hh = c * 2 + h2
            rows = slice(h2 * 2 * tq, (h2 + 1) * 2 * tq)
            o = _dot(ps[0][rows], cv1[hh])
            res = o[:, c0:c0 + 128] / o[:, 128 - c0:256 - c0]
            o = jnp.where(_lane_mask(128, h2 * HEAD_W, (h2 + 1) * HEAD_W), res[0:tq] - lam * res[tq:2 * tq], 0.0)
            och = o if och is None else och + o
        oc.append(och)
    oc = jnp.concatenate(oc, axis=-1)
    o_ref[:, 256:512] = _group_rms(oc, bd_ref[...], HEAD_W, cong_ref[...]) * (1.0 - lam_init)

    ps = [p.astype(BF16) for p in _exp2_parts(s_d)]
    o = _dot(ps[0], vlat)
    lat = (o[:, 0:128] / o[:, 128:256]).astype(BF16)
    o_ref[:, 512:768] = _dot(jnp.concatenate([lat[hh * tq:(hh + 1) * tq] for hh in range(4)], axis=-1),
                             wuv_ref[...])


def _attn_ctx_kernel(q_ref, kva_ref, ck_ref, cv_ref, dkv_ref, lam_ref, cong_ref, wuv_ref, bd_ref,
                     o_ref, *, lam_init):
    kva = kva_ref[...]
    dk = dkv_ref[...]
    av1, cv1, vlat = _value_variants(kva[:, 128:256], cv_ref[...], dk)
    _attn_core(q_ref[...], kva[:, 0:128], av1, ck_ref[...], cv1, dk, vlat, lam_ref, cong_ref, wuv_ref, bd_ref,
               o_ref, lam_init)


def _attn_lat_kernel(q_ref, kva_ref, ck_ref, cv_ref, dkv_ref, cak_ref, cav_ref, cck_ref, ccv_ref,
                     cckv_ref, ckr_ref, lam_ref, cong_ref, wuv_ref, bd_ref, tile_ref, prev_ref, o_ref,
                     ak_s, ck_s, dk_s, av1_s, cv1_s, vlat_s, *, lam_init):
    del prev_ref

    @pl.when(pl.program_id(1) == 0)
    def _():
        p = PAST_LEN
        kva = kva_ref[...]
        kr4 = _dot(ckr_ref[...].astype(BF16), tile_ref[...]).astype(BF16)
        pieces = ((cak_ref[...].astype(BF16), cav_ref[...].astype(BF16), cck_ref[...].astype(BF16),
                   ccv_ref[...].astype(BF16), jnp.concatenate([cckv_ref[...].astype(BF16), kr4], axis=-1)),
                  (kva[:, 0:128], kva[:, 128:256], ck_ref[...], cv_ref[...], dkv_ref[...]))
        for rows, (ak, av, ck, cv, dk) in zip((slice(0, p), slice(p, p + DEC_SEQ)), pieces):
            av1, cv1, vlat = _value_variants(av, cv, dk)
            ak_s[rows] = ak
            ck_s[rows] = ck
            dk_s[rows] = dk
            vlat_s[rows] = vlat
            for hh in range(2):
                av1_s[hh, rows] = av1[hh]
            for hh in range(4):
                cv1_s[hh, rows] = cv1[hh]

    _attn_core(q_ref[...], ak_s[...], [av1_s[hh] for hh in range(2)], ck_s[...],
               [cv1_s[hh] for hh in range(4)], dk_s[...], vlat_s[...], lam_ref, cong_ref, wuv_ref, bd_ref,
               o_ref, lam_init)


def _attn_calls(kva, qa, caches, lamv, cong, wuv, bd, tile, layer, lam_init, n_ctx_rows):
    n = kva.shape[0]
    n_ctx_seq = n_ctx_rows // SEQ
    n_lat_seq = (n - n_ctx_rows) // DEC_SEQ
    param_specs = lambda rank: [_layer_spec((4, 32), layer, rank), _layer_spec((1, 256), layer, rank),
                                _layer_spec((512, 256), layer, rank), _const_spec((256, 256), rank),
                                _const_spec((32, 128), rank)]
    out_shape = jax.ShapeDtypeStruct((n, 3 * BRANCH_WIDTH), F32)
    cparams = pltpu.CompilerParams(vmem_limit_bytes=VMEM_LIMIT)

    o_ctx = pl.pallas_call(
        functools.partial(_attn_ctx_kernel, lam_init=lam_init),
        grid=(n_ctx_seq,),
        in_specs=[
            pl.BlockSpec((SEQ, Q_W), lambda b: (b, 0)),
            pl.BlockSpec((SEQ, 256), lambda b: (b, KV_AK // 256)),
            pl.BlockSpec((SEQ, 256), lambda b: (b, KV_CK // 256)),
            pl.BlockSpec((SEQ, 256), lambda b: (b, KV_CV // 256)),
            pl.BlockSpec((SEQ, 256), lambda b: (b, KV_DKV // 256)),
        ] + param_specs(1)[:-1],
        out_specs=pl.BlockSpec((SEQ, 3 * BRANCH_WIDTH), lambda b: (b, 0)),
        out_shape=out_shape,
        compiler_params=cparams,
        name="attention_context",
    )(qa, kva, kva, kva, kva, lamv, cong, wuv, bd)

    qt = Q_TILE
    q_per_seq = DEC_SEQ // qt
    q0 = n_ctx_rows // qt
    s0 = n_ctx_rows // DEC_SEQ
    cak, cav, cck, ccv, cckv, ckr = caches
    cache_spec = lambda w: pl.BlockSpec((None, None, PAST_LEN, w), lambda b, i: (b, layer, 0, 0))
    n_keys = PAST_LEN + DEC_SEQ
    o_all = pl.pallas_call(
        functools.partial(_attn_lat_kernel, lam_init=lam_init),
        grid=(n_lat_seq, q_per_seq),
        in_specs=[
            pl.BlockSpec((qt, Q_W), lambda b, i: (q0 + b * q_per_seq + i, 0)),
            pl.BlockSpec((DEC_SEQ, 256), lambda b, i: (s0 + b, KV_AK // 256)),
            pl.BlockSpec((DEC_SEQ, 256), lambda b, i: (s0 + b, KV_CK // 256)),
            pl.BlockSpec((DEC_SEQ, 256), lambda b, i: (s0 + b, KV_CV // 256)),
            pl.BlockSpec((DEC_SEQ, 256), lambda b, i: (s0 + b, KV_DKV // 256)),
            cache_spec(128), cache_spec(128), cache_spec(256), cache_spec(256), cache_spec(128),
            cache_spec(32),
        ] + param_specs(2) + [pl.BlockSpec(memory_space=pl.ANY)],
        out_specs=pl.BlockSpec((qt, 3 * BRANCH_WIDTH), lambda b, i: (q0 + b * q_per_seq + i, 0)),
        out_shape=out_shape,
        scratch_shapes=[pltpu.VMEM((n_keys, 128), BF16), pltpu.VMEM((n_keys, 256), BF16),
                        pltpu.VMEM((n_keys, 256), BF16), pltpu.VMEM((2, n_keys, 128), BF16),
                        pltpu.VMEM((4, n_keys, 256), BF16), pltpu.VMEM((n_keys, 256), BF16)],
        input_output_aliases={16: 0},
        compiler_params=cparams,
        name="attention_latent",
    )(qa, kva, kva, kva, kva, cak, cav, cck, ccv, cckv, ckr, lamv, cong, wuv, bd, tile, o_ctx)
    return o_all


def _gla_chunk(q, k, v, bc, st, reverse, masks):
    c, sub, nsub = GLA_CHUNK, GLA_SUB, GLA_CHUNK // GLA_SUB
    keep, qmask, src_row, blk = masks
    edge = bc[0:1] if reverse else bc[c - 1:c]
    qs, kd = [], []
    for i in range(nsub):
        r0 = i * sub
        ref = bc[r0 + sub - 1:r0 + sub] if reverse else bc[r0:r0 + 1]
        qd = q[r0:r0 + sub] * jnp.exp(bc[r0:r0 + sub] - ref)
        qs.append(jnp.where(qmask, jnp.concatenate([qd] * 4, axis=0), 0.0))
        in_range = (src_row >= r0) if reverse else (src_row < r0 + sub)
        kd.append(k * jnp.exp(jnp.where(in_range, ref - bc, 0.0)))
    att = _dot_nt(jnp.concatenate(qs, axis=0).astype(BF16), jnp.concatenate(kd, axis=0).astype(BF16))
    att = jnp.where(keep, att, 0.0).astype(BF16)
    vb = v.astype(BF16)
    pv = _dot(att, jnp.concatenate([vb] * nsub, axis=0))
    out_blocks = []
    for i in range(nsub):
        o = None
        for hh in range(4):
            r0 = i * c + hh * sub
            part = jnp.where(_lane_mask(256, hh * HEAD_W, (hh + 1) * HEAD_W), pv[r0:r0 + sub], 0.0)
            o = part if o is None else o + part
        out_blocks.append(o)
    o_intra = jnp.concatenate(out_blocks, axis=0)
    q_dec = q * jnp.exp(bc)
    k_dec = k * jnp.exp(edge - bc)
    o_inter = _dot_nt(q_dec.astype(BF16), st.astype(BF16))
    upd = _dot_tn(vb, k_dec.astype(BF16))
    st_new = st * jnp.exp(edge) + jnp.where(blk, upd, 0.0)
    return o_intra + o_inter, st_new


def _gla_masks(reverse):
    c, sub = GLA_CHUNK, GLA_SUB
    row = lax.broadcasted_iota(jnp.int32, (4 * c, 4 * c), 0)
    col = lax.broadcasted_iota(jnp.int32, (4 * c, 4 * c), 1)
    t = (row // c) * sub + row % sub
    s = col % c
    keep = (row // c == col // c) & ((s >= t) if reverse else (s <= t))
    qmask = (lax.broadcasted_iota(jnp.int32, (4 * sub, 128), 0) // sub
             == lax.broadcasted_iota(jnp.int32, (4 * sub, 128), 1) // GLA_KEY_W)
    src_row = lax.broadcasted_iota(jnp.int32, (c, 1), 0)
    blk = (lax.broadcasted_iota(jnp.int32, (256, 128), 0) // HEAD_W
           == lax.broadcasted_iota(jnp.int32, (256, 128), 1) // GLA_KEY_W)
    return keep, qmask, src_row, blk


def _gla_body(qk_ref, v_ref, r_ref, lg_ref, s0f, s0b, g_ref, bd_ref, tri_ref, o_ref, accf_ref, accb_ref,
              bcf_ref, bcb_ref, seq_len, unroll):
    nc = seq_len // GLA_CHUNK
    for r0 in range(0, seq_len, GLA_BLOCK):
        lg = lg_ref[r0:r0 + GLA_BLOCK, :]
        for d, bc_ref in ((0, bcf_ref), (1, bcb_ref)):
            res = _dot(tri_ref[d], jnp.concatenate(_split3(lg[:, d * 128:(d + 1) * 128]), axis=-1))
            bc_ref[r0:r0 + GLA_BLOCK, :] = res[:, 0:128] + res[:, 128:256] + res[:, 256:384]
    masks_f = _gla_masks(False)
    masks_b = _gla_masks(True)

    def step(ci, carry):
        sf, sb = carry
        rows = pl.ds(pl.multiple_of(ci * GLA_CHUNK, GLA_CHUNK), GLA_CHUNK)
        o, sf = _gla_chunk(qk_ref[rows, 0:128], qk_ref[rows, 128:256], v_ref[rows, :], bcf_ref[rows, :],
                           sf, False, masks_f)
        accf_ref[rows, :] = o
        rows = pl.ds(pl.multiple_of((nc - 1 - ci) * GLA_CHUNK, GLA_CHUNK), GLA_CHUNK)
        o, sb = _gla_chunk(qk_ref[rows, 0:128], qk_ref[rows, 128:256], v_ref[rows, :], bcb_ref[rows, :],
                           sb, True, masks_b)
        accb_ref[rows, :] = o
        return sf, sb

    sf, sb = lax.fori_loop(0, nc, step, (s0f, s0b), unroll=unroll)
    r = r_ref[...]
    o_ref[...] = (_group_rms(accf_ref[...] + accb_ref[...], bd_ref[...], HEAD_W, g_ref[...])
                  * (r * jax.nn.sigmoid(r)))
    return sf, sb


def _gla_ctx_kernel(qk_ref, v_ref, r_ref, lg_ref, g_ref, bd_ref, tri_ref, o_ref, sf_ref, sb_ref, *scratch):
    zero = jnp.zeros((256, 128), F32)
    sf, sb = _gla_body(qk_ref, v_ref, r_ref, lg_ref, zero, zero, g_ref, bd_ref, tri_ref, o_ref, *scratch,
                       SEQ, True)
    sf_ref[...] = sf
    sb_ref[...] = sb


def _gla_lat_kernel(qk_ref, v_ref, r_ref, lg_ref, s0f_ref, s0b_ref, g_ref, bd_ref, tri_ref, prev_ref,
                    o_ref, *scratch):
    del prev_ref
    _gla_body(qk_ref, v_ref, r_ref, lg_ref, s0f_ref[...], s0b_ref[...], g_ref, bd_ref, tri_ref, o_ref,
              *scratch, DEC_SEQ, 2)


def _gla_calls(fa, s0f_t, s0b_t, bong, bd, tri, layer, n_ctx_rows):
    n = fa.shape[0]
    n_ctx_seq = n_ctx_rows // SEQ
    n_lat_seq = (n - n_ctx_rows) // DEC_SEQ
    cparams = pltpu.CompilerParams(vmem_limit_bytes=VMEM_LIMIT)
    const = lambda shape: _const_spec(shape, 1)
    gain_spec = _layer_spec((1, 256), layer, 1)
    state_spec = pl.BlockSpec((None, None, 256, 128), lambda b: (b, layer, 0, 0))
    out_shape = jax.ShapeDtypeStruct((n, BRANCH_WIDTH), F32)
    st_shape = jax.ShapeDtypeStruct((n_ctx_seq, 256, 128), F32)
    o_ctx, sf, sb = pl.pallas_call(
        _gla_ctx_kernel,
        grid=(n_ctx_seq,),
        in_specs=[
            pl.BlockSpec((SEQ, 256), lambda b: (b, F_BQ // 256)),
            pl.BlockSpec((SEQ, 256), lambda b: (b, F_BV // 256)),
            pl.BlockSpec((SEQ, 256), lambda b: (b, F_BR // 256)),
            pl.BlockSpec((SEQ, 256), lambda b: (b, F_LGF // 256)),
            gain_spec, const((256, 256)), const((2, GLA_BLOCK, GLA_BLOCK)),
        ],
        out_specs=[pl.BlockSpec((SEQ, BRANCH_WIDTH), lambda b: (b, 0)),
                   pl.BlockSpec((None, 256, 128), lambda b: (b, 0, 0)),
                   pl.BlockSpec((None, 256, 128), lambda b: (b, 0, 0))],
        out_shape=[out_shape, st_shape, st_shape],
        scratch_shapes=[pltpu.VMEM((SEQ, BRANCH_WIDTH), F32), pltpu.VMEM((SEQ, BRANCH_WIDTH), F32),
                        pltpu.VMEM((SEQ, 128), F32), pltpu.VMEM((SEQ, 128), F32)],
        compiler_params=cparams,
        name="gla_context",
    )(fa, fa, fa, fa, bong, bd, tri)

    s0 = n_ctx_rows // DEC_SEQ
    o_all = pl.pallas_call(
        _gla_lat_kernel,
        grid=(n_lat_seq,),
        in_specs=[
            pl.BlockSpec((DEC_SEQ, 256), lambda b: (s0 + b, F_BQ // 256)),
            pl.BlockSpec((DEC_SEQ, 256), lambda b: (s0 + b, F_BV // 256)),
            pl.BlockSpec((DEC_SEQ, 256), lambda b: (s0 + b, F_BR // 256)),
            pl.BlockSpec((DEC_SEQ, 256), lambda b: (s0 + b, F_LGF // 256)),
            state_spec, state_spec,
            gain_spec, const((256, 256)), const((2, GLA_BLOCK, GLA_BLOCK)),
            pl.BlockSpec(memory_space=pl.ANY),
        ],
        out_specs=pl.BlockSpec((DEC_SEQ, BRANCH_WIDTH), lambda b: (s0 + b, 0)),
        out_shape=out_shape,
        scratch_shapes=[pltpu.VMEM((DEC_SEQ, BRANCH_WIDTH), F32), pltpu.VMEM((DEC_SEQ, BRANCH_WIDTH), F32),
                        pltpu.VMEM((DEC_SEQ, 128), F32), pltpu.VMEM((DEC_SEQ, 128), F32)],
        input_output_aliases={9: 0},
        compiler_params=cparams,
        name="gla_latent",
    )(fa, fa, fa, fa, s0f_t, s0b_t, bong, bd, tri, o_ctx)
    return o_all, sf, sb


def _merge_kernel(*refs, ctx_tiles):
    mod_ref, n1g_ref, wg_ref, oacd_ref, ob_ref, wb_ref, wout_ref, o_ref = refs[-8:]
    if len(refs) == 9:
        x = refs[0][...]
    else:
        x = jnp.where(pl.program_id(0) < ctx_tiles, refs[0][...].reshape(-1, D_MODEL), refs[1][...])
    mod = mod_ref[...]
    h = _modulated_norm(x, n1g_ref[...], mod[:, 0:D_MODEL], mod[:, D_MODEL:2 * D_MODEL]).astype(BF16)
    oacd = oacd_ref[...]
    branches = [oacd[:, 0:256], ob_ref[...], oacd[:, 256:512], oacd[:, 512:768]]
    acc = None
    for j in range(4):
        gate = jax.nn.sigmoid(_dot(h, wg_ref[:, j * D_MODEL:(j + 1) * D_MODEL]))
        term = gate * _dot(branches[j].astype(BF16), wb_ref[j])
        acc = term if acc is None else acc + term
    mix = _dot(acc.astype(BF16), wout_ref[...])
    o_ref[...] = x + mod[:, 2 * D_MODEL:3 * D_MODEL] * mix


def _mod_map(layer, ctx_tiles, seq_tiles):
    return lambda i: (layer, jnp.where(i < ctx_tiles, 0, 1 + (i - ctx_tiles) // seq_tiles), 0, 0)


def _merge_call(x, n, mod4, n1g, wgates, oacd, ob, wb, wout, layer, n_ctx_rows):
    tm = ROW_TILE
    ctx_tiles = n_ctx_rows // tm
    seq_tiles = DEC_SEQ // tm
    const = lambda shape: _layer_spec(shape, layer, 1)
    if isinstance(x, tuple):
        xs = list(x)
        x_specs = [
            pl.BlockSpec((tm // SEQ, SEQ, D_MODEL), lambda i: (jnp.minimum(i, ctx_tiles - 1), 0, 0)),
            pl.BlockSpec((None, tm, D_MODEL), lambda i: (jnp.maximum(i - ctx_tiles, 0) // seq_tiles,
                                                         jnp.maximum(i - ctx_tiles, 0) % seq_tiles, 0)),
        ]
    else:
        xs = [x]
        x_specs = [pl.BlockSpec((tm, D_MODEL), lambda i: (i, 0))]
    return pl.pallas_call(
        functools.partial(_merge_kernel, ctx_tiles=ctx_tiles),
        grid=(n // tm,),
        in_specs=x_specs + [
            pl.BlockSpec((None, None, 1, 6 * D_MODEL), _mod_map(layer, ctx_tiles, seq_tiles)),
            const((1, D_MODEL)),
            const((D_MODEL, 4 * D_MODEL)),
            pl.BlockSpec((tm, 3 * BRANCH_WIDTH), lambda i: (i, 0)),
            pl.BlockSpec((tm, BRANCH_WIDTH), lambda i: (i, 0)),
            const((4, BRANCH_WIDTH, D_MODEL)),
            const((D_MODEL, D_MODEL)),
        ],
        out_specs=pl.BlockSpec((tm, D_MODEL), lambda i: (i, 0)),
        out_shape=jax.ShapeDtypeStruct((n, D_MODEL), F32),
        compiler_params=pltpu.CompilerParams(vmem_limit_bytes=VMEM_LIMIT),
        name="merge",
    )(*xs, mod4, n1g, wgates, oacd, ob, wb, wout)


def _ffn_kernel(x_ref, xp_ref, xn_ref, mod_ref, n2g_ref, wu_ref, wg_ref, cw_ref, cb_ref, wd_ref, fg_ref,
                *rest, ctx_tiles, final):
    out_refs, (hext_ref, act_ref) = rest[:-2], rest[-2:]
    tm = x_ref.shape[0]
    i = pl.program_id(0)
    x = x_ref[...]
    mod = mod_ref[...]
    sh2, sc2, g2 = mod[:, 3 * D_MODEL:4 * D_MODEL], mod[:, 4 * D_MODEL:5 * D_MODEL], mod[:, 5 * D_MODEL:]
    n2g = n2g_ref[...]
    hext_ref[0:HALO] = _modulated_norm(xp_ref[...], n2g, sh2, sc2).astype(BF16)
    hext_ref[HALO:HALO + tm] = _modulated_norm(x, n2g, sh2, sc2).astype(BF16)
    hext_ref[HALO + tm:] = _modulated_norm(xn_ref[...], n2g, sh2, sc2).astype(BF16)
    seq_len = jnp.where(i < ctx_tiles, SEQ, DEC_SEQ)
    row = lax.broadcasted_iota(jnp.int32, (tm, 1), 0)
    pos = (row + i * tm) & (seq_len - 1)
    first = pos == 0
    last = pos == seq_len - 1
    for c0 in range(0, D_FF, FF_CHUNK):
        u_ext = _dot(hext_ref[...], wu_ref[:, c0:c0 + FF_CHUNK])
        u = u_ext[HALO:HALO + tm]
        u_dn = jnp.where(first, 0.0, u_ext[HALO - 1:HALO - 1 + tm])
        u_up = jnp.where(last, 0.0, u_ext[HALO + 1:HALO + 1 + tm])
        cw = cw_ref[:, c0:c0 + FF_CHUNK]
        uc = u_dn * cw[0:1] + u * cw[1:2] + u_up * cw[2:3] + cb_ref[:, c0:c0 + FF_CHUNK]
        g = _dot(hext_ref[HALO:HALO + tm], wg_ref[:, c0:c0 + FF_CHUNK])
        act_ref[:, c0:c0 + FF_CHUNK] = (jax.nn.gelu(uc) * g).astype(BF16)
    out = x + g2 * _dot(act_ref[...], wd_ref[...])
    if final:
        yp_ref, ys_ref = out_refs
        ms = jnp.mean(out * out, axis=-1, keepdims=True)
        y = out * lax.rsqrt(ms + EPS) * fg_ref[...]

        @pl.when(i < ctx_tiles)
        def _():
            for j in range(tm // SEQ):
                yp_ref[j] = y[j * SEQ:(j + 1) * SEQ]

        @pl.when(i >= ctx_tiles)
        def _():
            ys_ref[...] = y
    else:
        out_refs[0][...] = out


def _ffn_call(x, mod4, n2g, wu, wg, cw, cb, wd, fg, layer, n_ctx_rows, final):
    n = x.shape[0]
    tm = ROW_TILE
    ctx_tiles = n_ctx_rows // tm
    seq_tiles = DEC_SEQ // tm
    hb = tm // HALO
    last_halo = n // HALO - 1
    lspec = lambda shape: _layer_spec(shape, layer, 1)
    if final:
        spt = tm // SEQ
        out_specs = [
            pl.BlockSpec((spt, SEQ, D_MODEL), lambda i: (jnp.minimum(i, ctx_tiles - 1), 0, 0)),
            pl.BlockSpec((None, tm, D_MODEL), lambda i: (jnp.maximum(i - ctx_tiles, 0) // seq_tiles,
                                                         jnp.maximum(i - ctx_tiles, 0) % seq_tiles, 0)),
        ]
        out_shape = [jax.ShapeDtypeStruct((n_ctx_rows // SEQ, SEQ, D_MODEL), F32),
                     jax.ShapeDtypeStruct(((n - n_ctx_rows) // DEC_SEQ, DEC_SEQ, D_MODEL), F32)]
    else:
        out_specs = [pl.BlockSpec((tm, D_MODEL), lambda i: (i, 0))]
        out_shape = [jax.ShapeDtypeStruct((n, D_MODEL), F32)]
    return pl.pallas_call(
        functools.partial(_ffn_kernel, ctx_tiles=ctx_tiles, final=final),
        grid=(n // tm,),
        in_specs=[
            pl.BlockSpec((tm, D_MODEL), lambda i: (i, 0)),
            pl.BlockSpec((HALO, D_MODEL), lambda i: (jnp.maximum(i * hb - 1, 0), 0)),
            pl.BlockSpec((HALO, D_MODEL), lambda i: (jnp.minimum((i + 1) * hb, last_halo), 0)),
            pl.BlockSpec((None, None, 1, 6 * D_MODEL), _mod_map(layer, ctx_tiles, seq_tiles)),
            lspec((1, D_MODEL)),
            lspec((D_MODEL, D_FF)), lspec((D_MODEL, D_FF)), lspec((8, D_FF)), lspec((1, D_FF)),
            lspec((D_FF, D_MODEL)), _const_spec((1, D_MODEL), 1),
        ],
        out_specs=out_specs,
        out_shape=out_shape,
        scratch_shapes=[pltpu.VMEM((tm + 2 * HALO, D_MODEL), BF16), pltpu.VMEM((tm, D_FF), BF16)],
        compiler_params=pltpu.CompilerParams(vmem_limit_bytes=VMEM_LIMIT),
        name="conv_ffn",
    )(x, x, x, mod4, n2g, wu, wg, cw, cb, wd, fg)


def _rope_tables():
    t = np.arange(DEC_SEQ)
    row = (t // GRID_W).astype(np.float32)
    col = (t % GRID_W).astype(np.float32)

    def tables(head_dim):
        r = head_dim // 2
        half = r // 2
        inv = (np.float32(ROPE_BASE) ** (-np.arange(half, dtype=np.float32) / np.float32(half))).astype(np.float32)
        lane = np.arange(128) % head_dim
        use_col = lane >= r
        e = lane % r
        first_half = e < half
        freq = inv[e % half]
        pos = np.where(use_col[None, :], col[:, None], row[:, None]).astype(np.float32)
        ang = (pos * freq[None, :]).astype(np.float32)
        cos = np.cos(ang).astype(np.float32)
        sin = np.sin(ang).astype(np.float32)
        sin_a = np.where(first_half[None, :], -sin, 0.0).astype(np.float32)
        sin_b = np.where(first_half[None, :], 0.0, sin).astype(np.float32)
        return cos, sin_a, sin_b

    return jnp.asarray(np.concatenate(list(tables(64)) + list(tables(32)), axis=1))


def kernel(x_prompt, x_sample, c, cache_a_k, cache_a_v, state_b_fwd, state_b_bwd, cache_c_k, cache_c_v, cache_d_ckv, cache_d_krope, c_ctx, w_mod, b_mod, norm1_g, norm2_g, w_in, a_qnorm_g, a_knorm_g, b_gate_w_fwd, b_gate_b_fwd, b_gate_w_bwd, b_gate_b_bwd, b_onorm_g, c_lq1, c_lk1, c_lq2, c_lk2, c_onorm_g, d_qnorm_g, d_w_uq, d_kvnorm_g, d_w_ukv, w_branch, w_out, w_ffu, w_ffg, conv_w, conv_b, w_ffd, final_g):
    depth = w_in.shape[0]
    nb, nd = x_prompt.shape[0], x_sample.shape[0]
    n_ctx = nb * SEQ
    assert n_ctx % DEC_SEQ == 0 and nd + 1 <= N_MOD_ROWS

    n = n_ctx + nd * DEC_SEQ
    cvec = jnp.zeros((N_MOD_ROWS, D_MODEL), F32).at[0].set(c_ctx).at[1:1 + nd].set(c)
    mod4 = _mod_call(cvec, w_mod, b_mod).reshape(depth, N_MOD_ROWS, 1, 6 * D_MODEL)

    rope_tab = _rope_tables()
    lane = np.arange(256)
    bd = jnp.asarray(lane[:, None] // HEAD_W == lane[None, :] // HEAD_W, BF16)
    tile = jnp.asarray(np.arange(32)[:, None] == np.arange(128)[None, :] % 32, BF16)
    tt = np.arange(GLA_BLOCK)
    same_chunk = tt[:, None] // GLA_CHUNK == tt[None, :] // GLA_CHUNK
    tri = jnp.asarray(np.stack([same_chunk & (tt[:, None] >= tt[None, :]),
                                same_chunk & (tt[:, None] <= tt[None, :])]), BF16)

    w_aq = (w_in[:, :, 0:IN_AQ_W].reshape(depth, D_MODEL, 2, 2, HEAD_W).transpose(0, 1, 3, 2, 4)
            .reshape(depth, D_MODEL, IN_AQ_W).astype(BF16))
    g0, g1 = IN_GATE_IN
    w_rest = jnp.concatenate([w_in[:, :, IN_AQ_W:g0], w_in[:, :, g1:GATES_OFF], w_in[:, :, g0:g1]],
                             axis=-1).astype(BF16)
    w_gates = w_in[:, :, GATES_OFF:].astype(BF16)
    wgate = jnp.zeros((depth, 128, 256), F32)
    wgate = wgate.at[:, 32:48, 0:128].set(b_gate_w_fwd).at[:, 48:64, 128:256].set(b_gate_w_bwd).astype(BF16)
    bgate = jnp.concatenate([b_gate_b_fwd, b_gate_b_bwd], axis=-1)[:, None, :]
    dqg = jnp.pad(d_qnorm_g, ((0, 0), (0, 256 - D_Q_RANK)))[:, None, :]
    uq = d_w_uq.reshape(depth, D_Q_RANK, 4, 96)
    wuq = jnp.concatenate([uq[..., :64].reshape(depth, D_Q_RANK, 256), uq[..., 64:].reshape(depth, D_Q_RANK, 128)], axis=-1)
    wuq = jnp.pad(wuq, ((0, 0), (0, 256 - D_Q_RANK), (0, 0))).astype(BF16)
    ukv = d_w_ukv.reshape(depth, 128, 4, 128)
    eye4 = jnp.eye(4, dtype=F32)
    wuk = (ukv[..., :64].transpose(0, 2, 3, 1)[:, :, :, None, :] * eye4[None, :, None, :, None])
    wuk = wuk.reshape(depth, 256, 512).astype(BF16)
    wuv = (ukv[..., 64:].transpose(0, 2, 1, 3)[:, :, :, None, :] * eye4[None, :, None, :, None])
    wuv = wuv.reshape(depth, 512, 256).astype(BF16)
    wb0 = w_branch[:, 0].reshape(depth, 2, 2, 64, D_MODEL).transpose(0, 2, 1, 3, 4).reshape(depth, 256, D_MODEL)
    wb = jnp.concatenate([wb0[:, None], w_branch[:, 1:]], axis=1).astype(BF16)
    wout = w_out.astype(BF16)
    wu, wg, wd = w_ffu.astype(BF16), w_ffg.astype(BF16), w_ffd.astype(BF16)
    cw = jnp.pad(conv_w, ((0, 0), (0, 8 - conv_w.shape[1]), (0, 0)))
    aqg = jnp.tile(a_qnorm_g, (1, 4))[:, None, :]
    akg = jnp.tile(a_knorm_g, (1, 2))[:, None, :]
    bong = jnp.tile(b_onorm_g, (1, 4))[:, None, :]
    cong = jnp.tile(c_onorm_g, (1, 4))[:, None, :]
    lamv = jnp.stack([c_lq1, c_lk1, c_lq2, c_lk2], axis=1)
    caches = (cache_a_k.reshape(nd, depth, PAST_LEN, 128), cache_a_v.reshape(nd, depth, PAST_LEN, 128),
              cache_c_k.reshape(nd, depth, PAST_LEN, 256), cache_c_v.reshape(nd, depth, PAST_LEN, 256),
              cache_d_ckv, cache_d_krope)
    expand = lambda s: (s.transpose(0, 1, 2, 4, 3)[:, :, :, :, None, :]
                        * eye4[None, None, :, None, :, None]).reshape(nd, depth, 256, 128)
    s0f_all, s0b_all = expand(state_b_fwd), expand(state_b_bwd)

    n1g, n2g = norm1_g[:, None, :], norm2_g[:, None, :]
    small = (aqg, akg, bd, wgate, bgate, dqg, wuq, wuk, d_kvnorm_g[:, None, :], tile)
    cb = conv_b[:, None, :]
    new_caches = None
    sfs, sbs = [], []
    x = (x_prompt, x_sample)
    for l in range(depth):
        lam_init = 0.8 - 0.6 * math.exp(-0.3 * l)
        fa, kva, qa, new_caches = _inproj_calls(x, n, mod4, n1g, w_aq, w_rest, rope_tab, small, new_caches, l,
                                                depth, n_ctx)
        oacd = _attn_calls(kva, qa, caches, lamv, cong, wuv, bd, tile, l, lam_init, n_ctx)
        ob, sf, sb = _gla_calls(fa, s0f_all, s0b_all, bong, bd, tri, l, n_ctx)
        x1 = _merge_call(x, n, mod4, n1g, w_gates, oacd, ob, wb, wout, l, n_ctx)
        res = _ffn_call(x1, mod4, n2g, wu, wg, cw, cb, wd, final_g[None], l, n_ctx, l == depth - 1)
        x = res[0]
        sfs.append(sf)
        sbs.append(sb)

    diag = lambda s: jnp.stack([s[:, :, hh * 64:(hh + 1) * 64, hh * 32:(hh + 1) * 32] for hh in range(4)],
                               axis=2).transpose(0, 1, 2, 4, 3)
    y_prompt, y_sample = res
    nak, nav, nck, ncv, nckv, nkr = new_caches
    return (y_prompt, y_sample, nak.reshape(nb, depth, SEQ, 2, 64), nav.reshape(nb, depth, SEQ, 2, 64),
            diag(jnp.stack(sfs, axis=1)), diag(jnp.stack(sbs, axis=1)), nck.reshape(nb, depth, SEQ, 4, 2, 32),
            ncv.reshape(nb, depth, SEQ, 4, 64), nckv, nkr)
```

```python
import functools
import math

import numpy as np
import jax
import jax.numpy as jnp
from jax import lax
from jax.experimental import pallas as pl
from jax.experimental.pallas import tpu as pltpu

F32 = jnp.float32
BF16 = jnp.bfloat16

D_MODEL = 1024
SEQ = 256
DEC_SEQ = 1024
PAST_LEN = 512
GRID_W = 64
ROPE_BASE = 10000.0
EPS = 1e-6
LOG2E = math.log2(math.e)
HEAD_W = 64
BRANCH_WIDTH = 256
GLA_KEY_W = 32
GLA_CHUNK = 64
GLA_SUB = 16
GLA_BLOCK = 256
GLA_GATE_NORM = 16.0
D_Q_RANK = 192
D_FF = 2816
FF_CHUNK = 256
N_MOD_ROWS = 16

ROW_TILE = 512
FFN_TILE = 1024
Q_TILE = 256
CTX_SEQ_PER_STEP = 2
HALO = 16
VMEM_LIMIT = 56 * 1024 * 1024

IN_AQ_W = 256
IN_GATE_IN = (1280, 1312)
NZ = 2176
Z_AK, Z_AV, Z_BQ, Z_BK, Z_BV, Z_BR, Z_CQ, Z_CK, Z_CV, Z_DQ = 0, 128, 256, 384, 512, 768, 1024, 1280, 1536, 1792
GATES_OFF = 2432
F_W = 1024
F_BQ, F_BK, F_BV, F_BR, F_LGF, F_LGB = 0, 128, 256, 512, 768, 896
KV_W = 1024
KV_AK, KV_AV, KV_CK, KV_CV, KV_DKV, KV_KR = 0, 128, 256, 512, 768, 896
Q_W = 1152
Q_AQ, Q_CQ, Q_DLAT, Q_DQR = 0, 256, 512, 1024
ROPE_W = 768


def _dot(a, b):
    return jnp.dot(a, b, preferred_element_type=F32)


def _dot_nt(a, b):
    return lax.dot_general(a, b, (((1,), (1,)), ((), ())), preferred_element_type=F32)


def _dot_tn(a, b):
    return lax.dot_general(a, b, (((0,), (0,)), ((), ())), preferred_element_type=F32)


def _split3(a):
    hi = a.astype(BF16)
    r1 = a - hi.astype(F32)
    mid = r1.astype(BF16)
    lo = (r1 - mid.astype(F32)).astype(BF16)
    return hi, mid, lo


def _split_dot(a, b01):
    hi, mid, lo = _split3(a)
    return _dot(hi, b01) + _dot(mid, b01) + _dot(lo, b01)


def _split_dot_tn(b01, a):
    hi, mid, lo = _split3(a)
    return _dot(b01, hi) + _dot(b01, mid) + _dot(b01, lo)


def _group_rms(v, bd, width, gain):
    ss = _split_dot(v * v, bd)
    return v * lax.rsqrt(ss * (1.0 / width) + EPS) * gain


def _rope128(v, cos, sin_a, sin_b, shift):
    return v * cos + pltpu.roll(v, 128 - shift, 1) * sin_a + pltpu.roll(v, shift, 1) * sin_b


def _modulated_norm(x, gain, shift, scale):
    ms = jnp.mean(x * x, axis=-1, keepdims=True)
    return (x * lax.rsqrt(ms + EPS) * gain) * (1.0 + scale) + shift


def _lane_mask(width, lo, hi):
    lane = lax.broadcasted_iota(jnp.int32, (1, width), 1)
    return (lane >= lo) & (lane < hi)


def _const_spec(shape, grid_rank):
    zeros = tuple(0 for _ in shape)
    if grid_rank == 1:
        return pl.BlockSpec(shape, lambda i: zeros)
    return pl.BlockSpec(shape, lambda b, i: zeros)


def _layer_spec(shape, layer, grid_rank):
    idx = (layer,) + tuple(0 for _ in shape)
    once = pl.Buffered(1)
    if grid_rank == 1:
        return pl.BlockSpec((None,) + tuple(shape), lambda i: idx, pipeline_mode=once)
    return pl.BlockSpec((None,) + tuple(shape), lambda b, i: idx, pipeline_mode=once)


def _mod_kernel(cv_ref, w_ref, b_ref, o_ref):
    cv = cv_ref[...]
    a = cv * jax.nn.sigmoid(cv)
    o_ref[...] = _dot(a.astype(BF16), w_ref[...].astype(BF16)) + b_ref[...]


def _mod_call(cvec, w_mod, b_mod):
    depth = w_mod.shape[0]
    nblk = 4
    bw = 6 * D_MODEL // nblk
    return pl.pallas_call(
        _mod_kernel,
        grid=(depth, nblk),
        in_specs=[
            pl.BlockSpec((N_MOD_ROWS, D_MODEL), lambda l, j: (0, 0)),
            pl.BlockSpec((None, D_MODEL, bw), lambda l, j: (l, 0, j)),
            pl.BlockSpec((None, 1, bw), lambda l, j: (l, 0, j)),
        ],
        out_specs=pl.BlockSpec((None, N_MOD_ROWS, bw), lambda l, j: (l, 0, j)),
        out_shape=jax.ShapeDtypeStruct((depth, N_MOD_ROWS, 6 * D_MODEL), F32),
        compiler_params=pltpu.CompilerParams(vmem_limit_bytes=VMEM_LIMIT),
        name="mod_vectors",
    )(cvec, w_mod, b_mod.reshape(depth, 1, 6 * D_MODEL))


def _inproj_body(x, mod_ref, n1g_ref, waq_ref, w_ref, rope_ref, aqg_ref, akg_ref, bd_ref, wgate_ref, bgate_ref,
                 dqg_ref, wuq_ref, wuk_ref, dkvg_ref, tile_ref, f_ref, kv_ref, q_ref, h_ref, cache_refs):
    mod = mod_ref[...]
    h = _modulated_norm(x, n1g_ref[...], mod[:, 0:D_MODEL], mod[:, D_MODEL:2 * D_MODEL]).astype(BF16)
    h_ref[...] = h
    z_aq = _dot(h, waq_ref[...])
    z = _dot(h, w_ref[...])
    zd = pltpu.roll(z[:, NZ - 256:NZ], 192, 1)
    bd = bd_ref[...]
    bd128 = bd[0:128, 0:128]
    if rope_ref is None:
        rope64 = rope32 = lambda v: v
    else:
        rope = rope_ref[...]
        cos64, sa64, sb64 = rope[:, 0:128], rope[:, 128:256], rope[:, 256:384]
        cos32, sa32, sb32 = rope[:, 384:512], rope[:, 512:640], rope[:, 640:768]
        rope64 = lambda v: _rope128(v, cos64, sa64, sb64, 16)
        rope32 = lambda v: _rope128(v, cos32, sa32, sb32, 8)

    def emit_kv(col, cache_idx, val, kv_val=None):
        kv_val = val if kv_val is None else kv_val
        kv_ref[:, col:col + kv_val.shape[1]] = kv_val.astype(BF16)
        if cache_refs is not None:
            ref = cache_refs[cache_idx]
            for j in range(val.shape[0] // SEQ):
                ref[j] = val[j * SEQ:(j + 1) * SEQ, 0:ref.shape[-1]]

    aq = _group_rms(z_aq, bd, HEAD_W, aqg_ref[...])
    a_scale = HEAD_W ** -0.5 * LOG2E
    q_ref[:, Q_AQ:Q_AQ + 128] = (rope64(aq[:, 0:128]) * a_scale).astype(BF16)
    q_ref[:, Q_AQ + 128:Q_AQ + 256] = (rope64(aq[:, 128:256]) * a_scale).astype(BF16)
    emit_kv(KV_AK, 0, rope64(_group_rms(z[:, Z_AK:Z_AK + 128], bd128, HEAD_W, akg_ref[...])))
    emit_kv(KV_AV, 1, z[:, Z_AV:Z_AV + 128])

    f_ref[:, F_BQ:F_BK] = z[:, Z_BQ:Z_BK] * (GLA_KEY_W ** -0.5)
    f_ref[:, F_BK:F_BV] = z[:, Z_BK:Z_BV]
    f_ref[:, F_BV:F_BR] = z[:, Z_BV:Z_BR]
    f_ref[:, F_BR:F_LGF] = z[:, Z_BR:Z_CQ]
    pre = _dot(zd[:, 128:256].astype(BF16), wgate_ref[...]) + bgate_ref[...]
    logsig = jnp.minimum(pre, 0.0) - jnp.log1p(jnp.exp(-jnp.abs(pre)))
    f_ref[:, F_LGF:F_W] = logsig / GLA_GATE_NORM

    c_scale = 32 ** -0.5 * LOG2E
    q_ref[:, Q_CQ:Q_CQ + 128] = (rope32(z[:, Z_CQ:Z_CQ + 128]) * c_scale).astype(BF16)
    q_ref[:, Q_CQ + 128:Q_CQ + 256] = (rope32(z[:, Z_CQ + 128:Z_CQ + 256]) * c_scale).astype(BF16)
    emit_kv(KV_CK, 2, jnp.concatenate([rope32(z[:, Z_CK:Z_CK + 128]), rope32(z[:, Z_CK + 128:Z_CK + 256])],
                                      axis=-1))
    emit_kv(KV_CV, 3, z[:, Z_CV:Z_DQ])

    dkv = zd[:, 0:128]
    ms = jnp.mean(dkv * dkv, axis=-1, keepdims=True)
    emit_kv(KV_DKV, 4, dkv * lax.rsqrt(ms + EPS) * dkvg_ref[...])
    kr = rope32(zd[:, 128:256])
    emit_kv(KV_KR, 5, kr, _dot(kr[:, 0:32].astype(BF16), tile_ref[...]))
    zq = jnp.where(_lane_mask(256, 0, D_Q_RANK), z[:, Z_DQ:Z_DQ + 256], 0.0)
    msq = jnp.sum(zq * zq, axis=-1, keepdims=True) * (1.0 / D_Q_RANK)
    qn = zq * lax.rsqrt(msq + EPS) * dqg_ref[...]
    dq = _dot(qn.astype(BF16), wuq_ref[...])
    d_scale = 96 ** -0.5 * LOG2E
    q_ref[:, Q_DLAT:Q_DQR] = (_dot(dq[:, 0:256].astype(BF16), wuk_ref[...]) * d_scale).astype(BF16)
    q_ref[:, Q_DQR:Q_W] = (rope32(dq[:, 256:384]) * d_scale).astype(BF16)


def _inproj_ctx_kernel(x_ref, mod_ref, n1g_ref, waq_ref, w_ref, aqg_ref, akg_ref, bd_ref, wgate_ref, bgate_ref,
                       dqg_ref, wuq_ref, wuk_ref, dkvg_ref, tile_ref, *rest):
    outs = rest[-10:]
    x = x_ref[...].reshape(-1, D_MODEL)
    _inproj_body(x, mod_ref, n1g_ref, waq_ref, w_ref, None, aqg_ref, akg_ref, bd_ref, wgate_ref, bgate_ref,
                 dqg_ref, wuq_ref, wuk_ref, dkvg_ref, tile_ref, outs[0], outs[1], outs[2], outs[3], outs[4:])


def _inproj_lat_kernel(x_ref, mod_ref, n1g_ref, waq_ref, w_ref, rope_ref, aqg_ref, akg_ref, bd_ref, wgate_ref,
                       bgate_ref, dqg_ref, wuq_ref, wuk_ref, dkvg_ref, tile_ref, pf_ref, pkv_ref, pq_ref, ph_ref,
                       f_ref, kv_ref, q_ref, h_ref):
    del pf_ref, pkv_ref, pq_ref, ph_ref
    _inproj_body(x_ref[...], mod_ref, n1g_ref, waq_ref, w_ref, rope_ref, aqg_ref, akg_ref, bd_ref, wgate_ref,
                 bgate_ref, dqg_ref, wuq_ref, wuk_ref, dkvg_ref, tile_ref, f_ref, kv_ref, q_ref, h_ref, None)


CACHE_WIDTHS = (128, 128, 256, 256, 128, 32)


def _inproj_calls(x, n, mod4, n1g, w_aq, w_rest, rope_tab, small, prev_caches, layer, depth, n_ctx_rows):
    tm = ROW_TILE
    ctx_tiles = n_ctx_rows // tm
    seq_tiles = DEC_SEQ // tm
    spt = tm // SEQ
    n_ctx_seq = n_ctx_rows // SEQ
    if isinstance(x, tuple):
        x_ctx, x_lat = x
        x_ctx_spec = pl.BlockSpec((spt, SEQ, D_MODEL), lambda i: (i, 0, 0))
        x_lat_spec = pl.BlockSpec((None, tm, D_MODEL), lambda i: (i // seq_tiles, i % seq_tiles, 0))
    else:
        x_ctx = x_lat = x
        x_ctx_spec = pl.BlockSpec((tm, D_MODEL), lambda i: (i, 0))
        x_lat_spec = pl.BlockSpec((tm, D_MODEL), lambda i: (i + ctx_tiles, 0))
    const = lambda shape: _layer_spec(shape, layer, 1)
    small_specs = [const((1, 256)), const((1, 128)), _const_spec((256, 256), 1), const((128, 256)),
                   const((1, 256)), const((1, 256)), const((256, 384)), const((256, 512)), const((1, 128)),
                   _const_spec((32, 128), 1)]
    act_shapes = [jax.ShapeDtypeStruct((n, F_W), F32), jax.ShapeDtypeStruct((n, KV_W), BF16),
                  jax.ShapeDtypeStruct((n, Q_W), BF16), jax.ShapeDtypeStruct((n, D_MODEL), BF16)]
    act_specs = lambda off: [pl.BlockSpec((tm, w), lambda i: (i + off, 0)) for w in (F_W, KV_W, Q_W, D_MODEL)]
    n_act = len(act_shapes)
    cparams = pltpu.CompilerParams(vmem_limit_bytes=VMEM_LIMIT)
    any_spec = pl.BlockSpec(memory_space=pl.ANY)

    n_in = 5 + len(small)
    alias = {} if prev_caches is None else {n_in + k: n_act + k for k in range(6)}
    res = pl.pallas_call(
        _inproj_ctx_kernel,
        grid=(ctx_tiles,),
        in_specs=[
            x_ctx_spec,
            pl.BlockSpec((None, None, 1, 6 * D_MODEL), lambda i: (layer, 0, 0, 0)),
            const((1, D_MODEL)),
            const((D_MODEL, IN_AQ_W)),
            const((D_MODEL, NZ)),
        ] + small_specs + ([] if prev_caches is None else [any_spec] * 6),
        out_specs=act_specs(0) + [pl.BlockSpec((spt, None, SEQ, w), lambda i: (i, layer, 0, 0))
                                  for w in CACHE_WIDTHS],
        out_shape=act_shapes + [jax.ShapeDtypeStruct((n_ctx_seq, depth, SEQ, w), F32) for w in CACHE_WIDTHS],
        input_output_aliases=alias,
        compiler_params=cparams,
        name="in_projection_context",
    )(x_ctx, mod4, n1g, w_aq, w_rest, *small, *([] if prev_caches is None else prev_caches))
    acts_c, caches = res[:n_act], tuple(res[n_act:])

    fa, kva, qa, ha = pl.pallas_call(
        _inproj_lat_kernel,
        grid=((n - n_ctx_rows) // tm,),
        in_specs=[
            x_lat_spec,
            pl.BlockSpec((None, None, 1, 6 * D_MODEL), lambda i: (layer, 1 + i // seq_tiles, 0, 0)),
            const((1, D_MODEL)),
            const((D_MODEL, IN_AQ_W)),
            const((D_MODEL, NZ)),
            pl.BlockSpec((tm, ROPE_W), lambda i: (i % seq_tiles, 0)),
        ] + small_specs + [any_spec] * n_act,
        out_specs=act_specs(ctx_tiles),
        out_shape=act_shapes,
        input_output_aliases={n_in + 1 + k: k for k in range(n_act)},
        compiler_params=cparams,
        name="in_projection_latent",
    )(x_lat, mod4, n1g, w_aq, w_rest, rope_tab, *small, *acts_c)
    return fa, kva, qa, ha, caches


def _exp2_parts(parts):
    m = parts[0].max(axis=-1, keepdims=True)
    for s in parts[1:]:
        m = jnp.maximum(m, s.max(axis=-1, keepdims=True))
    return [jnp.exp2(s - m) for s in parts]


def _value_variants(av, cv, dk):
    one = jnp.ones((), BF16)
    av1 = [jnp.where(_lane_mask(128, hh * HEAD_W, (hh + 1) * HEAD_W), av, one) for hh in range(2)]
    cv1 = [jnp.where(_lane_mask(256, hh * HEAD_W, (hh + 1) * HEAD_W), cv, one) for hh in range(4)]
    vlat = jnp.where(_lane_mask(256, 0, 128), dk, one)
    return av1, cv1, vlat


def _attn_core(q, ak, av1, ck, cv1, dk, vlat, lam_ref, cong_ref, wuv_ref, bd_ref, o_ref, lam_init):
    zero = jnp.zeros((), BF16)
    ak, ck, dk = [ak], [ck], [dk]
    tq = q.shape[0]

    masks_a = [_lane_mask(128, hh * HEAD_W, (hh + 1) * HEAD_W) for hh in range(2)]
    qa = jnp.concatenate([jnp.where(masks_a[hh], q[:, Q_AQ + g * 128:Q_AQ + (g + 1) * 128], zero)
                          for hh in range(2) for g in range(2)], axis=0)
    qcs = [jnp.concatenate([jnp.where(_lane_mask(128, u * 32, (u + 1) * 32),
                                      q[:, Q_CQ + c * 128:Q_CQ + (c + 1) * 128], zero) for u in range(4)], axis=0)
           for c in range(2)]
    qd = jnp.concatenate(
        [jnp.concatenate([q[:, Q_DLAT + hh * 128:Q_DLAT + (hh + 1) * 128],
                          jnp.where(_lane_mask(128, hh * 32, (hh + 1) * 32), q[:, Q_DQR:Q_W], zero)], axis=-1)
         for hh in range(4)], axis=0)
    s_a = [_dot_nt(qa, k) for k in ak]
    s_c = [[_dot_nt(qcs[c], k[:, c * 128:(c + 1) * 128]) for k in ck] for c in range(2)]
    s_d = [_dot_nt(qd, k) for k in dk]

    ps = [p.astype(BF16) for p in _exp2_parts(s_a)]
    oa = [None, None]
    for hh in range(2):
        rows = slice(hh * 2 * tq, (hh + 1) * 2 * tq)
        o = _dot(ps[0][rows], av1[hh])
        o = jnp.where(masks_a[hh], o / pltpu.roll(o, HEAD_W, 1), 0.0)
        for g in range(2):
            part = o[g * tq:(g + 1) * tq]
            oa[g] = part if oa[g] is None else oa[g] + part
    o_ref[:, 0:128] = oa[0]
    o_ref[:, 128:256] = oa[1]

    lv = lam_ref[...]
    lam = (jnp.exp(jnp.sum(lv[0:1] * lv[1:2], axis=-1, keepdims=True))
           - jnp.exp(jnp.sum(lv[2:3] * lv[3:4], axis=-1, keepdims=True)) + lam_init)
    oc = []
    for c in range(2):
        c0 = c * 128
        ps = [p.astype(BF16) for p in _exp2_parts(s_c[c])]
        och = None
        for h2 in range(2):
            hh = c * 2 + h2
            rows = slice(h2 * 2 * tq, (h2 + 1) * 2 * tq)
            o = _dot(ps[0][rows], cv1[hh])
            res = o[:, c0:c0 + 128] / o[:, 128 - c0:256 - c0]
            o = jnp.where(_lane_mask(128, h2 * HEAD_W, (h2 + 1) * HEAD_W), res[0:tq] - lam * res[tq:2 * tq], 0.0)
            och = o if och is None else och + o
        oc.append(och)
    oc = jnp.concatenate(oc, axis=-1)
    o_ref[:, 256:512] = _group_rms(oc, bd_ref[...], HEAD_W, cong_ref[...]) * (1.0 - lam_init)

    ps = [p.astype(BF16) for p in _exp2_parts(s_d)]
    o = _dot(ps[0], vlat)
    lat = (o[:, 0:128] / o[:, 128:256]).astype(BF16)
    o_ref[:, 512:768] = _dot(jnp.concatenate([lat[hh * tq:(hh + 1) * tq] for hh in range(4)], axis=-1),
                             wuv_ref[...])


def _attn_ctx_kernel(q_ref, kva_ref, ck_ref, cv_ref, dkv_ref, lam_ref, cong_ref, wuv_ref, bd_ref,
                     o_ref, *, lam_init):
    for j in range(q_ref.shape[0] // SEQ):
        rows = slice(j * SEQ, (j + 1) * SEQ)
        kva = kva_ref[rows, :]
        dk = dkv_ref[rows, :]
        av1, cv1, vlat = _value_variants(kva[:, 128:256], cv_ref[rows, :], dk)
        _attn_core(q_ref[rows, :], kva[:, 0:128], av1, ck_ref[rows, :], cv1, dk, vlat, lam_ref, cong_ref, wuv_ref,
                   bd_ref, o_ref.at[rows, :], lam_init)


def _attn_lat_kernel(q_ref, kva_ref, ck_ref, cv_ref, dkv_ref, cak_ref, cav_ref, cck_ref, ccv_ref,
                     cckv_ref, ckr_ref, lam_ref, cong_ref, wuv_ref, bd_ref, tile_ref, prev_ref, o_ref,
                     ak_s, ck_s, dk_s, av1_s, cv1_s, vlat_s, *, lam_init):
    del prev_ref

    @pl.when(pl.program_id(1) == 0)
    def _():
        p = PAST_LEN
        kva = kva_ref[...]
        kr4 = _dot(ckr_ref[...].astype(BF16), tile_ref[...]).astype(BF16)
        pieces = ((cak_ref[...].astype(BF16), cav_ref[...].astype(BF16), cck_ref[...].astype(BF16),
                   ccv_ref[...].astype(BF16), jnp.concatenate([cckv_ref[...].astype(BF16), kr4], axis=-1)),
                  (kva[:, 0:128], kva[:, 128:256], ck_ref[...], cv_ref[...], dkv_ref[...]))
        for rows, (ak, av, ck, cv, dk) in zip((slice(0, p), slice(p, p + DEC_SEQ)), pieces):
            av1, cv1, vlat = _value_variants(av, cv, dk)
            ak_s[rows] = ak
            ck_s[rows] = ck
            dk_s[rows] = dk
            vlat_s[rows] = vlat
            for hh in range(2):
                av1_s[hh, rows] = av1[hh]
            for hh in range(4):
                cv1_s[hh, rows] = cv1[hh]

    _attn_core(q_ref[...], ak_s[...], [av1_s[hh] for hh in range(2)], ck_s[...],
               [cv1_s[hh] for hh in range(4)], dk_s[...], vlat_s[...], lam_ref, cong_ref, wuv_ref, bd_ref,
               o_ref, lam_init)


def _attn_calls(kva, qa, caches, lamv, cong, wuv, bd, tile, layer, lam_init, n_ctx_rows):
    n = kva.shape[0]
    n_ctx_seq = n_ctx_rows // SEQ
    n_lat_seq = (n - n_ctx_rows) // DEC_SEQ
    param_specs = lambda rank: [_layer_spec((4, 32), layer, rank), _layer_spec((1, 256), layer, rank),
                                _layer_spec((512, 256), layer, rank), _const_spec((256, 256), rank),
                                _const_spec((32, 128), rank)]
    out_shape = jax.ShapeDtypeStruct((n, 3 * BRANCH_WIDTH), F32)
    cparams = pltpu.CompilerParams(vmem_limit_bytes=VMEM_LIMIT)
    crows = CTX_SEQ_PER_STEP * SEQ

    o_ctx = pl.pallas_call(
        functools.partial(_attn_ctx_kernel, lam_init=lam_init),
        grid=(n_ctx_seq // CTX_SEQ_PER_STEP,),
        in_specs=[
            pl.BlockSpec((crows, Q_W), lambda b: (b, 0)),
            pl.BlockSpec((crows, 256), lambda b: (b, KV_AK // 256)),
            pl.BlockSpec((crows, 256), lambda b: (b, KV_CK // 256)),
            pl.BlockSpec((crows, 256), lambda b: (b, KV_CV // 256)),
            pl.BlockSpec((crows, 256), lambda b: (b, KV_DKV // 256)),
        ] + param_specs(1)[:-1],
        out_specs=pl.BlockSpec((crows, 3 * BRANCH_WIDTH), lambda b: (b, 0)),
        out_shape=out_shape,
        compiler_params=cparams,
        name="attention_context",
    )(qa, kva, kva, kva, kva, lamv, cong, wuv, bd)

    qt = Q_TILE
    q_per_seq = DEC_SEQ // qt
    q0 = n_ctx_rows // qt
    s0 = n_ctx_rows // DEC_SEQ
    cak, cav, cck, ccv, cckv, ckr = caches
    cache_spec = lambda w: pl.BlockSpec((None, None, PAST_LEN, w), lambda b, i: (b, layer, 0, 0))
    n_keys = PAST_LEN + DEC_SEQ
    o_all = pl.pallas_call(
        functools.partial(_attn_lat_kernel, lam_init=lam_init),
        grid=(n_lat_seq, q_per_seq),
        in_specs=[
            pl.BlockSpec((qt, Q_W), lambda b, i: (q0 + b * q_per_seq + i, 0)),
            pl.BlockSpec((DEC_SEQ, 256), lambda b, i: (s0 + b, KV_AK // 256)),
            pl.BlockSpec((DEC_SEQ, 256), lambda b, i: (s0 + b, KV_CK // 256)),
            pl.BlockSpec((DEC_SEQ, 256), lambda b, i: (s0 + b, KV_CV // 256)),
            pl.BlockSpec((DEC_SEQ, 256), lambda b, i: (s0 + b, KV_DKV // 256)),
            cache_spec(128), cache_spec(128), cache_spec(256), cache_spec(256), cache_spec(128),
            cache_spec(32),
        ] + param_specs(2) + [pl.BlockSpec(memory_space=pl.ANY)],
        out_specs=pl.BlockSpec((qt, 3 * BRANCH_WIDTH), lambda b, i: (q0 + b * q_per_seq + i, 0)),
        out_shape=out_shape,
        scratch_shapes=[pltpu.VMEM((n_keys, 128), BF16), pltpu.VMEM((n_keys, 256), BF16),
                        pltpu.VMEM((n_keys, 256), BF16), pltpu.VMEM((2, n_keys, 128), BF16),
                        pltpu.VMEM((4, n_keys, 256), BF16), pltpu.VMEM((n_keys, 256), BF16)],
        input_output_aliases={16: 0},
        compiler_params=cparams,
        name="attention_latent",
    )(qa, kva, kva, kva, kva, cak, cav, cck, ccv, cckv, ckr, lamv, cong, wuv, bd, tile, o_ctx)
    return o_all


def _gla_chunk(q, k, v, bc, st, reverse, masks):
    c, sub, nsub = GLA_CHUNK, GLA_SUB, GLA_CHUNK // GLA_SUB
    keep, qmask, src_row, blk = masks
    edge = bc[0:1] if reverse else bc[c - 1:c]
    qs, kd = [], []
    for i in range(nsub):
        r0 = i * sub
        ref = bc[r0 + sub - 1:r0 + sub] if reverse else bc[r0:r0 + 1]
        qd = q[r0:r0 + sub] * jnp.exp(bc[r0:r0 + sub] - ref)
        qs.append(jnp.where(qmask, jnp.concatenate([qd] * 4, axis=0), 0.0))
        in_range = (src_row >= r0) if reverse else (src_row < r0 + sub)
        kd.append(k * jnp.exp(jnp.where(in_range, ref - bc, 0.0)))
    att = _dot_nt(jnp.concatenate(qs, axis=0).astype(BF16), jnp.concatenate(kd, axis=0).astype(BF16))
    att = jnp.where(keep, att, 0.0).astype(BF16)
    vb = v.astype(BF16)
    pv = _dot(att, jnp.concatenate([vb] * nsub, axis=0))
    out_blocks = []
    for i in range(nsub):
        o = None
        for hh in range(4):
            r0 = i * c + hh * sub
            part = jnp.where(_lane_mask(256, hh * HEAD_W, (hh + 1) * HEAD_W), pv[r0:r0 + sub], 0.0)
            o = part if o is None else o + part
        out_blocks.append(o)
    o_intra = jnp.concatenate(out_blocks, axis=0)
    q_dec = q * jnp.exp(bc)
    k_dec = k * jnp.exp(edge - bc)
    o_inter = _dot_nt(q_dec.astype(BF16), st.astype(BF16))
    upd = _dot_tn(vb, k_dec.astype(BF16))
    st_new = st * jnp.exp(edge) + jnp.where(blk, upd, 0.0)
    return o_intra + o_inter, st_new


def _gla_masks(reverse):
    c, sub = GLA_CHUNK, GLA_SUB
    row = lax.broadcasted_iota(jnp.int32, (4 * c, 4 * c), 0)
    col = lax.broadcasted_iota(jnp.int32, (4 * c, 4 * c), 1)
    t = (row // c) * sub + row % sub
    s = col % c
    keep = (row // c == col // c) & ((s >= t) if reverse else (s <= t))
    qmask = (lax.broadcasted_iota(jnp.int32, (4 * sub, 128), 0) // sub
             == lax.broadcasted_iota(jnp.int32, (4 * sub, 128), 1) // GLA_KEY_W)
    src_row = lax.broadcasted_iota(jnp.int32, (c, 1), 0)
    blk = (lax.broadcasted_iota(jnp.int32, (256, 128), 0) // HEAD_W
           == lax.broadcasted_iota(jnp.int32, (256, 128), 1) // GLA_KEY_W)
    return keep, qmask, src_row, blk


def _gla_body(qk_ref, v_ref, r_ref, lg_ref, s0f, s0b, g_ref, bd_ref, tri_ref, o_ref, accf_ref, accb_ref,
              bcf_ref, bcb_ref, seq_len, unroll):
    nc = seq_len // GLA_CHUNK
    for r0 in range(0, seq_len, GLA_BLOCK):
        lg = lg_ref[r0:r0 + GLA_BLOCK, :]
        for d, bc_ref in ((0, bcf_ref), (1, bcb_ref)):
            res = _dot(tri_ref[d], jnp.concatenate(_split3(lg[:, d * 128:(d + 1) * 128]), axis=-1))
            bc_ref[r0:r0 + GLA_BLOCK, :] = res[:, 0:128] + res[:, 128:256] + res[:, 256:384]
    masks_f = _gla_masks(False)
    masks_b = _gla_masks(True)

    def step(ci, carry):
        sf, sb = carry
        rows = pl.ds(pl.multiple_of(ci * GLA_CHUNK, GLA_CHUNK), GLA_CHUNK)
        o, sf = _gla_chunk(qk_ref[rows, 0:128], qk_ref[rows, 128:256], v_ref[rows, :], bcf_ref[rows, :],
                           sf, False, masks_f)
        accf_ref[rows, :] = o
        rows = pl.ds(pl.multiple_of((nc - 1 - ci) * GLA_CHUNK, GLA_CHUNK), GLA_CHUNK)
        o, sb = _gla_chunk(qk_ref[rows, 0:128], qk_ref[rows, 128:256], v_ref[rows, :], bcb_ref[rows, :],
                           sb, True, masks_b)
        accb_ref[rows, :] = o
        return sf, sb

    sf, sb = lax.fori_loop(0, nc, step, (s0f, s0b), unroll=unroll)
    r = r_ref[...]
    o_ref[...] = (_group_rms(accf_ref[...] + accb_ref[...], bd_ref[...], HEAD_W, g_ref[...])
                  * (r * jax.nn.sigmoid(r)))
    return sf, sb


def _gla_ctx_kernel(qk_ref, v_ref, r_ref, lg_ref, g_ref, bd_ref, tri_ref, o_ref, sf_ref, sb_ref, *scratch):
    zero = jnp.zeros((256, 128), F32)
    sf, sb = _gla_body(qk_ref, v_ref, r_ref, lg_ref, zero, zero, g_ref, bd_ref, tri_ref, o_ref, *scratch,
                       SEQ, True)
    sf_ref[...] = sf
    sb_ref[...] = sb


def _gla_lat_kernel(qk_ref, v_ref, r_ref, lg_ref, s0f_ref, s0b_ref, g_ref, bd_ref, tri_ref, prev_ref,
                    o_ref, *scratch):
    del prev_ref
    _gla_body(qk_ref, v_ref, r_ref, lg_ref, s0f_ref[...], s0b_ref[...], g_ref, bd_ref, tri_ref, o_ref,
              *scratch, DEC_SEQ, 2)


def _gla_calls(fa, s0f_t, s0b_t, bong, bd, tri, layer, n_ctx_rows):
    n = fa.shape[0]
    n_ctx_seq = n_ctx_rows // SEQ
    n_lat_seq = (n - n_ctx_rows) // DEC_SEQ
    cparams = pltpu.CompilerParams(vmem_limit_bytes=VMEM_LIMIT)
    const = lambda shape: _const_spec(shape, 1)
    gain_spec = _layer_spec((1, 256), layer, 1)
    state_spec = pl.BlockSpec((None, None, 256, 128), lambda b: (b, layer, 0, 0))
    out_shape = jax.ShapeDtypeStruct((n, BRANCH_WIDTH), F32)
    st_shape = jax.ShapeDtypeStruct((n_ctx_seq, 256, 128), F32)
    o_ctx, sf, sb = pl.pallas_call(
        _gla_ctx_kernel,
        grid=(n_ctx_seq,),
        in_specs=[
            pl.BlockSpec((SEQ, 256), lambda b: (b, F_BQ // 256)),
            pl.BlockSpec((SEQ, 256), lambda b: (b, F_BV // 256)),
            pl.BlockSpec((SEQ, 256), lambda b: (b, F_BR // 256)),
            pl.BlockSpec((SEQ, 256), lambda b: (b, F_LGF // 256)),
            gain_spec, const((256, 256)), const((2, GLA_BLOCK, GLA_BLOCK)),
        ],
        out_specs=[pl.BlockSpec((SEQ, BRANCH_WIDTH), lambda b: (b, 0)),
                   pl.BlockSpec((None, 256, 128), lambda b: (b, 0, 0)),
                   pl.BlockSpec((None, 256, 128), lambda b: (b, 0, 0))],
        out_shape=[out_shape, st_shape, st_shape],
        scratch_shapes=[pltpu.VMEM((SEQ, BRANCH_WIDTH), F32), pltpu.VMEM((SEQ, BRANCH_WIDTH), F32),
                        pltpu.VMEM((SEQ, 128), F32), pltpu.VMEM((SEQ, 128), F32)],
        compiler_params=cparams,
        name="gla_context",
    )(fa, fa, fa, fa, bong, bd, tri)

    s0 = n_ctx_rows // DEC_SEQ
    o_all = pl.pallas_call(
        _gla_lat_kernel,
        grid=(n_lat_seq,),
        in_specs=[
            pl.BlockSpec((DEC_SEQ, 256), lambda b: (s0 + b, F_BQ // 256)),
            pl.BlockSpec((DEC_SEQ, 256), lambda b: (s0 + b, F_BV // 256)),
            pl.BlockSpec((DEC_SEQ, 256), lambda b: (s0 + b, F_BR // 256)),
            pl.BlockSpec((DEC_SEQ, 256), lambda b: (s0 + b, F_LGF // 256)),
            state_spec, state_spec,
            gain_spec, const((256, 256)), const((2, GLA_BLOCK, GLA_BLOCK)),
            pl.BlockSpec(memory_space=pl.ANY),
        ],
        out_specs=pl.BlockSpec((DEC_SEQ, BRANCH_WIDTH), lambda b: (s0 + b, 0)),
        out_shape=out_shape,
        scratch_shapes=[pltpu.VMEM((DEC_SEQ, BRANCH_WIDTH), F32), pltpu.VMEM((DEC_SEQ, BRANCH_WIDTH), F32),
                        pltpu.VMEM((DEC_SEQ, 128), F32), pltpu.VMEM((DEC_SEQ, 128), F32)],
        input_output_aliases={9: 0},
        compiler_params=cparams,
        name="gla_latent",
    )(fa, fa, fa, fa, s0f_t, s0b_t, bong, bd, tri, o_ctx)
    return o_all, sf, sb


def _merge_kernel(*refs, ctx_tiles):
    mod_ref, h_ref, wg_ref, oacd_ref, ob_ref, wb_ref, wout_ref, o_ref = refs[-8:]
    if len(refs) == 9:
        x = refs[0][...]
    else:
        x = jnp.where(pl.program_id(0) < ctx_tiles, refs[0][...].reshape(-1, D_MODEL), refs[1][...])
    mod = mod_ref[...]
    h = h_ref[...]
    oacd = oacd_ref[...]
    branches = [oacd[:, 0:256], ob_ref[...], oacd[:, 256:512], oacd[:, 512:768]]
    acc = None
    for j in range(4):
        gate = jax.nn.sigmoid(_dot(h, wg_ref[:, j * D_MODEL:(j + 1) * D_MODEL]))
        term = gate * _dot(branches[j].astype(BF16), wb_ref[j])
        acc = term if acc is None else acc + term
    mix = _dot(acc.astype(BF16), wout_ref[...])
    o_ref[...] = x + mod[:, 2 * D_MODEL:3 * D_MODEL] * mix


def _mod_map(layer, ctx_tiles, seq_tiles):
    return lambda i: (layer, jnp.where(i < ctx_tiles, 0, 1 + (i - ctx_tiles) // seq_tiles), 0, 0)


def _merge_call(x, n, mod4, ha, wgates, oacd, ob, wb, wout, layer, n_ctx_rows):
    tm = ROW_TILE
    ctx_tiles = n_ctx_rows // tm
    seq_tiles = DEC_SEQ // tm
    const = lambda shape: _layer_spec(shape, layer, 1)
    if isinstance(x, tuple):
        xs = list(x)
        x_specs = [
            pl.BlockSpec((tm // SEQ, SEQ, D_MODEL), lambda i: (jnp.minimum(i, ctx_tiles - 1), 0, 0)),
            pl.BlockSpec((None, tm, D_MODEL), lambda i: (jnp.maximum(i - ctx_tiles, 0) // seq_tiles,
                                                         jnp.maximum(i - ctx_tiles, 0) % seq_tiles, 0)),
        ]
    else:
        xs = [x]
        x_specs = [pl.BlockSpec((tm, D_MODEL), lambda i: (i, 0))]
    return pl.pallas_call(
        functools.partial(_merge_kernel, ctx_tiles=ctx_tiles),
        grid=(n // tm,),
        in_specs=x_specs + [
            pl.BlockSpec((None, None, 1, 6 * D_MODEL), _mod_map(layer, ctx_tiles, seq_tiles)),
            pl.BlockSpec((tm, D_MODEL), lambda i: (i, 0)),
            const((D_MODEL, 4 * D_MODEL)),
            pl.BlockSpec((tm, 3 * BRANCH_WIDTH), lambda i: (i, 0)),
            pl.BlockSpec((tm, BRANCH_WIDTH), lambda i: (i, 0)),
            const((4, BRANCH_WIDTH, D_MODEL)),
            const((D_MODEL, D_MODEL)),
        ],
        out_specs=pl.BlockSpec((tm, D_MODEL), lambda i: (i, 0)),
        out_shape=jax.ShapeDtypeStruct((n, D_MODEL), F32),
        compiler_params=pltpu.CompilerParams(vmem_limit_bytes=VMEM_LIMIT),
        name="merge",
    )(*xs, mod4, ha, wgates, oacd, ob, wb, wout)


def _ffn_kernel(x_ref, xp_ref, xn_ref, mod_ref, n2g_ref, wu_ref, wg_ref, cw_ref, cb_ref, wd_ref, fg_ref,
                *rest, ctx_tiles, final):
    out_refs, (hext_ref, act_ref) = rest[:-2], rest[-2:]
    tm = x_ref.shape[0]
    i = pl.program_id(0)
    x = x_ref[...]
    mod = mod_ref[...]
    sh2, sc2, g2 = mod[:, 3 * D_MODEL:4 * D_MODEL], mod[:, 4 * D_MODEL:5 * D_MODEL], mod[:, 5 * D_MODEL:]
    n2g = n2g_ref[...]
    hext_ref[0:HALO] = _modulated_norm(xp_ref[...], n2g, sh2, sc2).astype(BF16)
    hext_ref[HALO:HALO + tm] = _modulated_norm(x, n2g, sh2, sc2).astype(BF16)
    hext_ref[HALO + tm:] = _modulated_norm(xn_ref[...], n2g, sh2, sc2).astype(BF16)
    seq_len = jnp.where(i < ctx_tiles, SEQ, DEC_SEQ)
    row = lax.broadcasted_iota(jnp.int32, (tm, 1), 0)
    pos = (row + i * tm) & (seq_len - 1)
    first = pos == 0
    last = pos == seq_len - 1
    for c0 in range(0, D_FF, FF_CHUNK):
        u_ext = _dot(hext_ref[...], wu_ref[:, c0:c0 + FF_CHUNK])
        u = u_ext[HALO:HALO + tm]
        u_dn = jnp.where(first, 0.0, u_ext[HALO - 1:HALO - 1 + tm])
        u_up = jnp.where(last, 0.0, u_ext[HALO + 1:HALO + 1 + tm])
        cw = cw_ref[:, c0:c0 + FF_CHUNK]
        uc = u_dn * cw[0:1] + u * cw[1:2] + u_up * cw[2:3] + cb_ref[:, c0:c0 + FF_CHUNK]
        g = _dot(hext_ref[HALO:HALO + tm], wg_ref[:, c0:c0 + FF_CHUNK])
        act_ref[:, c0:c0 + FF_CHUNK] = (jax.nn.gelu(uc) * g).astype(BF16)
    out = x + g2 * _dot(act_ref[...], wd_ref[...])
    if final:
        yp_ref, ys_ref = out_refs
        ms = jnp.mean(out * out, axis=-1, keepdims=True)
        y = out * lax.rsqrt(ms + EPS) * fg_ref[...]

        @pl.when(i < ctx_tiles)
        def _():
            for j in range(tm // SEQ):
                yp_ref[j] = y[j * SEQ:(j + 1) * SEQ]

        @pl.when(i >= ctx_tiles)
        def _():
            ys_ref[...] = y
    else:
        out_refs[0][...] = out


def _ffn_call(x, mod4, n2g, wu, wg, cw, cb, wd, fg, layer, n_ctx_rows, final):
    n = x.shape[0]
    tm = FFN_TILE
    ctx_tiles = n_ctx_rows // tm
    seq_tiles = DEC_SEQ // tm
    hb = tm // HALO
    last_halo = n // HALO - 1
    lspec = lambda shape: _layer_spec(shape, layer, 1)
    if final:
        spt = tm // SEQ
        out_specs = [
            pl.BlockSpec((spt, SEQ, D_MODEL), lambda i: (jnp.minimum(i, ctx_tiles - 1), 0, 0)),
            pl.BlockSpec((None, tm, D_MODEL), lambda i: (jnp.maximum(i - ctx_tiles, 0) // seq_tiles,
                                                         jnp.maximum(i - ctx_tiles, 0) % seq_tiles, 0)),
        ]
        out_shape = [jax.ShapeDtypeStruct((n_ctx_rows // SEQ, SEQ, D_MODEL), F32),
                     jax.ShapeDtypeStruct(((n - n_ctx_rows) // DEC_SEQ, DEC_SEQ, D_MODEL), F32)]
    else:
        out_specs = [pl.BlockSpec((tm, D_MODEL), lambda i: (i, 0))]
        out_shape = [jax.ShapeDtypeStruct((n, D_MODEL), F32)]
    return pl.pallas_call(
        functools.partial(_ffn_kernel, ctx_tiles=ctx_tiles, final=final),
        grid=(n // tm,),
        in_specs=[
            pl.BlockSpec((tm, D_MODEL), lambda i: (i, 0)),
            pl.BlockSpec((HALO, D_MODEL), lambda i: (jnp.maximum(i * hb - 1, 0), 0)),
            pl.BlockSpec((HALO, D_MODEL), lambda i: (jnp.minimum((i + 1) * hb, last_halo), 0)),
            pl.BlockSpec((None, None, 1, 6 * D_MODEL), _mod_map(layer, ctx_tiles, seq_tiles)),
            lspec((1, D_MODEL)),
            lspec((D_MODEL, D_FF)), lspec((D_MODEL, D_FF)), lspec((8, D_FF)), lspec((1, D_FF)),
            lspec((D_FF, D_MODEL)), _const_spec((1, D_MODEL), 1),
        ],
        out_specs=out_specs,
        out_shape=out_shape,
        scratch_shapes=[pltpu.VMEM((tm + 2 * HALO, D_MODEL), BF16), pltpu.VMEM((tm, D_FF), BF16)],
        compiler_params=pltpu.CompilerParams(vmem_limit_bytes=VMEM_LIMIT),
        name="conv_ffn",
    )(x, x, x, mod4, n2g, wu, wg, cw, cb, wd, fg)


def _rope_tables():
    t = np.arange(DEC_SEQ)
    row = (t // GRID_W).astype(np.float32)
    col = (t % GRID_W).astype(np.float32)

    def tables(head_dim):
        r = head_dim // 2
        half = r // 2
        inv = (np.float32(ROPE_BASE) ** (-np.arange(half, dtype=np.float32) / np.float32(half))).astype(np.float32)
        lane = np.arange(128) % head_dim
        use_col = lane >= r
        e = lane % r
        first_half = e < half
        freq = inv[e % half]
        pos = np.where(use_col[None, :], col[:, None], row[:, None]).astype(np.float32)
        ang = (pos * freq[None, :]).astype(np.float32)
        cos = np.cos(ang).astype(np.float32)
        sin = np.sin(ang).astype(np.float32)
        sin_a = np.where(first_half[None, :], -sin, 0.0).astype(np.float32)
        sin_b = np.where(first_half[None, :], 0.0, sin).astype(np.float32)
        return cos, sin_a, sin_b

    return jnp.asarray(np.concatenate(list(tables(64)) + list(tables(32)), axis=1))


def kernel(x_prompt, x_sample, c, cache_a_k, cache_a_v, state_b_fwd, state_b_bwd, cache_c_k, cache_c_v, cache_d_ckv, cache_d_krope, c_ctx, w_mod, b_mod, norm1_g, norm2_g, w_in, a_qnorm_g, a_knorm_g, b_gate_w_fwd, b_gate_b_fwd, b_gate_w_bwd, b_gate_b_bwd, b_onorm_g, c_lq1, c_lk1, c_lq2, c_lk2, c_onorm_g, d_qnorm_g, d_w_uq, d_kvnorm_g, d_w_ukv, w_branch, w_out, w_ffu, w_ffg, conv_w, conv_b, w_ffd, final_g):
    depth = w_in.shape[0]
    nb, nd = x_prompt.shape[0], x_sample.shape[0]
    n_ctx = nb * SEQ
    assert n_ctx % DEC_SEQ == 0 and nd + 1 <= N_MOD_ROWS

    n = n_ctx + nd * DEC_SEQ
    cvec = jnp.zeros((N_MOD_ROWS, D_MODEL), F32).at[0].set(c_ctx).at[1:1 + nd].set(c)
    mod4 = _mod_call(cvec, w_mod, b_mod).reshape(depth, N_MOD_ROWS, 1, 6 * D_MODEL)

    rope_tab = _rope_tables()
    lane = np.arange(256)
    bd = jnp.asarray(lane[:, None] // HEAD_W == lane[None, :] // HEAD_W, BF16)
    tile = jnp.asarray(np.arange(32)[:, None] == np.arange(128)[None, :] % 32, BF16)
    tt = np.arange(GLA_BLOCK)
    same_chunk = tt[:, None] // GLA_CHUNK == tt[None, :] // GLA_CHUNK
    tri = jnp.asarray(np.stack([same_chunk & (tt[:, None] >= tt[None, :]),
                                same_chunk & (tt[:, None] <= tt[None, :])]), BF16)

    w_aq = (w_in[:, :, 0:IN_AQ_W].reshape(depth, D_MODEL, 2, 2, HEAD_W).transpose(0, 1, 3, 2, 4)
            .reshape(depth, D_MODEL, IN_AQ_W).astype(BF16))
    g0, g1 = IN_GATE_IN
    w_rest = jnp.concatenate([w_in[:, :, IN_AQ_W:g0], w_in[:, :, g1:GATES_OFF], w_in[:, :, g0:g1]],
                             axis=-1).astype(BF16)
    w_gates = w_in[:, :, GATES_OFF:].astype(BF16)
    wgate = jnp.zeros((depth, 128, 256), F32)
    wgate = wgate.at[:, 32:48, 0:128].set(b_gate_w_fwd).at[:, 48:64, 128:256].set(b_gate_w_bwd).astype(BF16)
    bgate = jnp.concatenate([b_gate_b_fwd, b_gate_b_bwd], axis=-1)[:, None, :]
    dqg = jnp.pad(d_qnorm_g, ((0, 0), (0, 256 - D_Q_RANK)))[:, None, :]
    uq = d_w_uq.reshape(depth, D_Q_RANK, 4, 96)
    wuq = jnp.concatenate([uq[..., :64].reshape(depth, D_Q_RANK, 256), uq[..., 64:].reshape(depth, D_Q_RANK, 128)], axis=-1)
    wuq = jnp.pad(wuq, ((0, 0), (0, 256 - D_Q_RANK), (0, 0))).astype(BF16)
    ukv = d_w_ukv.reshape(depth, 128, 4, 128)
    eye4 = jnp.eye(4, dtype=F32)
    wuk = (ukv[..., :64].transpose(0, 2, 3, 1)[:, :, :, None, :] * eye4[None, :, None, :, None])
    wuk = wuk.reshape(depth, 256, 512).astype(BF16)
    wuv = (ukv[..., 64:].transpose(0, 2, 1, 3)[:, :, :, None, :] * eye4[None, :, None, :, None])
    wuv = wuv.reshape(depth, 512, 256).astype(BF16)
    wb0 = w_branch[:, 0].reshape(depth, 2, 2, 64, D_MODEL).transpose(0, 2, 1, 3, 4).reshape(depth, 256, D_MODEL)
    wb = jnp.concatenate([wb0[:, None], w_branch[:, 1:]], axis=1).astype(BF16)
    wout = w_out.astype(BF16)
    wu, wg, wd = w_ffu.astype(BF16), w_ffg.astype(BF16), w_ffd.astype(BF16)
    cw = jnp.pad(conv_w, ((0, 0), (0, 8 - conv_w.shape[1]), (0, 0)))
    aqg = jnp.tile(a_qnorm_g, (1, 4))[:, None, :]
    akg = jnp.tile(a_knorm_g, (1, 2))[:, None, :]
    bong = jnp.tile(b_onorm_g, (1, 4))[:, None, :]
    cong = jnp.tile(c_onorm_g, (1, 4))[:, None, :]
    lamv = jnp.stack([c_lq1, c_lk1, c_lq2, c_lk2], axis=1)
    caches = (cache_a_k.reshape(nd, depth, PAST_LEN, 128), cache_a_v.reshape(nd, depth, PAST_LEN, 128),
              cache_c_k.reshape(nd, depth, PAST_LEN, 256), cache_c_v.reshape(nd, depth, PAST_LEN, 256),
              cache_d_ckv, cache_d_krope)
    expand = lambda s: (s.transpose(0, 1, 2, 4, 3)[:, :, :, :, None, :]
                        * eye4[None, None, :, None, :, None]).reshape(nd, depth, 256, 128)
    s0f_all, s0b_all = expand(state_b_fwd), expand(state_b_bwd)

    n1g, n2g = norm1_g[:, None, :], norm2_g[:, None, :]
    small = (aqg, akg, bd, wgate, bgate, dqg, wuq, wuk, d_kvnorm_g[:, None, :], tile)
    cb = conv_b[:, None, :]
    new_caches = None
    sfs, sbs = [], []
    x = (x_prompt, x_sample)
    for l in range(depth):
        lam_init = 0.8 - 0.6 * math.exp(-0.3 * l)
        fa, kva, qa, ha, new_caches = _inproj_calls(x, n, mod4, n1g, w_aq, w_rest, rope_tab, small, new_caches, l,
                                                    depth, n_ctx)
        oacd = _attn_calls(kva, qa, caches, lamv, cong, wuv, bd, tile, l, lam_init, n_ctx)
        ob, sf, sb = _gla_calls(fa, s0f_all, s0b_all, bong, bd, tri, l, n_ctx)
        x1 = _merge_call(x, n, mod4, ha, w_gates, oacd, ob, wb, wout, l, n_ctx)
        res = _ffn_call(x1, mod4, n2g, wu, wg, cw, cb, wd, final_g[None], l, n_ctx, l == depth - 1)
        x = res[0]
        sfs.append(sf)
        sbs.append(sb)

    diag = lambda s: jnp.stack([s[:, :, hh * 64:(hh + 1) * 64, hh * 32:(hh + 1) * 32] for hh in range(4)],
                               axis=2).transpose(0, 1, 2, 4, 3)
    y_prompt, y_sample = res
    nak, nav, nck, ncv, nckv, nkr = new_caches
    return (y_prompt, y_sample, nak.reshape(nb, depth, SEQ, 2, 64), nav.reshape(nb, depth, SEQ, 2, 64),
            diag(jnp.stack(sfs, axis=1)), diag(jnp.stack(sbs, axis=1)), nck.reshape(nb, depth, SEQ, 4, 2, 32),
            ncv.reshape(nb, depth, SEQ, 4, 64), nckv, nkr)
```

```python
import functools
import math

import numpy as np
import jax
import jax.numpy as jnp
from jax import lax
from jax.experimental import pallas as pl
from jax.experimental.pallas import tpu as pltpu

F32 = jnp.float32
BF16 = jnp.bfloat16

D_MODEL = 1024
SEQ = 256
DEC_SEQ = 1024
PAST_LEN = 512
GRID_W = 64
ROPE_BASE = 10000.0
EPS = 1e-6
LOG2E = math.log2(math.e)
HEAD_W = 64
BRANCH_WIDTH = 256
GLA_KEY_W = 32
GLA_CHUNK = 64
GLA_SUB = 16
GLA_BLOCK = 256
GLA_GATE_NORM = 16.0
D_Q_RANK = 192
D_FF = 2816
FF_CHUNK = 256
N_MOD_ROWS = 16

ROW_TILE = 512
FFN_TILE = 1024
Q_TILE = 256
CTX_SEQ_PER_STEP = 2
HALO = 16
VMEM_LIMIT = 56 * 1024 * 1024

IN_AQ_W = 256
IN_GATE_IN = (1280, 1312)
NZ = 2176
Z_AK, Z_AV, Z_BQ, Z_BK, Z_BV, Z_BR, Z_CQ, Z_CK, Z_CV, Z_DQ = 0, 128, 256, 384, 512, 768, 1024, 1280, 1536, 1792
GATES_OFF = 2432
F_W = 1024
F_BQ, F_BK, F_BV, F_BR, F_LGF, F_LGB = 0, 128, 256, 512, 768, 896
KV_W = 1024
KV_AK, KV_AV, KV_CK, KV_CV, KV_DKV, KV_KR = 0, 128, 256, 512, 768, 896
Q_W = 1152
Q_AQ, Q_CQ, Q_DLAT, Q_DQR = 0, 256, 512, 1024
ROPE_W = 768


def _dot(a, b):
    return jnp.dot(a, b, preferred_element_type=F32)


def _dot_nt(a, b):
    return lax.dot_general(a, b, (((1,), (1,)), ((), ())), preferred_element_type=F32)


def _dot_tn(a, b):
    return lax.dot_general(a, b, (((0,), (0,)), ((), ())), preferred_element_type=F32)


def _split3(a):
    hi = a.astype(BF16)
    r1 = a - hi.astype(F32)
    mid = r1.astype(BF16)
    lo = (r1 - mid.astype(F32)).astype(BF16)
    return hi, mid, lo


def _split_dot(a, b01):
    hi, mid, lo = _split3(a)
    return _dot(hi, b01) + _dot(mid, b01) + _dot(lo, b01)


def _split_dot_tn(b01, a):
    hi, mid, lo = _split3(a)
    return _dot(b01, hi) + _dot(b01, mid) + _dot(b01, lo)


def _group_rms(v, bd, width, gain):
    ss = _split_dot(v * v, bd)
    return v * lax.rsqrt(ss * (1.0 / width) + EPS) * gain


def _rope128(v, cos, sin_a, sin_b, shift):
    return v * cos + pltpu.roll(v, 128 - shift, 1) * sin_a + pltpu.roll(v, shift, 1) * sin_b


def _modulated_norm(x, gain, shift, scale):
    ms = jnp.mean(x * x, axis=-1, keepdims=True)
    return (x * lax.rsqrt(ms + EPS) * gain) * (1.0 + scale) + shift


def _lane_mask(width, lo, hi):
    lane = lax.broadcasted_iota(jnp.int32, (1, width), 1)
    return (lane >= lo) & (lane < hi)


def _const_spec(shape, grid_rank):
    zeros = tuple(0 for _ in shape)
    if grid_rank == 1:
        return pl.BlockSpec(shape, lambda i: zeros)
    return pl.BlockSpec(shape, lambda b, i: zeros)


def _layer_spec(shape, layer, grid_rank):
    idx = (layer,) + tuple(0 for _ in shape)
    once = pl.Buffered(1)
    if grid_rank == 1:
        return pl.BlockSpec((None,) + tuple(shape), lambda i: idx, pipeline_mode=once)
    return pl.BlockSpec((None,) + tuple(shape), lambda b, i: idx, pipeline_mode=once)


def _mod_kernel(cv_ref, w_ref, b_ref, o_ref):
    cv = cv_ref[...]
    a = cv * jax.nn.sigmoid(cv)
    o_ref[...] = _dot(a.astype(BF16), w_ref[...].astype(BF16)) + b_ref[...]


def _mod_call(cvec, w_mod, b_mod):
    depth = w_mod.shape[0]
    nblk = 4
    bw = 6 * D_MODEL // nblk
    return pl.pallas_call(
        _mod_kernel,
        grid=(depth, nblk),
        in_specs=[
            pl.BlockSpec((N_MOD_ROWS, D_MODEL), lambda l, j: (0, 0)),
            pl.BlockSpec((None, D_MODEL, bw), lambda l, j: (l, 0, j)),
            pl.BlockSpec((None, 1, bw), lambda l, j: (l, 0, j)),
        ],
        out_specs=pl.BlockSpec((None, N_MOD_ROWS, bw), lambda l, j: (l, 0, j)),
        out_shape=jax.ShapeDtypeStruct((depth, N_MOD_ROWS, 6 * D_MODEL), F32),
        compiler_params=pltpu.CompilerParams(vmem_limit_bytes=VMEM_LIMIT),
        name="mod_vectors",
    )(cvec, w_mod, b_mod.reshape(depth, 1, 6 * D_MODEL))


def _inproj_body(x, mod_ref, n1g_ref, waq_ref, w_ref, rope_ref, aqg_ref, akg_ref, bd_ref, wgate_ref, bgate_ref,
                 dqg_ref, wuq_ref, wuk_ref, dkvg_ref, tile_ref, f_ref, kv_ref, q_ref, h_ref, cache_refs):
    mod = mod_ref[...]
    h = _modulated_norm(x, n1g_ref[...], mod[:, 0:D_MODEL], mod[:, D_MODEL:2 * D_MODEL]).astype(BF16)
    h_ref[...] = h
    z_aq = _dot(h, waq_ref[...])
    z = _dot(h, w_ref[...])
    zd = pltpu.roll(z[:, NZ - 256:NZ], 192, 1)
    bd = bd_ref[...]
    bd128 = bd[0:128, 0:128]
    if rope_ref is None:
        rope64 = rope32 = lambda v: v
    else:
        rope = rope_ref[...]
        cos64, sa64, sb64 = rope[:, 0:128], rope[:, 128:256], rope[:, 256:384]
        cos32, sa32, sb32 = rope[:, 384:512], rope[:, 512:640], rope[:, 640:768]
        rope64 = lambda v: _rope128(v, cos64, sa64, sb64, 16)
        rope32 = lambda v: _rope128(v, cos32, sa32, sb32, 8)

    def emit_kv(col, cache_idx, val, kv_val=None):
        kv_val = val if kv_val is None else kv_val
        kv_ref[:, col:col + kv_val.shape[1]] = kv_val.astype(BF16)
        if cache_refs is not None:
            ref = cache_refs[cache_idx]
            for j in range(val.shape[0] // SEQ):
                ref[j] = val[j * SEQ:(j + 1) * SEQ, 0:ref.shape[-1]]

    aq = _group_rms(z_aq, bd, HEAD_W, aqg_ref[...])
    a_scale = HEAD_W ** -0.5 * LOG2E
    q_ref[:, Q_AQ:Q_AQ + 128] = (rope64(aq[:, 0:128]) * a_scale).astype(BF16)
    q_ref[:, Q_AQ + 128:Q_AQ + 256] = (rope64(aq[:, 128:256]) * a_scale).astype(BF16)
    emit_kv(KV_AK, 0, rope64(_group_rms(z[:, Z_AK:Z_AK + 128], bd128, HEAD_W, akg_ref[...])))
    emit_kv(KV_AV, 1, z[:, Z_AV:Z_AV + 128])

    f_ref[:, F_BQ:F_BK] = z[:, Z_BQ:Z_BK] * (GLA_KEY_W ** -0.5)
    f_ref[:, F_BK:F_BV] = z[:, Z_BK:Z_BV]
    f_ref[:, F_BV:F_BR] = z[:, Z_BV:Z_BR]
    f_ref[:, F_BR:F_LGF] = z[:, Z_BR:Z_CQ]
    pre = _dot(zd[:, 128:256].astype(BF16), wgate_ref[...]) + bgate_ref[...]
    logsig = jnp.minimum(pre, 0.0) - jnp.log1p(jnp.exp(-jnp.abs(pre)))
    f_ref[:, F_LGF:F_W] = logsig / GLA_GATE_NORM

    c_scale = 32 ** -0.5 * LOG2E
    q_ref[:, Q_CQ:Q_CQ + 128] = (rope32(z[:, Z_CQ:Z_CQ + 128]) * c_scale).astype(BF16)
    q_ref[:, Q_CQ + 128:Q_CQ + 256] = (rope32(z[:, Z_CQ + 128:Z_CQ + 256]) * c_scale).astype(BF16)
    emit_kv(KV_CK, 2, jnp.concatenate([rope32(z[:, Z_CK:Z_CK + 128]), rope32(z[:, Z_CK + 128:Z_CK + 256])],
                                      axis=-1))
    emit_kv(KV_CV, 3, z[:, Z_CV:Z_DQ])

    dkv = zd[:, 0:128]
    ms = jnp.mean(dkv * dkv, axis=-1, keepdims=True)
    emit_kv(KV_DKV, 4, dkv * lax.rsqrt(ms + EPS) * dkvg_ref[...])
    kr = rope32(zd[:, 128:256])
    emit_kv(KV_KR, 5, kr, _dot(kr[:, 0:32].astype(BF16), tile_ref[...]))
    zq = jnp.where(_lane_mask(256, 0, D_Q_RANK), z[:, Z_DQ:Z_DQ + 256], 0.0)
    msq = jnp.sum(zq * zq, axis=-1, keepdims=True) * (1.0 / D_Q_RANK)
    qn = zq * lax.rsqrt(msq + EPS) * dqg_ref[...]
    dq = _dot(qn.astype(BF16), wuq_ref[...])
    d_scale = 96 ** -0.5 * LOG2E
    q_ref[:, Q_DLAT:Q_DQR] = (_dot(dq[:, 0:256].astype(BF16), wuk_ref[...]) * d_scale).astype(BF16)
    q_ref[:, Q_DQR:Q_W] = (rope32(dq[:, 256:384]) * d_scale).astype(BF16)


def _inproj_ctx_kernel(x_ref, mod_ref, n1g_ref, waq_ref, w_ref, aqg_ref, akg_ref, bd_ref, wgate_ref, bgate_ref,
                       dqg_ref, wuq_ref, wuk_ref, dkvg_ref, tile_ref, *rest):
    outs = rest[-10:]
    x = x_ref[...].reshape(-1, D_MODEL)
    _inproj_body(x, mod_ref, n1g_ref, waq_ref, w_ref, None, aqg_ref, akg_ref, bd_ref, wgate_ref, bgate_ref,
                 dqg_ref, wuq_ref, wuk_ref, dkvg_ref, tile_ref, outs[0], outs[1], outs[2], outs[3], outs[4:])


def _inproj_lat_kernel(x_ref, mod_ref, n1g_ref, waq_ref, w_ref, rope_ref, aqg_ref, akg_ref, bd_ref, wgate_ref,
                       bgate_ref, dqg_ref, wuq_ref, wuk_ref, dkvg_ref, tile_ref, pf_ref, pkv_ref, pq_ref, ph_ref,
                       f_ref, kv_ref, q_ref, h_ref):
    del pf_ref, pkv_ref, pq_ref, ph_ref
    _inproj_body(x_ref[...], mod_ref, n1g_ref, waq_ref, w_ref, rope_ref, aqg_ref, akg_ref, bd_ref, wgate_ref,
                 bgate_ref, dqg_ref, wuq_ref, wuk_ref, dkvg_ref, tile_ref, f_ref, kv_ref, q_ref, h_ref, None)


CACHE_WIDTHS = (128, 128, 256, 256, 128, 32)


def _inproj_calls(x, n, mod4, n1g, w_aq, w_rest, rope_tab, small, prev_caches, layer, depth, n_ctx_rows):
    tm = ROW_TILE
    ctx_tiles = n_ctx_rows // tm
    seq_tiles = DEC_SEQ // tm
    spt = tm // SEQ
    n_ctx_seq = n_ctx_rows // SEQ
    if isinstance(x, tuple):
        x_ctx, x_lat = x
        x_ctx_spec = pl.BlockSpec((spt, SEQ, D_MODEL), lambda i: (i, 0, 0))
        x_lat_spec = pl.BlockSpec((None, tm, D_MODEL), lambda i: (i // seq_tiles, i % seq_tiles, 0))
    else:
        x_ctx = x_lat = x
        x_ctx_spec = pl.BlockSpec((tm, D_MODEL), lambda i: (i, 0))
        x_lat_spec = pl.BlockSpec((tm, D_MODEL), lambda i: (i + ctx_tiles, 0))
    const = lambda shape: _layer_spec(shape, layer, 1)
    small_specs = [const((1, 256)), const((1, 128)), _const_spec((256, 256), 1), const((128, 256)),
                   const((1, 256)), const((1, 256)), const((256, 384)), const((256, 512)), const((1, 128)),
                   _const_spec((32, 128), 1)]
    act_shapes = [jax.ShapeDtypeStruct((n, F_W), F32), jax.ShapeDtypeStruct((n, KV_W), BF16),
                  jax.ShapeDtypeStruct((n, Q_W), BF16), jax.ShapeDtypeStruct((n, D_MODEL), BF16)]
    act_specs = lambda off: [pl.BlockSpec((tm, w), lambda i: (i + off, 0)) for w in (F_W, KV_W, Q_W, D_MODEL)]
    n_act = len(act_shapes)
    cparams = pltpu.CompilerParams(vmem_limit_bytes=VMEM_LIMIT)
    any_spec = pl.BlockSpec(memory_space=pl.ANY)

    n_in = 5 + len(small)
    alias = {} if prev_caches is None else {n_in + k: n_act + k for k in range(6)}
    res = pl.pallas_call(
        _inproj_ctx_kernel,
        grid=(ctx_tiles,),
        in_specs=[
            x_ctx_spec,
            pl.BlockSpec((None, None, 1, 6 * D_MODEL), lambda i: (layer, 0, 0, 0)),
            const((1, D_MODEL)),
            const((D_MODEL, IN_AQ_W)),
            const((D_MODEL, NZ)),
        ] + small_specs + ([] if prev_caches is None else [any_spec] * 6),
        out_specs=act_specs(0) + [pl.BlockSpec((spt, None, SEQ, w), lambda i: (i, layer, 0, 0))
                                  for w in CACHE_WIDTHS],
        out_shape=act_shapes + [jax.ShapeDtypeStruct((n_ctx_seq, depth, SEQ, w), F32) for w in CACHE_WIDTHS],
        input_output_aliases=alias,
        compiler_params=cparams,
        name="in_projection_context",
    )(x_ctx, mod4, n1g, w_aq, w_rest, *small, *([] if prev_caches is None else prev_caches))
    acts_c, caches = res[:n_act], tuple(res[n_act:])

    fa, kva, qa, ha = pl.pallas_call(
        _inproj_lat_kernel,
        grid=((n - n_ctx_rows) // tm,),
        in_specs=[
            x_lat_spec,
            pl.BlockSpec((None, None, 1, 6 * D_MODEL), lambda i: (layer, 1 + i // seq_tiles, 0, 0)),
            const((1, D_MODEL)),
            const((D_MODEL, IN_AQ_W)),
            const((D_MODEL, NZ)),
            pl.BlockSpec((tm, ROPE_W), lambda i: (i % seq_tiles, 0)),
        ] + small_specs + [any_spec] * n_act,
        out_specs=act_specs(ctx_tiles),
        out_shape=act_shapes,
        input_output_aliases={n_in + 1 + k: k for k in range(n_act)},
        compiler_params=cparams,
        name="in_projection_latent",
    )(x_lat, mod4, n1g, w_aq, w_rest, rope_tab, *small, *acts_c)
    return fa, kva, qa, ha, caches


def _exp2_parts(parts):
    m = parts[0].max(axis=-1, keepdims=True)
    for s in parts[1:]:
        m = jnp.maximum(m, s.max(axis=-1, keepdims=True))
    return [jnp.exp2(s - m) for s in parts]


def _value_variants(av, cv, dk):
    one = jnp.ones((), BF16)
    av1 = [jnp.where(_lane_mask(128, hh * HEAD_W, (hh + 1) * HEAD_W), av, one) for hh in range(2)]
    cv1 = [jnp.where(_lane_mask(256, hh * HEAD_W, (hh + 1) * HEAD_W), cv, one) for hh in range(4)]
    vlat = jnp.where(_lane_mask(256, 0, 128), dk, one)
    return av1, cv1, vlat


def _run(phases):
    for _ in phases:
        pass


def _interleave(*phase_generators):
    live = list(phase_generators)
    while live:
        live = [g for g in live if next(g, StopIteration) is not StopIteration]


def _attn_core(*args):
    _run(_attn_phases(*args))


def _attn_phases(q, ak, av1, ck, cv1, dk, vlat, lam_ref, cong_ref, wuv_ref, bd_ref, o_ref, lam_init):
    zero = jnp.zeros((), BF16)
    ak, ck, dk = [ak], [ck], [dk]
    tq = q.shape[0]

    masks_a = [_lane_mask(128, hh * HEAD_W, (hh + 1) * HEAD_W) for hh in range(2)]
    qa = jnp.concatenate([jnp.where(masks_a[hh], q[:, Q_AQ + g * 128:Q_AQ + (g + 1) * 128], zero)
                          for hh in range(2) for g in range(2)], axis=0)
    qcs = [jnp.concatenate([jnp.where(_lane_mask(128, u * 32, (u + 1) * 32),
                                      q[:, Q_CQ + c * 128:Q_CQ + (c + 1) * 128], zero) for u in range(4)], axis=0)
           for c in range(2)]
    qd = jnp.concatenate(
        [jnp.concatenate([q[:, Q_DLAT + hh * 128:Q_DLAT + (hh + 1) * 128],
                          jnp.where(_lane_mask(128, hh * 32, (hh + 1) * 32), q[:, Q_DQR:Q_W], zero)], axis=-1)
         for hh in range(4)], axis=0)
    s_a = [_dot_nt(qa, k) for k in ak]
    s_c = [[_dot_nt(qcs[c], k[:, c * 128:(c + 1) * 128]) for k in ck] for c in range(2)]
    s_d = [_dot_nt(qd, k) for k in dk]
    yield

    ps = [p.astype(BF16) for p in _exp2_parts(s_a)]
    oa = [None, None]
    for hh in range(2):
        rows = slice(hh * 2 * tq, (hh + 1) * 2 * tq)
        o = _dot(ps[0][rows], av1[hh])
        o = jnp.where(masks_a[hh], o / pltpu.roll(o, HEAD_W, 1), 0.0)
        for g in range(2):
            part = o[g * tq:(g + 1) * tq]
            oa[g] = part if oa[g] is None else oa[g] + part
    o_ref[:, 0:128] = oa[0]
    o_ref[:, 128:256] = oa[1]
    yield

    lv = lam_ref[...]
    lam = (jnp.exp(jnp.sum(lv[0:1] * lv[1:2], axis=-1, keepdims=True))
           - jnp.exp(jnp.sum(lv[2:3] * lv[3:4], axis=-1, keepdims=True)) + lam_init)
    oc = []
    for c in range(2):
        c0 = c * 128
        ps = [p.astype(BF16) for p in _exp2_parts(s_c[c])]
        och = None
        for h2 in range(2):
            hh = c * 2 + h2
            rows = slice(h2 * 2 * tq, (h2 + 1) * 2 * tq)
            o = _dot(ps[0][rows], cv1[hh])
            res = o[:, c0:c0 + 128] / o[:, 128 - c0:256 - c0]
            o = jnp.where(_lane_mask(128, h2 * HEAD_W, (h2 + 1) * HEAD_W), res[0:tq] - lam * res[tq:2 * tq], 0.0)
            och = o if och is None else och + o
        oc.append(och)
        yield
    oc = jnp.concatenate(oc, axis=-1)
    o_ref[:, 256:512] = _group_rms(oc, bd_ref[...], HEAD_W, cong_ref[...]) * (1.0 - lam_init)

    ps = [p.astype(BF16) for p in _exp2_parts(s_d)]
    o = _dot(ps[0], vlat)
    lat = (o[:, 0:128] / o[:, 128:256]).astype(BF16)
    o_ref[:, 512:768] = _dot(jnp.concatenate([lat[hh * tq:(hh + 1) * tq] for hh in range(4)], axis=-1),
                             wuv_ref[...])


def _attn_lat_kernel(q_ref, kva_ref, ck_ref, cv_ref, dkv_ref, cak_ref, cav_ref, cck_ref, ccv_ref,
                     cckv_ref, ckr_ref, lam_ref, cong_ref, wuv_ref, bd_ref, tile_ref, prev_ref, o_ref,
                     ak_s, ck_s, dk_s, av1_s, cv1_s, vlat_s, *, lam_init):
    del prev_ref

    @pl.when(pl.program_id(1) == 0)
    def _():
        p = PAST_LEN
        kva = kva_ref[...]
        kr4 = _dot(ckr_ref[...].astype(BF16), tile_ref[...]).astype(BF16)
        pieces = ((cak_ref[...].astype(BF16), cav_ref[...].astype(BF16), cck_ref[...].astype(BF16),
                   ccv_ref[...].astype(BF16), jnp.concatenate([cckv_ref[...].astype(BF16), kr4], axis=-1)),
                  (kva[:, 0:128], kva[:, 128:256], ck_ref[...], cv_ref[...], dkv_ref[...]))
        for rows, (ak, av, ck, cv, dk) in zip((slice(0, p), slice(p, p + DEC_SEQ)), pieces):
            av1, cv1, vlat = _value_variants(av, cv, dk)
            ak_s[rows] = ak
            ck_s[rows] = ck
            dk_s[rows] = dk
            vlat_s[rows] = vlat
            for hh in range(2):
                av1_s[hh, rows] = av1[hh]
            for hh in range(4):
                cv1_s[hh, rows] = cv1[hh]

    _attn_core(q_ref[...], ak_s[...], [av1_s[hh] for hh in range(2)], ck_s[...],
               [cv1_s[hh] for hh in range(4)], dk_s[...], vlat_s[...], lam_ref, cong_ref, wuv_ref, bd_ref,
               o_ref, lam_init)


def _attn_lat_call(kva, qa, caches, lamv, cong, wuv, bd, tile, o_ctx, layer, lam_init, n_ctx_rows):
    n = kva.shape[0]
    n_lat_seq = (n - n_ctx_rows) // DEC_SEQ
    param_specs = lambda rank: [_layer_spec((4, 32), layer, rank), _layer_spec((1, 256), layer, rank),
                                _layer_spec((512, 256), layer, rank), _const_spec((256, 256), rank),
                                _const_spec((32, 128), rank)]
    out_shape = jax.ShapeDtypeStruct((n, 3 * BRANCH_WIDTH), F32)
    cparams = pltpu.CompilerParams(vmem_limit_bytes=VMEM_LIMIT)

    qt = Q_TILE
    q_per_seq = DEC_SEQ // qt
    q0 = n_ctx_rows // qt
    s0 = n_ctx_rows // DEC_SEQ
    cak, cav, cck, ccv, cckv, ckr = caches
    cache_spec = lambda w: pl.BlockSpec((None, None, PAST_LEN, w), lambda b, i: (b, layer, 0, 0))
    n_keys = PAST_LEN + DEC_SEQ
    o_all = pl.pallas_call(
        functools.partial(_attn_lat_kernel, lam_init=lam_init),
        grid=(n_lat_seq, q_per_seq),
        in_specs=[
            pl.BlockSpec((qt, Q_W), lambda b, i: (q0 + b * q_per_seq + i, 0)),
            pl.BlockSpec((DEC_SEQ, 256), lambda b, i: (s0 + b, KV_AK // 256)),
            pl.BlockSpec((DEC_SEQ, 256), lambda b, i: (s0 + b, KV_CK // 256)),
            pl.BlockSpec((DEC_SEQ, 256), lambda b, i: (s0 + b, KV_CV // 256)),
            pl.BlockSpec((DEC_SEQ, 256), lambda b, i: (s0 + b, KV_DKV // 256)),
            cache_spec(128), cache_spec(128), cache_spec(256), cache_spec(256), cache_spec(128),
            cache_spec(32),
        ] + param_specs(2) + [pl.BlockSpec(memory_space=pl.ANY)],
        out_specs=pl.BlockSpec((qt, 3 * BRANCH_WIDTH), lambda b, i: (q0 + b * q_per_seq + i, 0)),
        out_shape=out_shape,
        scratch_shapes=[pltpu.VMEM((n_keys, 128), BF16), pltpu.VMEM((n_keys, 256), BF16),
                        pltpu.VMEM((n_keys, 256), BF16), pltpu.VMEM((2, n_keys, 128), BF16),
                        pltpu.VMEM((4, n_keys, 256), BF16), pltpu.VMEM((n_keys, 256), BF16)],
        input_output_aliases={16: 0},
        compiler_params=cparams,
        name="attention_latent",
    )(qa, kva, kva, kva, kva, cak, cav, cck, ccv, cckv, ckr, lamv, cong, wuv, bd, tile, o_ctx)
    return o_all


def _gla_chunk(q, k, v, bc, st, reverse, masks, result):
    c, sub, nsub = GLA_CHUNK, GLA_SUB, GLA_CHUNK // GLA_SUB
    keep, qmask, src_row, blk = masks
    edge = bc[0:1] if reverse else bc[c - 1:c]
    qs, kd = [], []
    for i in range(nsub):
        r0 = i * sub
        ref = bc[r0 + sub - 1:r0 + sub] if reverse else bc[r0:r0 + 1]
        qd = q[r0:r0 + sub] * jnp.exp(bc[r0:r0 + sub] - ref)
        qs.append(jnp.where(qmask, jnp.concatenate([qd] * 4, axis=0), 0.0))
        in_range = (src_row >= r0) if reverse else (src_row < r0 + sub)
        kd.append(k * jnp.exp(jnp.where(in_range, ref - bc, 0.0)))
    att = _dot_nt(jnp.concatenate(qs, axis=0).astype(BF16), jnp.concatenate(kd, axis=0).astype(BF16))
    vb = v.astype(BF16)
    q_dec = q * jnp.exp(bc)
    k_dec = k * jnp.exp(edge - bc)
    o_inter = _dot_nt(q_dec.astype(BF16), st.astype(BF16))
    upd = _dot_tn(vb, k_dec.astype(BF16))
    yield
    att = jnp.where(keep, att, 0.0).astype(BF16)
    pv = _dot(att, jnp.concatenate([vb] * nsub, axis=0))
    st_new = st * jnp.exp(edge) + jnp.where(blk, upd, 0.0)
    yield
    out_blocks = []
    for i in range(nsub):
        o = None
        for hh in range(4):
            r0 = i * c + hh * sub
            part = jnp.where(_lane_mask(256, hh * HEAD_W, (hh + 1) * HEAD_W), pv[r0:r0 + sub], 0.0)
            o = part if o is None else o + part
        out_blocks.append(o)
    o_intra = jnp.concatenate(out_blocks, axis=0)
    result.extend([o_intra + o_inter, st_new])


def _gla_masks(reverse):
    c, sub = GLA_CHUNK, GLA_SUB
    row = lax.broadcasted_iota(jnp.int32, (4 * c, 4 * c), 0)
    col = lax.broadcasted_iota(jnp.int32, (4 * c, 4 * c), 1)
    t = (row // c) * sub + row % sub
    s = col % c
    keep = (row // c == col // c) & ((s >= t) if reverse else (s <= t))
    qmask = (lax.broadcasted_iota(jnp.int32, (4 * sub, 128), 0) // sub
             == lax.broadcasted_iota(jnp.int32, (4 * sub, 128), 1) // GLA_KEY_W)
    src_row = lax.broadcasted_iota(jnp.int32, (c, 1), 0)
    blk = (lax.broadcasted_iota(jnp.int32, (256, 128), 0) // HEAD_W
           == lax.broadcasted_iota(jnp.int32, (256, 128), 1) // GLA_KEY_W)
    return keep, qmask, src_row, blk


def _gla_body(qk_ref, v_ref, r_ref, lg_ref, s0f, s0b, g_ref, bd_ref, tri_ref, o_ref, accf_ref, accb_ref,
              bcf_ref, bcb_ref, seq_len, unroll):
    nc = seq_len // GLA_CHUNK
    for r0 in range(0, seq_len, GLA_BLOCK):
        lg = lg_ref[r0:r0 + GLA_BLOCK, :]
        for d, bc_ref in ((0, bcf_ref), (1, bcb_ref)):
            res = _dot(tri_ref[d], jnp.concatenate(_split3(lg[:, d * 128:(d + 1) * 128]), axis=-1))
            bc_ref[r0:r0 + GLA_BLOCK, :] = res[:, 0:128] + res[:, 128:256] + res[:, 256:384]
    masks_f = _gla_masks(False)
    masks_b = _gla_masks(True)

    def chunk_rows(c):
        start = c * GLA_CHUNK
        return pl.ds(start if isinstance(c, int) else pl.multiple_of(start, GLA_CHUNK), GLA_CHUNK)

    def step(ci, carry):
        sf, sb = carry
        rf, rb = chunk_rows(ci), chunk_rows(nc - 1 - ci)
        res_f, res_b = [], []
        _interleave(
            _gla_chunk(qk_ref[rf, 0:128], qk_ref[rf, 128:256], v_ref[rf, :], bcf_ref[rf, :], sf, False, masks_f,
                       res_f),
            _gla_chunk(qk_ref[rb, 0:128], qk_ref[rb, 128:256], v_ref[rb, :], bcb_ref[rb, :], sb, True, masks_b,
                       res_b))
        accf_ref[rf, :] = res_f[0]
        accb_ref[rb, :] = res_b[0]
        return res_f[1], res_b[1]

    def finish():
        r = r_ref[...]
        o_ref[...] = (_group_rms(accf_ref[...] + accb_ref[...], bd_ref[...], HEAD_W, g_ref[...])
                      * (r * jax.nn.sigmoid(r)))

    if unroll is None:
        def phases(states):
            yield
            carry = (s0f, s0b)
            for ci in range(nc):
                carry = step(ci, carry)
                yield
            finish()
            states.extend(carry)
        return phases
    sf, sb = lax.fori_loop(0, nc, step, (s0f, s0b), unroll=unroll)
    finish()
    return sf, sb


def _ctx_mixers_kernel(q_ref, kva_ref, ck_ref, cv_ref, dkv_ref, qk_ref, v_ref, r_ref, lg_ref, lam_ref, cong_ref,
                       wuv_ref, bd_ref, bong_ref, tri_ref, o_ref, ob_ref, sf_ref, sb_ref, *scratch, lam_init):
    zero = jnp.zeros((256, 128), F32)
    nseq = q_ref.shape[0] // SEQ
    gens, states = [], [[] for _ in range(nseq)]
    for j in range(nseq):
        rows = slice(j * SEQ, (j + 1) * SEQ)
        kva = kva_ref[rows, :]
        dk = dkv_ref[rows, :]
        av1, cv1, vlat = _value_variants(kva[:, 128:256], cv_ref[rows, :], dk)
        gens.append(_attn_phases(q_ref[rows, :], kva[:, 0:128], av1, ck_ref[rows, :], cv1, dk, vlat, lam_ref,
                                 cong_ref, wuv_ref, bd_ref, o_ref.at[rows, :], lam_init))
        gens.append(_gla_body(qk_ref.at[rows, :], v_ref.at[rows, :], r_ref.at[rows, :], lg_ref.at[rows, :], zero,
                              zero, bong_ref, bd_ref, tri_ref, ob_ref.at[rows, :],
                              *[s.at[rows, :] for s in scratch], SEQ, None)(states[j]))
    _interleave(*gens)
    for j in range(nseq):
        sf_ref[j], sb_ref[j] = states[j]


def _gla_lat_kernel(qk_ref, v_ref, r_ref, lg_ref, s0f_ref, s0b_ref, g_ref, bd_ref, tri_ref, prev_ref,
                    o_ref, *scratch):
    del prev_ref
    _gla_body(qk_ref, v_ref, r_ref, lg_ref, s0f_ref[...], s0b_ref[...], g_ref, bd_ref, tri_ref, o_ref,
              *scratch, DEC_SEQ, 2)


def _ctx_mixers_call(fa, kva, qa, lamv, cong, wuv, bd, bong, tri, layer, lam_init, n_ctx_rows):
    n = fa.shape[0]
    n_ctx_seq = n_ctx_rows // SEQ
    nseq = CTX_SEQ_PER_STEP
    crows = nseq * SEQ
    const = lambda shape: _const_spec(shape, 1)
    lspec = lambda shape: _layer_spec(shape, layer, 1)
    col = lambda w, off: pl.BlockSpec((crows, w), lambda b: (b, off // w))
    st_shape = jax.ShapeDtypeStruct((n_ctx_seq, 256, 128), F32)
    st_spec = pl.BlockSpec((nseq, 256, 128), lambda b: (b, 0, 0))
    return pl.pallas_call(
        functools.partial(_ctx_mixers_kernel, lam_init=lam_init),
        grid=(n_ctx_seq // nseq,),
        in_specs=[
            col(Q_W, 0), col(256, KV_AK), col(256, KV_CK), col(256, KV_CV), col(256, KV_DKV),
            col(256, F_BQ), col(256, F_BV), col(256, F_BR), col(256, F_LGF),
            lspec((4, 32)), lspec((1, 256)), lspec((512, 256)), const((256, 256)), lspec((1, 256)),
            const((2, GLA_BLOCK, GLA_BLOCK)),
        ],
        out_specs=[col(3 * BRANCH_WIDTH, 0), col(BRANCH_WIDTH, 0), st_spec, st_spec],
        out_shape=[jax.ShapeDtypeStruct((n, 3 * BRANCH_WIDTH), F32), jax.ShapeDtypeStruct((n, BRANCH_WIDTH), F32),
                   st_shape, st_shape],
        scratch_shapes=[pltpu.VMEM((crows, BRANCH_WIDTH), F32), pltpu.VMEM((crows, BRANCH_WIDTH), F32),
                        pltpu.VMEM((crows, 128), F32), pltpu.VMEM((crows, 128), F32)],
        compiler_params=pltpu.CompilerParams(vmem_limit_bytes=VMEM_LIMIT),
        name="mixers_context",
    )(qa, kva, kva, kva, kva, fa, fa, fa, fa, lamv, cong, wuv, bd, bong, tri)


def _gla_lat_call(fa, s0f_t, s0b_t, bong, bd, tri, o_ctx, layer, n_ctx_rows):
    n = fa.shape[0]
    n_lat_seq = (n - n_ctx_rows) // DEC_SEQ
    cparams = pltpu.CompilerParams(vmem_limit_bytes=VMEM_LIMIT)
    const = lambda shape: _const_spec(shape, 1)
    gain_spec = _layer_spec((1, 256), layer, 1)
    state_spec = pl.BlockSpec((None, None, 256, 128), lambda b: (b, layer, 0, 0))
    out_shape = jax.ShapeDtypeStruct((n, BRANCH_WIDTH), F32)

    s0 = n_ctx_rows // DEC_SEQ
    o_all = pl.pallas_call(
        _gla_lat_kernel,
        grid=(n_lat_seq,),
        in_specs=[
            pl.BlockSpec((DEC_SEQ, 256), lambda b: (s0 + b, F_BQ // 256)),
            pl.BlockSpec((DEC_SEQ, 256), lambda b: (s0 + b, F_BV // 256)),
            pl.BlockSpec((DEC_SEQ, 256), lambda b: (s0 + b, F_BR // 256)),
            pl.BlockSpec((DEC_SEQ, 256), lambda b: (s0 + b, F_LGF // 256)),
            state_spec, state_spec,
            gain_spec, const((256, 256)), const((2, GLA_BLOCK, GLA_BLOCK)),
            pl.BlockSpec(memory_space=pl.ANY),
        ],
        out_specs=pl.BlockSpec((DEC_SEQ, BRANCH_WIDTH), lambda b: (s0 + b, 0)),
        out_shape=out_shape,
        scratch_shapes=[pltpu.VMEM((DEC_SEQ, BRANCH_WIDTH), F32), pltpu.VMEM((DEC_SEQ, BRANCH_WIDTH), F32),
                        pltpu.VMEM((DEC_SEQ, 128), F32), pltpu.VMEM((DEC_SEQ, 128), F32)],
        input_output_aliases={9: 0},
        compiler_params=cparams,
        name="gla_latent",
    )(fa, fa, fa, fa, s0f_t, s0b_t, bong, bd, tri, o_ctx)
    return o_all


def _merge_kernel(*refs, ctx_tiles):
    mod_ref, h_ref, wg_ref, oacd_ref, ob_ref, wb_ref, wout_ref, o_ref = refs[-8:]
    if len(refs) == 9:
        x = refs[0][...]
    else:
        x = jnp.where(pl.program_id(0) < ctx_tiles, refs[0][...].reshape(-1, D_MODEL), refs[1][...])
    mod = mod_ref[...]
    h = h_ref[...]
    oacd = oacd_ref[...]
    branches = [oacd[:, 0:256], ob_ref[...], oacd[:, 256:512], oacd[:, 512:768]]
    acc = None
    for j in range(4):
        gate = jax.nn.sigmoid(_dot(h, wg_ref[:, j * D_MODEL:(j + 1) * D_MODEL]))
        term = gate * _dot(branches[j].astype(BF16), wb_ref[j])
        acc = term if acc is None else acc + term
    mix = _dot(acc.astype(BF16), wout_ref[...])
    o_ref[...] = x + mod[:, 2 * D_MODEL:3 * D_MODEL] * mix


def _mod_map(layer, ctx_tiles, seq_tiles):
    return lambda i: (layer, jnp.where(i < ctx_tiles, 0, 1 + (i - ctx_tiles) // seq_tiles), 0, 0)


def _merge_call(x, n, mod4, ha, wgates, oacd, ob, wb, wout, layer, n_ctx_rows):
    tm = ROW_TILE
    ctx_tiles = n_ctx_rows // tm
    seq_tiles = DEC_SEQ // tm
    const = lambda shape: _layer_spec(shape, layer, 1)
    if isinstance(x, tuple):
        xs = list(x)
        x_specs = [
            pl.BlockSpec((tm // SEQ, SEQ, D_MODEL), lambda i: (jnp.minimum(i, ctx_tiles - 1), 0, 0)),
            pl.BlockSpec((None, tm, D_MODEL), lambda i: (jnp.maximum(i - ctx_tiles, 0) // seq_tiles,
                                                         jnp.maximum(i - ctx_tiles, 0) % seq_tiles, 0)),
        ]
    else:
        xs = [x]
        x_specs = [pl.BlockSpec((tm, D_MODEL), lambda i: (i, 0))]
    return pl.pallas_call(
        functools.partial(_merge_kernel, ctx_tiles=ctx_tiles),
        grid=(n // tm,),
        in_specs=x_specs + [
            pl.BlockSpec((None, None, 1, 6 * D_MODEL), _mod_map(layer, ctx_tiles, seq_tiles)),
            pl.BlockSpec((tm, D_MODEL), lambda i: (i, 0)),
            const((D_MODEL, 4 * D_MODEL)),
            pl.BlockSpec((tm, 3 * BRANCH_WIDTH), lambda i: (i, 0)),
            pl.BlockSpec((tm, BRANCH_WIDTH), lambda i: (i, 0)),
            const((4, BRANCH_WIDTH, D_MODEL)),
            const((D_MODEL, D_MODEL)),
        ],
        out_specs=pl.BlockSpec((tm, D_MODEL), lambda i: (i, 0)),
        out_shape=jax.ShapeDtypeStruct((n, D_MODEL), F32),
        compiler_params=pltpu.CompilerParams(vmem_limit_bytes=VMEM_LIMIT),
        name="merge",
    )(*xs, mod4, ha, wgates, oacd, ob, wb, wout)


def _ffn_kernel(x_ref, xp_ref, xn_ref, mod_ref, n2g_ref, wu_ref, wg_ref, cw_ref, cb_ref, wd_ref, fg_ref,
                *rest, ctx_tiles, final):
    out_refs, (hext_ref, act_ref) = rest[:-2], rest[-2:]
    tm = x_ref.shape[0]
    i = pl.program_id(0)
    x = x_ref[...]
    mod = mod_ref[...]
    sh2, sc2, g2 = mod[:, 3 * D_MODEL:4 * D_MODEL], mod[:, 4 * D_MODEL:5 * D_MODEL], mod[:, 5 * D_MODEL:]
    n2g = n2g_ref[...]
    hext_ref[0:HALO] = _modulated_norm(xp_ref[...], n2g, sh2, sc2).astype(BF16)
    hext_ref[HALO:HALO + tm] = _modulated_norm(x, n2g, sh2, sc2).astype(BF16)
    hext_ref[HALO + tm:] = _modulated_norm(xn_ref[...], n2g, sh2, sc2).astype(BF16)
    seq_len = jnp.where(i < ctx_tiles, SEQ, DEC_SEQ)
    row = lax.broadcasted_iota(jnp.int32, (tm, 1), 0)
    pos = (row + i * tm) & (seq_len - 1)
    first = pos == 0
    last = pos == seq_len - 1
    for c0 in range(0, D_FF, FF_CHUNK):
        u_ext = _dot(hext_ref[...], wu_ref[:, c0:c0 + FF_CHUNK])
        u = u_ext[HALO:HALO + tm]
        u_dn = jnp.where(first, 0.0, u_ext[HALO - 1:HALO - 1 + tm])
        u_up = jnp.where(last, 0.0, u_ext[HALO + 1:HALO + 1 + tm])
        cw = cw_ref[:, c0:c0 + FF_CHUNK]
        uc = u_dn * cw[0:1] + u * cw[1:2] + u_up * cw[2:3] + cb_ref[:, c0:c0 + FF_CHUNK]
        g = _dot(hext_ref[HALO:HALO + tm], wg_ref[:, c0:c0 + FF_CHUNK])
        act_ref[:, c0:c0 + FF_CHUNK] = (jax.nn.gelu(uc) * g).astype(BF16)
    out = x + g2 * _dot(act_ref[...], wd_ref[...])
    if final:
        yp_ref, ys_ref = out_refs
        ms = jnp.mean(out * out, axis=-1, keepdims=True)
        y = out * lax.rsqrt(ms + EPS) * fg_ref[...]

        @pl.when(i < ctx_tiles)
        def _():
            for j in range(tm // SEQ):
                yp_ref[j] = y[j * SEQ:(j + 1) * SEQ]

        @pl.when(i >= ctx_tiles)
        def _():
            ys_ref[...] = y
    else:
        out_refs[0][...] = out


def _ffn_call(x, mod4, n2g, wu, wg, cw, cb, wd, fg, layer, n_ctx_rows, final):
    n = x.shape[0]
    tm = FFN_TILE
    ctx_tiles = n_ctx_rows // tm
    seq_tiles = DEC_SEQ // tm
    hb = tm // HALO
    last_halo = n // HALO - 1
    lspec = lambda shape: _layer_spec(shape, layer, 1)
    if final:
        spt = tm // SEQ
        out_specs = [
            pl.BlockSpec((spt, SEQ, D_MODEL), lambda i: (jnp.minimum(i, ctx_tiles - 1), 0, 0)),
            pl.BlockSpec((None, tm, D_MODEL), lambda i: (jnp.maximum(i - ctx_tiles, 0) // seq_tiles,
                                                         jnp.maximum(i - ctx_tiles, 0) % seq_tiles, 0)),
        ]
        out_shape = [jax.ShapeDtypeStruct((n_ctx_rows // SEQ, SEQ, D_MODEL), F32),
                     jax.ShapeDtypeStruct(((n - n_ctx_rows) // DEC_SEQ, DEC_SEQ, D_MODEL), F32)]
    else:
        out_specs = [pl.BlockSpec((tm, D_MODEL), lambda i: (i, 0))]
        out_shape = [jax.ShapeDtypeStruct((n, D_MODEL), F32)]
    return pl.pallas_call(
        functools.partial(_ffn_kernel, ctx_tiles=ctx_tiles, final=final),
        grid=(n // tm,),
        in_specs=[
            pl.BlockSpec((tm, D_MODEL), lambda i: (i, 0)),
            pl.BlockSpec((HALO, D_MODEL), lambda i: (jnp.maximum(i * hb - 1, 0), 0)),
            pl.BlockSpec((HALO, D_MODEL), lambda i: (jnp.minimum((i + 1) * hb, last_halo), 0)),
            pl.BlockSpec((None, None, 1, 6 * D_MODEL), _mod_map(layer, ctx_tiles, seq_tiles)),
            lspec((1, D_MODEL)),
            lspec((D_MODEL, D_FF)), lspec((D_MODEL, D_FF)), lspec((8, D_FF)), lspec((1, D_FF)),
            lspec((D_FF, D_MODEL)), _const_spec((1, D_MODEL), 1),
        ],
        out_specs=out_specs,
        out_shape=out_shape,
        scratch_shapes=[pltpu.VMEM((tm + 2 * HALO, D_MODEL), BF16), pltpu.VMEM((tm, D_FF), BF16)],
        compiler_params=pltpu.CompilerParams(vmem_limit_bytes=VMEM_LIMIT),
        name="conv_ffn",
    )(x, x, x, mod4, n2g, wu, wg, cw, cb, wd, fg)


def _rope_tables():
    t = np.arange(DEC_SEQ)
    row = (t // GRID_W).astype(np.float32)
    col = (t % GRID_W).astype(np.float32)

    def tables(head_dim):
        r = head_dim // 2
        half = r // 2
        inv = (np.float32(ROPE_BASE) ** (-np.arange(half, dtype=np.float32) / np.float32(half))).astype(np.float32)
        lane = np.arange(128) % head_dim
        use_col = lane >= r
        e = lane % r
        first_half = e < half
        freq = inv[e % half]
        pos = np.where(use_col[None, :], col[:, None], row[:, None]).astype(np.float32)
        ang = (pos * freq[None, :]).astype(np.float32)
        cos = np.cos(ang).astype(np.float32)
        sin = np.sin(ang).astype(np.float32)
        sin_a = np.where(first_half[None, :], -sin, 0.0).astype(np.float32)
        sin_b = np.where(first_half[None, :], 0.0, sin).astype(np.float32)
        return cos, sin_a, sin_b

    return jnp.asarray(np.concatenate(list(tables(64)) + list(tables(32)), axis=1))


def kernel(x_prompt, x_sample, c, cache_a_k, cache_a_v, state_b_fwd, state_b_bwd, cache_c_k, cache_c_v, cache_d_ckv, cache_d_krope, c_ctx, w_mod, b_mod, norm1_g, norm2_g, w_in, a_qnorm_g, a_knorm_g, b_gate_w_fwd, b_gate_b_fwd, b_gate_w_bwd, b_gate_b_bwd, b_onorm_g, c_lq1, c_lk1, c_lq2, c_lk2, c_onorm_g, d_qnorm_g, d_w_uq, d_kvnorm_g, d_w_ukv, w_branch, w_out, w_ffu, w_ffg, conv_w, conv_b, w_ffd, final_g):
    depth = w_in.shape[0]
    nb, nd = x_prompt.shape[0], x_sample.shape[0]
    n_ctx = nb * SEQ
    assert n_ctx % DEC_SEQ == 0 and nd + 1 <= N_MOD_ROWS

    n = n_ctx + nd * DEC_SEQ
    cvec = jnp.zeros((N_MOD_ROWS, D_MODEL), F32).at[0].set(c_ctx).at[1:1 + nd].set(c)
    mod4 = _mod_call(cvec, w_mod, b_mod).reshape(depth, N_MOD_ROWS, 1, 6 * D_MODEL)

    rope_tab = _rope_tables()
    lane = np.arange(256)
    bd = jnp.asarray(lane[:, None] // HEAD_W == lane[None, :] // HEAD_W, BF16)
    tile = jnp.asarray(np.arange(32)[:, None] == np.arange(128)[None, :] % 32, BF16)
    tt = np.arange(GLA_BLOCK)
    same_chunk = tt[:, None] // GLA_CHUNK == tt[None, :] // GLA_CHUNK
    tri = jnp.asarray(np.stack([same_chunk & (tt[:, None] >= tt[None, :]),
                                same_chunk & (tt[:, None] <= tt[None, :])]), BF16)

    w_aq = (w_in[:, :, 0:IN_AQ_W].reshape(depth, D_MODEL, 2, 2, HEAD_W).transpose(0, 1, 3, 2, 4)
            .reshape(depth, D_MODEL, IN_AQ_W).astype(BF16))
    g0, g1 = IN_GATE_IN
    w_rest = jnp.concatenate([w_in[:, :, IN_AQ_W:g0], w_in[:, :, g1:GATES_OFF], w_in[:, :, g0:g1]],
                             axis=-1).astype(BF16)
    w_gates = w_in[:, :, GATES_OFF:].astype(BF16)
    wgate = jnp.zeros((depth, 128, 256), F32)
    wgate = wgate.at[:, 32:48, 0:128].set(b_gate_w_fwd).at[:, 48:64, 128:256].set(b_gate_w_bwd).astype(BF16)
    bgate = jnp.concatenate([b_gate_b_fwd, b_gate_b_bwd], axis=-1)[:, None, :]
    dqg = jnp.pad(d_qnorm_g, ((0, 0), (0, 256 - D_Q_RANK)))[:, None, :]
    uq = d_w_uq.reshape(depth, D_Q_RANK, 4, 96)
    wuq = jnp.concatenate([uq[..., :64].reshape(depth, D_Q_RANK, 256), uq[..., 64:].reshape(depth, D_Q_RANK, 128)], axis=-1)
    wuq = jnp.pad(wuq, ((0, 0), (0, 256 - D_Q_RANK), (0, 0))).astype(BF16)
    ukv = d_w_ukv.reshape(depth, 128, 4, 128)
    eye4 = jnp.eye(4, dtype=F32)
    wuk = (ukv[..., :64].transpose(0, 2, 3, 1)[:, :, :, None, :] * eye4[None, :, None, :, None])
    wuk = wuk.reshape(depth, 256, 512).astype(BF16)
    wuv = (ukv[..., 64:].transpose(0, 2, 1, 3)[:, :, :, None, :] * eye4[None, :, None, :, None])
    wuv = wuv.reshape(depth, 512, 256).astype(BF16)
    wb0 = w_branch[:, 0].reshape(depth, 2, 2, 64, D_MODEL).transpose(0, 2, 1, 3, 4).reshape(depth, 256, D_MODEL)
    wb = jnp.concatenate([wb0[:, None], w_branch[:, 1:]], axis=1).astype(BF16)
    wout = w_out.astype(BF16)
    wu, wg, wd = w_ffu.astype(BF16), w_ffg.astype(BF16), w_ffd.astype(BF16)
    cw = jnp.pad(conv_w, ((0, 0), (0, 8 - conv_w.shape[1]), (0, 0)))
    aqg = jnp.tile(a_qnorm_g, (1, 4))[:, None, :]
    akg = jnp.tile(a_knorm_g, (1, 2))[:, None, :]
    bong = jnp.tile(b_onorm_g, (1, 4))[:, None, :]
    cong = jnp.tile(c_onorm_g, (1, 4))[:, None, :]
    lamv = jnp.stack([c_lq1, c_lk1, c_lq2, c_lk2], axis=1)
    caches = (cache_a_k.reshape(nd, depth, PAST_LEN, 128), cache_a_v.reshape(nd, depth, PAST_LEN, 128),
              cache_c_k.reshape(nd, depth, PAST_LEN, 256), cache_c_v.reshape(nd, depth, PAST_LEN, 256),
              cache_d_ckv, cache_d_krope)
    expand = lambda s: (s.transpose(0, 1, 2, 4, 3)[:, :, :, :, None, :]
                        * eye4[None, None, :, None, :, None]).reshape(nd, depth, 256, 128)
    s0f_all, s0b_all = expand(state_b_fwd), expand(state_b_bwd)

    n1g, n2g = norm1_g[:, None, :], norm2_g[:, None, :]
    small = (aqg, akg, bd, wgate, bgate, dqg, wuq, wuk, d_kvnorm_g[:, None, :], tile)
    cb = conv_b[:, None, :]
    new_caches = None
    sfs, sbs = [], []
    x = (x_prompt, x_sample)
    for l in range(depth):
        lam_init = 0.8 - 0.6 * math.exp(-0.3 * l)
        fa, kva, qa, ha, new_caches = _inproj_calls(x, n, mod4, n1g, w_aq, w_rest, rope_tab, small, new_caches, l,
                                                    depth, n_ctx)
        oacd, ob, sf, sb = _ctx_mixers_call(fa, kva, qa, lamv, cong, wuv, bd, bong, tri, l, lam_init, n_ctx)
        oacd = _attn_lat_call(kva, qa, caches, lamv, cong, wuv, bd, tile, oacd, l, lam_init, n_ctx)
        ob = _gla_lat_call(fa, s0f_all, s0b_all, bong, bd, tri, ob, l, n_ctx)
        x1 = _merge_call(x, n, mod4, ha, w_gates, oacd, ob, wb, wout, l, n_ctx)
        res = _ffn_call(x1, mod4, n2g, wu, wg, cw, cb, wd, final_g[None], l, n_ctx, l == depth - 1)
        x = res[0]
        sfs.append(sf)
        sbs.append(sb)

    diag = lambda s: jnp.stack([s[:, :, hh * 64:(hh + 1) * 64, hh * 32:(hh + 1) * 32] for hh in range(4)],
                               axis=2).transpose(0, 1, 2, 4, 3)
    y_prompt, y_sample = res
    nak, nav, nck, ncv, nckv, nkr = new_caches
    return (y_prompt, y_sample, nak.reshape(nb, depth, SEQ, 2, 64), nav.reshape(nb, depth, SEQ, 2, 64),
            diag(jnp.stack(sfs, axis=1)), diag(jnp.stack(sbs, axis=1)), nck.reshape(nb, depth, SEQ, 4, 2, 32),
            ncv.reshape(nb, depth, SEQ, 4, 64), nckv, nkr)
```

```python
import functools
import math

import numpy as np
import jax
import jax.numpy as jnp
from jax import lax
from jax.experimental import pallas as pl
from jax.experimental.pallas import tpu as pltpu

F32 = jnp.float32
BF16 = jnp.bfloat16

D_MODEL = 1024
SEQ = 256
DEC_SEQ = 1024
PAST_LEN = 512
GRID_W = 64
ROPE_BASE = 10000.0
EPS = 1e-6
LOG2E = math.log2(math.e)
HEAD_W = 64
BRANCH_WIDTH = 256
GLA_KEY_W = 32
GLA_CHUNK = 64
GLA_SUB = 16
GLA_BLOCK = 256
GLA_GATE_NORM = 16.0
D_Q_RANK = 192
D_FF = 2816
FF_CHUNK = 256
N_MOD_ROWS = 16

ROW_TILE = 512
FFN_TILE = 1024
Q_TILE = 256
CTX_SEQ_PER_STEP = 4
LAT_SEQ_PER_GLA_STEP = 2
HALO = 16
VMEM_LIMIT = 56 * 1024 * 1024

IN_AQ_W = 256
IN_GATE_IN = (1280, 1312)
NZ = 2176
Z_AK, Z_AV, Z_BQ, Z_BK, Z_BV, Z_BR, Z_CQ, Z_CK, Z_CV, Z_DQ = 0, 128, 256, 384, 512, 768, 1024, 1280, 1536, 1792
GATES_OFF = 2432
F_W = 1024
F_BQ, F_BK, F_BV, F_BR, F_LGF, F_LGB = 0, 128, 256, 512, 768, 896
KV_W = 1024
KV_AK, KV_AV, KV_CK, KV_CV, KV_DKV, KV_KR = 0, 128, 256, 512, 768, 896
Q_W = 1152
Q_AQ, Q_CQ, Q_DLAT, Q_DQR = 0, 256, 512, 1024
ROPE_W = 768


def _dot(a, b):
    return jnp.dot(a, b, preferred_element_type=F32)


def _dot_nt(a, b):
    return lax.dot_general(a, b, (((1,), (1,)), ((), ())), preferred_element_type=F32)


def _dot_tn(a, b):
    return lax.dot_general(a, b, (((0,), (0,)), ((), ())), preferred_element_type=F32)


def _split3(a):
    hi = a.astype(BF16)
    r1 = a - hi.astype(F32)
    mid = r1.astype(BF16)
    lo = (r1 - mid.astype(F32)).astype(BF16)
    return hi, mid, lo


def _split_dot(a, b01):
    hi, mid, lo = _split3(a)
    return _dot(hi, b01) + _dot(mid, b01) + _dot(lo, b01)


def _split_dot_tn(b01, a):
    hi, mid, lo = _split3(a)
    return _dot(b01, hi) + _dot(b01, mid) + _dot(b01, lo)


def _group_rms(v, bd, width, gain):
    ss = _split_dot(v * v, bd)
    return v * lax.rsqrt(ss * (1.0 / width) + EPS) * gain


def _rope128(v, cos, sin_a, sin_b, shift):
    return v * cos + pltpu.roll(v, 128 - shift, 1) * sin_a + pltpu.roll(v, shift, 1) * sin_b


def _modulated_norm(x, gain, shift, scale):
    ms = jnp.mean(x * x, axis=-1, keepdims=True)
    return (x * lax.rsqrt(ms + EPS) * gain) * (1.0 + scale) + shift


def _lane_mask(width, lo, hi):
    lane = lax.broadcasted_iota(jnp.int32, (1, width), 1)
    return (lane >= lo) & (lane < hi)


def _const_spec(shape, grid_rank):
    zeros = tuple(0 for _ in shape)
    if grid_rank == 1:
        return pl.BlockSpec(shape, lambda i: zeros)
    return pl.BlockSpec(shape, lambda b, i: zeros)


def _layer_spec(shape, layer, grid_rank):
    idx = (layer,) + tuple(0 for _ in shape)
    once = pl.Buffered(1)
    if grid_rank == 1:
        return pl.BlockSpec((None,) + tuple(shape), lambda i: idx, pipeline_mode=once)
    return pl.BlockSpec((None,) + tuple(shape), lambda b, i: idx, pipeline_mode=once)


def _mod_kernel(cv_ref, w_ref, b_ref, o_ref):
    cv = cv_ref[...]
    a = cv * jax.nn.sigmoid(cv)
    o_ref[...] = _dot(a.astype(BF16), w_ref[...].astype(BF16)) + b_ref[...]


def _mod_call(cvec, w_mod, b_mod):
    depth = w_mod.shape[0]
    nblk = 4
    bw = 6 * D_MODEL // nblk
    return pl.pallas_call(
        _mod_kernel,
        grid=(depth, nblk),
        in_specs=[
            pl.BlockSpec((N_MOD_ROWS, D_MODEL), lambda l, j: (0, 0)),
            pl.BlockSpec((None, D_MODEL, bw), lambda l, j: (l, 0, j)),
            pl.BlockSpec((None, 1, bw), lambda l, j: (l, 0, j)),
        ],
        out_specs=pl.BlockSpec((None, N_MOD_ROWS, bw), lambda l, j: (l, 0, j)),
        out_shape=jax.ShapeDtypeStruct((depth, N_MOD_ROWS, 6 * D_MODEL), F32),
        compiler_params=pltpu.CompilerParams(vmem_limit_bytes=VMEM_LIMIT),
        name="mod_vectors",
    )(cvec, w_mod, b_mod.reshape(depth, 1, 6 * D_MODEL))


def _inproj_body(x, mod_ref, n1g_ref, waq_ref, w_ref, rope_ref, aqg_ref, akg_ref, bd_ref, wgate_ref, bgate_ref,
                 dqg_ref, wuq_ref, wuk_ref, dkvg_ref, tile_ref, f_ref, kv_ref, q_ref, h_ref, cache_refs):
    mod = mod_ref[...]
    h = _modulated_norm(x, n1g_ref[...], mod[:, 0:D_MODEL], mod[:, D_MODEL:2 * D_MODEL]).astype(BF16)
    h_ref[...] = h
    z_aq = _dot(h, waq_ref[...])
    z = _dot(h, w_ref[...])
    zd = pltpu.roll(z[:, NZ - 256:NZ], 192, 1)
    bd = bd_ref[...]
    bd128 = bd[0:128, 0:128]
    if rope_ref is None:
        rope64 = rope32 = lambda v: v
    else:
        rope = rope_ref[...]
        cos64, sa64, sb64 = rope[:, 0:128], rope[:, 128:256], rope[:, 256:384]
        cos32, sa32, sb32 = rope[:, 384:512], rope[:, 512:640], rope[:, 640:768]
        rope64 = lambda v: _rope128(v, cos64, sa64, sb64, 16)
        rope32 = lambda v: _rope128(v, cos32, sa32, sb32, 8)

    def emit_kv(col, cache_idx, val, kv_val=None):
        kv_val = val if kv_val is None else kv_val
        kv_ref[:, col:col + kv_val.shape[1]] = kv_val.astype(BF16)
        if cache_refs is not None:
            ref = cache_refs[cache_idx]
            for j in range(val.shape[0] // SEQ):
                ref[j] = val[j * SEQ:(j + 1) * SEQ, 0:ref.shape[-1]]

    aq = _group_rms(z_aq, bd, HEAD_W, aqg_ref[...])
    a_scale = HEAD_W ** -0.5 * LOG2E
    q_ref[:, Q_AQ:Q_AQ + 128] = (rope64(aq[:, 0:128]) * a_scale).astype(BF16)
    q_ref[:, Q_AQ + 128:Q_AQ + 256] = (rope64(aq[:, 128:256]) * a_scale).astype(BF16)
    emit_kv(KV_AK, 0, rope64(_group_rms(z[:, Z_AK:Z_AK + 128], bd128, HEAD_W, akg_ref[...])))
    emit_kv(KV_AV, 1, z[:, Z_AV:Z_AV + 128])

    f_ref[:, F_BQ:F_BK] = z[:, Z_BQ:Z_BK] * (GLA_KEY_W ** -0.5)
    f_ref[:, F_BK:F_BV] = z[:, Z_BK:Z_BV]
    f_ref[:, F_BV:F_BR] = z[:, Z_BV:Z_BR]
    f_ref[:, F_BR:F_LGF] = z[:, Z_BR:Z_CQ]
    pre = _dot(zd[:, 128:256].astype(BF16), wgate_ref[...]) + bgate_ref[...]
    logsig = jnp.minimum(pre, 0.0) - jnp.log1p(jnp.exp(-jnp.abs(pre)))
    f_ref[:, F_LGF:F_W] = logsig / GLA_GATE_NORM

    c_scale = 32 ** -0.5 * LOG2E
    q_ref[:, Q_CQ:Q_CQ + 128] = (rope32(z[:, Z_CQ:Z_CQ + 128]) * c_scale).astype(BF16)
    q_ref[:, Q_CQ + 128:Q_CQ + 256] = (rope32(z[:, Z_CQ + 128:Z_CQ + 256]) * c_scale).astype(BF16)
    emit_kv(KV_CK, 2, jnp.concatenate([rope32(z[:, Z_CK:Z_CK + 128]), rope32(z[:, Z_CK + 128:Z_CK + 256])],
                                      axis=-1))
    emit_kv(KV_CV, 3, z[:, Z_CV:Z_DQ])

    dkv = zd[:, 0:128]
    ms = jnp.mean(dkv * dkv, axis=-1, keepdims=True)
    emit_kv(KV_DKV, 4, dkv * lax.rsqrt(ms + EPS) * dkvg_ref[...])
    kr = rope32(zd[:, 128:256])
    emit_kv(KV_KR, 5, kr, _dot(kr[:, 0:32].astype(BF16), tile_ref[...]))
    zq = jnp.where(_lane_mask(256, 0, D_Q_RANK), z[:, Z_DQ:Z_DQ + 256], 0.0)
    msq = jnp.sum(zq * zq, axis=-1, keepdims=True) * (1.0 / D_Q_RANK)
    qn = zq * lax.rsqrt(msq + EPS) * dqg_ref[...]
    dq = _dot(qn.astype(BF16), wuq_ref[...])
    d_scale = 96 ** -0.5 * LOG2E
    q_ref[:, Q_DLAT:Q_DQR] = (_dot(dq[:, 0:256].astype(BF16), wuk_ref[...]) * d_scale).astype(BF16)
    q_ref[:, Q_DQR:Q_W] = (rope32(dq[:, 256:384]) * d_scale).astype(BF16)


def _inproj_ctx_kernel(x_ref, mod_ref, n1g_ref, waq_ref, w_ref, aqg_ref, akg_ref, bd_ref, wgate_ref, bgate_ref,
                       dqg_ref, wuq_ref, wuk_ref, dkvg_ref, tile_ref, *rest):
    outs = rest[-10:]
    x = x_ref[...].reshape(-1, D_MODEL)
    _inproj_body(x, mod_ref, n1g_ref, waq_ref, w_ref, None, aqg_ref, akg_ref, bd_ref, wgate_ref, bgate_ref,
                 dqg_ref, wuq_ref, wuk_ref, dkvg_ref, tile_ref, outs[0], outs[1], outs[2], outs[3], outs[4:])


def _inproj_lat_kernel(x_ref, mod_ref, n1g_ref, waq_ref, w_ref, rope_ref, aqg_ref, akg_ref, bd_ref, wgate_ref,
                       bgate_ref, dqg_ref, wuq_ref, wuk_ref, dkvg_ref, tile_ref, pf_ref, pkv_ref, pq_ref, ph_ref,
                       f_ref, kv_ref, q_ref, h_ref):
    del pf_ref, pkv_ref, pq_ref, ph_ref
    _inproj_body(x_ref[...], mod_ref, n1g_ref, waq_ref, w_ref, rope_ref, aqg_ref, akg_ref, bd_ref, wgate_ref,
                 bgate_ref, dqg_ref, wuq_ref, wuk_ref, dkvg_ref, tile_ref, f_ref, kv_ref, q_ref, h_ref, None)


CACHE_WIDTHS = (128, 128, 256, 256, 128, 32)


def _inproj_calls(x, n, mod4, n1g, w_aq, w_rest, rope_tab, small, prev_caches, layer, depth, n_ctx_rows):
    tm = ROW_TILE
    ctx_tiles = n_ctx_rows // tm
    seq_tiles = DEC_SEQ // tm
    spt = tm // SEQ
    n_ctx_seq = n_ctx_rows // SEQ
    if isinstance(x, tuple):
        x_ctx, x_lat = x
        x_ctx_spec = pl.BlockSpec((spt, SEQ, D_MODEL), lambda i: (i, 0, 0))
        x_lat_spec = pl.BlockSpec((None, tm, D_MODEL), lambda i: (i // seq_tiles, i % seq_tiles, 0))
    else:
        x_ctx = x_lat = x
        x_ctx_spec = pl.BlockSpec((tm, D_MODEL), lambda i: (i, 0))
        x_lat_spec = pl.BlockSpec((tm, D_MODEL), lambda i: (i + ctx_tiles, 0))
    const = lambda shape: _layer_spec(shape, layer, 1)
    small_specs = [const((1, 256)), const((1, 128)), _const_spec((256, 256), 1), const((128, 256)),
                   const((1, 256)), const((1, 256)), const((256, 384)), const((256, 512)), const((1, 128)),
                   _const_spec((32, 128), 1)]
    act_shapes = [jax.ShapeDtypeStruct((n, F_W), F32), jax.ShapeDtypeStruct((n, KV_W), BF16),
                  jax.ShapeDtypeStruct((n, Q_W), BF16), jax.ShapeDtypeStruct((n, D_MODEL), BF16)]
    act_specs = lambda off: [pl.BlockSpec((tm, w), lambda i: (i + off, 0)) for w in (F_W, KV_W, Q_W, D_MODEL)]
    n_act = len(act_shapes)
    cparams = pltpu.CompilerParams(vmem_limit_bytes=VMEM_LIMIT)
    any_spec = pl.BlockSpec(memory_space=pl.ANY)

    n_in = 5 + len(small)
    alias = {} if prev_caches is None else {n_in + k: n_act + k for k in range(6)}
    res = pl.pallas_call(
        _inproj_ctx_kernel,
        grid=(ctx_tiles,),
        in_specs=[
            x_ctx_spec,
            pl.BlockSpec((None, None, 1, 6 * D_MODEL), lambda i: (layer, 0, 0, 0)),
            const((1, D_MODEL)),
            const((D_MODEL, IN_AQ_W)),
            const((D_MODEL, NZ)),
        ] + small_specs + ([] if prev_caches is None else [any_spec] * 6),
        out_specs=act_specs(0) + [pl.BlockSpec((spt, None, SEQ, w), lambda i: (i, layer, 0, 0))
                                  for w in CACHE_WIDTHS],
        out_shape=act_shapes + [jax.ShapeDtypeStruct((n_ctx_seq, depth, SEQ, w), F32) for w in CACHE_WIDTHS],
        input_output_aliases=alias,
        compiler_params=cparams,
        name="in_projection_context",
    )(x_ctx, mod4, n1g, w_aq, w_rest, *small, *([] if prev_caches is None else prev_caches))
    acts_c, caches = res[:n_act], tuple(res[n_act:])

    fa, kva, qa, ha = pl.pallas_call(
        _inproj_lat_kernel,
        grid=((n - n_ctx_rows) // tm,),
        in_specs=[
            x_lat_spec,
            pl.BlockSpec((None, None, 1, 6 * D_MODEL), lambda i: (layer, 1 + i // seq_tiles, 0, 0)),
            const((1, D_MODEL)),
            const((D_MODEL, IN_AQ_W)),
            const((D_MODEL, NZ)),
            pl.BlockSpec((tm, ROPE_W), lambda i: (i % seq_tiles, 0)),
        ] + small_specs + [any_spec] * n_act,
        out_specs=act_specs(ctx_tiles),
        out_shape=act_shapes,
        input_output_aliases={n_in + 1 + k: k for k in range(n_act)},
        compiler_params=cparams,
        name="in_projection_latent",
    )(x_lat, mod4, n1g, w_aq, w_rest, rope_tab, *small, *acts_c)
    return fa, kva, qa, ha, caches


def _exp2_parts(parts):
    m = parts[0].max(axis=-1, keepdims=True)
    for s in parts[1:]:
        m = jnp.maximum(m, s.max(axis=-1, keepdims=True))
    return [jnp.exp2(s - m) for s in parts]


def _value_variants(av, cv, dk):
    one = jnp.ones((), BF16)
    av1 = [jnp.where(_lane_mask(128, hh * HEAD_W, (hh + 1) * HEAD_W), av, one) for hh in range(2)]
    cv1 = [jnp.where(_lane_mask(256, hh * HEAD_W, (hh + 1) * HEAD_W), cv, one) for hh in range(4)]
    vlat = jnp.where(_lane_mask(256, 0, 128), dk, one)
    return av1, cv1, vlat


def _run(phases):
    for _ in phases:
        pass


def _interleave(*phase_generators):
    live = list(phase_generators)
    while live:
        live = [g for g in live if next(g, StopIteration) is not StopIteration]


def _attn_core(*args):
    _run(_attn_phases(*args))


def _attn_phases(q, ak, av1, ck, cv1, dk, vlat, lam_ref, cong_ref, wuv_ref, bd_ref, o_ref, lam_init):
    zero = jnp.zeros((), BF16)
    ak, ck, dk = [ak], [ck], [dk]
    tq = q.shape[0]

    masks_a = [_lane_mask(128, hh * HEAD_W, (hh + 1) * HEAD_W) for hh in range(2)]
    qa = jnp.concatenate([jnp.where(masks_a[hh], q[:, Q_AQ + g * 128:Q_AQ + (g + 1) * 128], zero)
                          for hh in range(2) for g in range(2)], axis=0)
    qcs = [jnp.concatenate([jnp.where(_lane_mask(128, u * 32, (u + 1) * 32),
                                      q[:, Q_CQ + c * 128:Q_CQ + (c + 1) * 128], zero) for u in range(4)], axis=0)
           for c in range(2)]
    qd = jnp.concatenate(
        [jnp.concatenate([q[:, Q_DLAT + hh * 128:Q_DLAT + (hh + 1) * 128],
                          jnp.where(_lane_mask(128, hh * 32, (hh + 1) * 32), q[:, Q_DQR:Q_W], zero)], axis=-1)
         for hh in range(4)], axis=0)
    s_a = [_dot_nt(qa, k) for k in ak]
    s_c = [[_dot_nt(qcs[c], k[:, c * 128:(c + 1) * 128]) for k in ck] for c in range(2)]
    yield

    ps = [p.astype(BF16) for p in _exp2_parts(s_a)]
    s_d = [_dot_nt(qd, k) for k in dk]
    oa = [None, None]
    for hh in range(2):
        rows = slice(hh * 2 * tq, (hh + 1) * 2 * tq)
        o = _dot(ps[0][rows], av1[hh])
        o = jnp.where(masks_a[hh], o / pltpu.roll(o, HEAD_W, 1), 0.0)
        for g in range(2):
            part = o[g * tq:(g + 1) * tq]
            oa[g] = part if oa[g] is None else oa[g] + part
    o_ref[:, 0:128] = oa[0]
    o_ref[:, 128:256] = oa[1]
    yield

    lv = lam_ref[...]
    lam = (jnp.exp(jnp.sum(lv[0:1] * lv[1:2], axis=-1, keepdims=True))
           - jnp.exp(jnp.sum(lv[2:3] * lv[3:4], axis=-1, keepdims=True)) + lam_init)
    oc = []
    for c in range(2):
        c0 = c * 128
        ps = [p.astype(BF16) for p in _exp2_parts(s_c[c])]
        och = None
        for h2 in range(2):
            hh = c * 2 + h2
            rows = slice(h2 * 2 * tq, (h2 + 1) * 2 * tq)
            o = _dot(ps[0][rows], cv1[hh])
            res = o[:, c0:c0 + 128] / o[:, 128 - c0:256 - c0]
            o = jnp.where(_lane_mask(128, h2 * HEAD_W, (h2 + 1) * HEAD_W), res[0:tq] - lam * res[tq:2 * tq], 0.0)
            och = o if och is None else och + o
            yield
        oc.append(och)
    oc = jnp.concatenate(oc, axis=-1)
    o_ref[:, 256:512] = _group_rms(oc, bd_ref[...], HEAD_W, cong_ref[...]) * (1.0 - lam_init)

    ps = [p.astype(BF16) for p in _exp2_parts(s_d)]
    o = _dot(ps[0], vlat)
    lat = (o[:, 0:128] / o[:, 128:256]).astype(BF16)
    o_ref[:, 512:768] = _dot(jnp.concatenate([lat[hh * tq:(hh + 1) * tq] for hh in range(4)], axis=-1),
                             wuv_ref[...])


def _attn_lat_kernel(q_ref, kva_ref, ck_ref, cv_ref, dkv_ref, cak_ref, cav_ref, cck_ref, ccv_ref,
                     cckv_ref, ckr_ref, lam_ref, cong_ref, wuv_ref, bd_ref, tile_ref, prev_ref, o_ref,
                     ak_s, ck_s, dk_s, av1_s, cv1_s, vlat_s, *, lam_init):
    del prev_ref

    @pl.when(pl.program_id(1) == 0)
    def _():
        p = PAST_LEN
        kva = kva_ref[...]
        kr4 = _dot(ckr_ref[...].astype(BF16), tile_ref[...]).astype(BF16)
        pieces = ((cak_ref[...].astype(BF16), cav_ref[...].astype(BF16), cck_ref[...].astype(BF16),
                   ccv_ref[...].astype(BF16), jnp.concatenate([cckv_ref[...].astype(BF16), kr4], axis=-1)),
                  (kva[:, 0:128], kva[:, 128:256], ck_ref[...], cv_ref[...], dkv_ref[...]))
        for rows, (ak, av, ck, cv, dk) in zip((slice(0, p), slice(p, p + DEC_SEQ)), pieces):
            av1, cv1, vlat = _value_variants(av, cv, dk)
            ak_s[rows] = ak
            ck_s[rows] = ck
            dk_s[rows] = dk
            vlat_s[rows] = vlat
            for hh in range(2):
                av1_s[hh, rows] = av1[hh]
            for hh in range(4):
                cv1_s[hh, rows] = cv1[hh]

    _attn_core(q_ref[...], ak_s[...], [av1_s[hh] for hh in range(2)], ck_s[...],
               [cv1_s[hh] for hh in range(4)], dk_s[...], vlat_s[...], lam_ref, cong_ref, wuv_ref, bd_ref,
               o_ref, lam_init)


def _attn_lat_call(kva, qa, caches, lamv, cong, wuv, bd, tile, o_ctx, layer, lam_init, n_ctx_rows):
    n = kva.shape[0]
    n_lat_seq = (n - n_ctx_rows) // DEC_SEQ
    param_specs = lambda rank: [_layer_spec((4, 32), layer, rank), _layer_spec((1, 256), layer, rank),
                                _layer_spec((512, 256), layer, rank), _const_spec((256, 256), rank),
                                _const_spec((32, 128), rank)]
    qt = Q_TILE
    q_per_seq = DEC_SEQ // qt
    q0 = n_ctx_rows // qt
    s0 = n_ctx_rows // DEC_SEQ
    cak, cav, cck, ccv, cckv, ckr = caches
    cache_spec = lambda w: pl.BlockSpec((None, None, PAST_LEN, w), lambda b, i: (b, layer, 0, 0))
    seq_cols = lambda idx: pl.BlockSpec((DEC_SEQ, 256), lambda b, i: (s0 + b, idx))
    n_keys = PAST_LEN + DEC_SEQ
    return pl.pallas_call(
        functools.partial(_attn_lat_kernel, lam_init=lam_init),
        grid=(n_lat_seq, q_per_seq),
        in_specs=[
            pl.BlockSpec((qt, Q_W), lambda b, i: (q0 + b * q_per_seq + i, 0)),
            seq_cols(KV_AK // 256), seq_cols(KV_CK // 256), seq_cols(KV_CV // 256), seq_cols(KV_DKV // 256),
            cache_spec(128), cache_spec(128), cache_spec(256), cache_spec(256), cache_spec(128),
            cache_spec(32),
        ] + param_specs(2) + [pl.BlockSpec(memory_space=pl.ANY)],
        out_specs=pl.BlockSpec((qt, 3 * BRANCH_WIDTH), lambda b, i: (q0 + b * q_per_seq + i, 0)),
        out_shape=jax.ShapeDtypeStruct((n, 3 * BRANCH_WIDTH), F32),
        scratch_shapes=[pltpu.VMEM((n_keys, 128), BF16), pltpu.VMEM((n_keys, 256), BF16),
                        pltpu.VMEM((n_keys, 256), BF16), pltpu.VMEM((2, n_keys, 128), BF16),
                        pltpu.VMEM((4, n_keys, 256), BF16), pltpu.VMEM((n_keys, 256), BF16)],
        input_output_aliases={16: 0},
        compiler_params=pltpu.CompilerParams(vmem_limit_bytes=VMEM_LIMIT),
        name="attention_latent",
    )(qa, kva, kva, kva, kva, cak, cav, cck, ccv, cckv, ckr, lamv, cong, wuv, bd, tile, o_ctx)


def _gla_chunk(q, k, v, bc, st, reverse, masks, result):
    c, sub, nsub = GLA_CHUNK, GLA_SUB, GLA_CHUNK // GLA_SUB
    keep, qmask, src_row, blk = masks
    edge = bc[0:1] if reverse else bc[c - 1:c]
    qs, kd = [], []
    for i in range(nsub):
        r0 = i * sub
        ref = bc[r0 + sub - 1:r0 + sub] if reverse else bc[r0:r0 + 1]
        qd = q[r0:r0 + sub] * jnp.exp(bc[r0:r0 + sub] - ref)
        qs.append(jnp.where(qmask, jnp.concatenate([qd] * 4, axis=0), 0.0))
        in_range = (src_row >= r0) if reverse else (src_row < r0 + sub)
        kd.append(k * jnp.exp(jnp.where(in_range, ref - bc, 0.0)))
    att = _dot_nt(jnp.concatenate(qs, axis=0).astype(BF16), jnp.concatenate(kd, axis=0).astype(BF16))
    vb = v.astype(BF16)
    q_dec = q * jnp.exp(bc)
    k_dec = k * jnp.exp(edge - bc)
    o_inter = _dot_nt(q_dec.astype(BF16), st.astype(BF16))
    upd = _dot_tn(vb, k_dec.astype(BF16))
    yield
    att = jnp.where(keep, att, 0.0).astype(BF16)
    pv = _dot(att, jnp.concatenate([vb] * nsub, axis=0))
    st_new = st * jnp.exp(edge) + jnp.where(blk, upd, 0.0)
    yield
    out_blocks = []
    for i in range(nsub):
        o = None
        for hh in range(4):
            r0 = i * c + hh * sub
            part = jnp.where(_lane_mask(256, hh * HEAD_W, (hh + 1) * HEAD_W), pv[r0:r0 + sub], 0.0)
            o = part if o is None else o + part
        out_blocks.append(o)
    o_intra = jnp.concatenate(out_blocks, axis=0)
    result.extend([o_intra + o_inter, st_new])


def _gla_masks(reverse):
    c, sub = GLA_CHUNK, GLA_SUB
    row = lax.broadcasted_iota(jnp.int32, (4 * c, 4 * c), 0)
    col = lax.broadcasted_iota(jnp.int32, (4 * c, 4 * c), 1)
    t = (row // c) * sub + row % sub
    s = col % c
    keep = (row // c == col // c) & ((s >= t) if reverse else (s <= t))
    qmask = (lax.broadcasted_iota(jnp.int32, (4 * sub, 128), 0) // sub
             == lax.broadcasted_iota(jnp.int32, (4 * sub, 128), 1) // GLA_KEY_W)
    src_row = lax.broadcasted_iota(jnp.int32, (c, 1), 0)
    blk = (lax.broadcasted_iota(jnp.int32, (256, 128), 0) // HEAD_W
           == lax.broadcasted_iota(jnp.int32, (256, 128), 1) // GLA_KEY_W)
    return keep, qmask, src_row, blk


def _gla_parts(qk_ref, v_ref, r_ref, lg_ref, g_ref, bd_ref, tri_ref, o_ref, accf_ref, accb_ref, bcf_ref, bcb_ref,
               seq_len):
    nc = seq_len // GLA_CHUNK
    masks_f, masks_b = _gla_masks(False), _gla_masks(True)

    def decays():
        for r0 in range(0, seq_len, GLA_BLOCK):
            lg = lg_ref[r0:r0 + GLA_BLOCK, :]
            for d, bc_ref in ((0, bcf_ref), (1, bcb_ref)):
                res = _dot(tri_ref[d], jnp.concatenate(_split3(lg[:, d * 128:(d + 1) * 128]), axis=-1))
                bc_ref[r0:r0 + GLA_BLOCK, :] = res[:, 0:128] + res[:, 128:256] + res[:, 256:384]

    def chunk_rows(c):
        start = c * GLA_CHUNK
        return pl.ds(start if isinstance(c, int) else pl.multiple_of(start, GLA_CHUNK), GLA_CHUNK)

    def pair_phases(ci, carry, new_carry):
        rf, rb = chunk_rows(ci), chunk_rows(nc - 1 - ci)
        res_f, res_b = [], []
        live = [_gla_chunk(qk_ref[rf, 0:128], qk_ref[rf, 128:256], v_ref[rf, :], bcf_ref[rf, :], carry[0], False,
                           masks_f, res_f),
                _gla_chunk(qk_ref[rb, 0:128], qk_ref[rb, 128:256], v_ref[rb, :], bcb_ref[rb, :], carry[1], True,
                           masks_b, res_b)]
        while live:
            live = [g for g in live if next(g, StopIteration) is not StopIteration]
            yield
        accf_ref[rf, :] = res_f[0]
        accb_ref[rb, :] = res_b[0]
        new_carry[:] = [res_f[1], res_b[1]]

    def finish():
        r = r_ref[...]
        o_ref[...] = (_group_rms(accf_ref[...] + accb_ref[...], bd_ref[...], HEAD_W, g_ref[...])
                      * (r * jax.nn.sigmoid(r)))

    return decays, pair_phases, finish


def _gla_seq_phases(parts, n_pairs, carry, states):
    decays, pair_phases, finish = parts
    decays()
    yield
    for ci in range(n_pairs):
        nxt = []
        yield from pair_phases(ci, carry, nxt)
        carry = tuple(nxt)
    finish()
    states.extend(carry)


def _ctx_mixers_kernel(q_ref, kva_ref, ck_ref, cv_ref, dkv_ref, qk_ref, v_ref, r_ref, lg_ref, lam_ref, cong_ref,
                       wuv_ref, bd_ref, bong_ref, tri_ref, o_ref, ob_ref, sf_ref, sb_ref, *scratch, lam_init):
    zero = jnp.zeros((256, 128), F32)
    nseq = q_ref.shape[0] // SEQ
    gens, states = [], [[] for _ in range(nseq)]
    for j in range(nseq):
        rows = slice(j * SEQ, (j + 1) * SEQ)
        kva = kva_ref[rows, :]
        dk = dkv_ref[rows, :]
        av1, cv1, vlat = _value_variants(kva[:, 128:256], cv_ref[rows, :], dk)
        gens.append(_attn_phases(q_ref[rows, :], kva[:, 0:128], av1, ck_ref[rows, :], cv1, dk, vlat, lam_ref,
                                 cong_ref, wuv_ref, bd_ref, o_ref.at[rows, :], lam_init))
        parts = _gla_parts(qk_ref.at[rows, :], v_ref.at[rows, :], r_ref.at[rows, :], lg_ref.at[rows, :], bong_ref,
                           bd_ref, tri_ref, ob_ref.at[rows, :], *[s.at[rows, :] for s in scratch], SEQ)
        gens.append(_gla_seq_phases(parts, SEQ // GLA_CHUNK, (zero, zero), states[j]))
    _interleave(*gens)
    for j in range(nseq):
        sf_ref[j], sb_ref[j] = states[j]


def _ctx_mixers_call(fa, kva, qa, lamv, cong, wuv, bd, bong, tri, layer, lam_init, n_ctx_rows):
    n = fa.shape[0]
    n_ctx_seq = n_ctx_rows // SEQ
    nseq = CTX_SEQ_PER_STEP
    crows = nseq * SEQ
    const = lambda shape: _const_spec(shape, 1)
    lspec = lambda shape: _layer_spec(shape, layer, 1)
    col = lambda w, off: pl.BlockSpec((crows, w), lambda b: (b, off // w))
    st_shape = jax.ShapeDtypeStruct((n_ctx_seq, 256, 128), F32)
    st_spec = pl.BlockSpec((nseq, 256, 128), lambda b: (b, 0, 0))
    return pl.pallas_call(
        functools.partial(_ctx_mixers_kernel, lam_init=lam_init),
        grid=(n_ctx_seq // nseq,),
        in_specs=[
            col(Q_W, 0), col(256, KV_AK), col(256, KV_CK), col(256, KV_CV), col(256, KV_DKV),
            col(256, F_BQ), col(256, F_BV), col(256, F_BR), col(256, F_LGF),
            lspec((4, 32)), lspec((1, 256)), lspec((512, 256)), const((256, 256)), lspec((1, 256)),
            const((2, GLA_BLOCK, GLA_BLOCK)),
        ],
        out_specs=[col(3 * BRANCH_WIDTH, 0), col(BRANCH_WIDTH, 0), st_spec, st_spec],
        out_shape=[jax.ShapeDtypeStruct((n, 3 * BRANCH_WIDTH), F32), jax.ShapeDtypeStruct((n, BRANCH_WIDTH), F32),
                   st_shape, st_shape],
        scratch_shapes=[pltpu.VMEM((crows, BRANCH_WIDTH), F32), pltpu.VMEM((crows, BRANCH_WIDTH), F32),
                        pltpu.VMEM((crows, 128), F32), pltpu.VMEM((crows, 128), F32)],
        compiler_params=pltpu.CompilerParams(vmem_limit_bytes=VMEM_LIMIT),
        name="mixers_context",
    )(qa, kva, kva, kva, kva, fa, fa, fa, fa, lamv, cong, wuv, bd, bong, tri)


def _gla_lat_kernel(qk_ref, v_ref, r_ref, lg_ref, s0f_ref, s0b_ref, g_ref, bd_ref, tri_ref, prev_ref,
                    o_ref, *scratch):
    del prev_ref
    nseq = qk_ref.shape[0] // DEC_SEQ
    parts = []
    for j in range(nseq):
        rows = slice(j * DEC_SEQ, (j + 1) * DEC_SEQ)
        parts.append(_gla_parts(qk_ref.at[rows, :], v_ref.at[rows, :], r_ref.at[rows, :], lg_ref.at[rows, :], g_ref,
                                bd_ref, tri_ref, o_ref.at[rows, :], *[s.at[rows, :] for s in scratch], DEC_SEQ))
    for decays, _, _ in parts:
        decays()

    def step(ci, carry):
        nxt = [[] for _ in range(nseq)]
        _interleave(*[parts[j][1](ci, carry[2 * j:2 * j + 2], nxt[j]) for j in range(nseq)])
        return tuple(s for pair in nxt for s in pair)

    init = tuple(s for j in range(nseq) for s in (s0f_ref[j], s0b_ref[j]))
    lax.fori_loop(0, DEC_SEQ // GLA_CHUNK, step, init)
    for _, _, finish in parts:
        finish()


def _gla_lat_call(fa, s0f_t, s0b_t, bong, bd, tri, o_ctx, layer, n_ctx_rows):
    n = fa.shape[0]
    n_lat_seq = (n - n_ctx_rows) // DEC_SEQ
    const = lambda shape: _const_spec(shape, 1)
    nseq = LAT_SEQ_PER_GLA_STEP
    rows = nseq * DEC_SEQ
    state_spec = pl.BlockSpec((nseq, None, 256, 128), lambda b: (b, layer, 0, 0))
    s0 = n_ctx_rows // rows
    seq_cols = lambda off: pl.BlockSpec((rows, 256), lambda b: (s0 + b, off // 256))
    return pl.pallas_call(
        _gla_lat_kernel,
        grid=(n_lat_seq // nseq,),
        in_specs=[
            seq_cols(F_BQ), seq_cols(F_BV), seq_cols(F_BR), seq_cols(F_LGF),
            state_spec, state_spec,
            _layer_spec((1, 256), layer, 1), const((256, 256)), const((2, GLA_BLOCK, GLA_BLOCK)),
            pl.BlockSpec(memory_space=pl.ANY),
        ],
        out_specs=pl.BlockSpec((rows, BRANCH_WIDTH), lambda b: (s0 + b, 0)),
        out_shape=jax.ShapeDtypeStruct((n, BRANCH_WIDTH), F32),
        scratch_shapes=[pltpu.VMEM((rows, BRANCH_WIDTH), F32), pltpu.VMEM((rows, BRANCH_WIDTH), F32),
                        pltpu.VMEM((rows, 128), F32), pltpu.VMEM((rows, 128), F32)],
        input_output_aliases={9: 0},
        compiler_params=pltpu.CompilerParams(vmem_limit_bytes=VMEM_LIMIT),
        name="gla_latent",
    )(fa, fa, fa, fa, s0f_t, s0b_t, bong, bd, tri, o_ctx)


def _merge_kernel(*refs, ctx_tiles):
    mod_ref, h_ref, wg_ref, oacd_ref, ob_ref, wb_ref, wout_ref, o_ref = refs[-8:]
    if len(refs) == 9:
        x = refs[0][...]
    else:
        x = jnp.where(pl.program_id(0) < ctx_tiles, refs[0][...].reshape(-1, D_MODEL), refs[1][...])
    mod = mod_ref[...]
    h = h_ref[...]
    oacd = oacd_ref[...]
    branches = [oacd[:, 0:256], ob_ref[...], oacd[:, 256:512], oacd[:, 512:768]]
    acc = None
    for j in range(4):
        gate = jax.nn.sigmoid(_dot(h, wg_ref[:, j * D_MODEL:(j + 1) * D_MODEL]))
        term = gate * _dot(branches[j].astype(BF16), wb_ref[j])
        acc = term if acc is None else acc + term
    mix = _dot(acc.astype(BF16), wout_ref[...])
    o_ref[...] = x + mod[:, 2 * D_MODEL:3 * D_MODEL] * mix


def _mod_map(layer, ctx_tiles, seq_tiles):
    return lambda i: (layer, jnp.where(i < ctx_tiles, 0, 1 + (i - ctx_tiles) // seq_tiles), 0, 0)


def _merge_call(x, n, mod4, ha, wgates, oacd, ob, wb, wout, layer, n_ctx_rows):
    tm = ROW_TILE
    ctx_tiles = n_ctx_rows // tm
    seq_tiles = DEC_SEQ // tm
    const = lambda shape: _layer_spec(shape, layer, 1)
    if isinstance(x, tuple):
        xs = list(x)
        x_specs = [
            pl.BlockSpec((tm // SEQ, SEQ, D_MODEL), lambda i: (jnp.minimum(i, ctx_tiles - 1), 0, 0)),
            pl.BlockSpec((None, tm, D_MODEL), lambda i: (jnp.maximum(i - ctx_tiles, 0) // seq_tiles,
                                                         jnp.maximum(i - ctx_tiles, 0) % seq_tiles, 0)),
        ]
    else:
        xs = [x]
        x_specs = [pl.BlockSpec((tm, D_MODEL), lambda i: (i, 0))]
    return pl.pallas_call(
        functools.partial(_merge_kernel, ctx_tiles=ctx_tiles),
        grid=(n // tm,),
        in_specs=x_specs + [
            pl.BlockSpec((None, None, 1, 6 * D_MODEL), _mod_map(layer, ctx_tiles, seq_tiles)),
            pl.BlockSpec((tm, D_MODEL), lambda i: (i, 0)),
            const((D_MODEL, 4 * D_MODEL)),
            pl.BlockSpec((tm, 3 * BRANCH_WIDTH), lambda i: (i, 0)),
            pl.BlockSpec((tm, BRANCH_WIDTH), lambda i: (i, 0)),
            const((4, BRANCH_WIDTH, D_MODEL)),
            const((D_MODEL, D_MODEL)),
        ],
        out_specs=pl.BlockSpec((tm, D_MODEL), lambda i: (i, 0)),
        out_shape=jax.ShapeDtypeStruct((n, D_MODEL), F32),
        compiler_params=pltpu.CompilerParams(vmem_limit_bytes=VMEM_LIMIT),
        name="merge",
    )(*xs, mod4, ha, wgates, oacd, ob, wb, wout)


def _ffn_kernel(x_ref, xp_ref, xn_ref, mod_ref, n2g_ref, wu_ref, wg_ref, cw_ref, cb_ref, wd_ref, fg_ref,
                *rest, ctx_tiles, final):
    out_refs, (hext_ref, act_ref) = rest[:-2], rest[-2:]
    tm = x_ref.shape[0]
    i = pl.program_id(0)
    x = x_ref[...]
    mod = mod_ref[...]
    sh2, sc2, g2 = mod[:, 3 * D_MODEL:4 * D_MODEL], mod[:, 4 * D_MODEL:5 * D_MODEL], mod[:, 5 * D_MODEL:]
    n2g = n2g_ref[...]
    hext_ref[0:HALO] = _modulated_norm(xp_ref[...], n2g, sh2, sc2).astype(BF16)
    hext_ref[HALO:HALO + tm] = _modulated_norm(x, n2g, sh2, sc2).astype(BF16)
    hext_ref[HALO + tm:] = _modulated_norm(xn_ref[...], n2g, sh2, sc2).astype(BF16)
    seq_len = jnp.where(i < ctx_tiles, SEQ, DEC_SEQ)
    row = lax.broadcasted_iota(jnp.int32, (tm, 1), 0)
    pos = (row + i * tm) & (seq_len - 1)
    first = pos == 0
    last = pos == seq_len - 1
    for c0 in range(0, D_FF, FF_CHUNK):
        u_ext = _dot(hext_ref[...], wu_ref[:, c0:c0 + FF_CHUNK])
        u = u_ext[HALO:HALO + tm]
        u_dn = jnp.where(first, 0.0, u_ext[HALO - 1:HALO - 1 + tm])
        u_up = jnp.where(last, 0.0, u_ext[HALO + 1:HALO + 1 + tm])
        cw = cw_ref[:, c0:c0 + FF_CHUNK]
        uc = u_dn * cw[0:1] + u * cw[1:2] + u_up * cw[2:3] + cb_ref[:, c0:c0 + FF_CHUNK]
        g = _dot(hext_ref[HALO:HALO + tm], wg_ref[:, c0:c0 + FF_CHUNK])
        act_ref[:, c0:c0 + FF_CHUNK] = (jax.nn.gelu(uc) * g).astype(BF16)
    out = x + g2 * _dot(act_ref[...], wd_ref[...])
    if final:
        yp_ref, ys_ref = out_refs
        ms = jnp.mean(out * out, axis=-1, keepdims=True)
        y = out * lax.rsqrt(ms + EPS) * fg_ref[...]

        @pl.when(i < ctx_tiles)
        def _():
            for j in range(tm // SEQ):
                yp_ref[j] = y[j * SEQ:(j + 1) * SEQ]

        @pl.when(i >= ctx_tiles)
        def _():
            ys_ref[...] = y
    else:
        out_refs[0][...] = out


def _ffn_call(x, mod4, n2g, wu, wg, cw, cb, wd, fg, layer, n_ctx_rows, final):
    n = x.shape[0]
    tm = FFN_TILE
    ctx_tiles = n_ctx_rows // tm
    seq_tiles = DEC_SEQ // tm
    hb = tm // HALO
    last_halo = n // HALO - 1
    lspec = lambda shape: _layer_spec(shape, layer, 1)
    if final:
        spt = tm // SEQ
        out_specs = [
            pl.BlockSpec((spt, SEQ, D_MODEL), lambda i: (jnp.minimum(i, ctx_tiles - 1), 0, 0)),
            pl.BlockSpec((None, tm, D_MODEL), lambda i: (jnp.maximum(i - ctx_tiles, 0) // seq_tiles,
                                                         jnp.maximum(i - ctx_tiles, 0) % seq_tiles, 0)),
        ]
        out_shape = [jax.ShapeDtypeStruct((n_ctx_rows // SEQ, SEQ, D_MODEL), F32),
                     jax.ShapeDtypeStruct(((n - n_ctx_rows) // DEC_SEQ, DEC_SEQ, D_MODEL), F32)]
    else:
        out_specs = [pl.BlockSpec((tm, D_MODEL), lambda i: (i, 0))]
        out_shape = [jax.ShapeDtypeStruct((n, D_MODEL), F32)]
    return pl.pallas_call(
        functools.partial(_ffn_kernel, ctx_tiles=ctx_tiles, final=final),
        grid=(n // tm,),
        in_specs=[
            pl.BlockSpec((tm, D_MODEL), lambda i: (i, 0)),
            pl.BlockSpec((HALO, D_MODEL), lambda i: (jnp.maximum(i * hb - 1, 0), 0)),
            pl.BlockSpec((HALO, D_MODEL), lambda i: (jnp.minimum((i + 1) * hb, last_halo), 0)),
            pl.BlockSpec((None, None, 1, 6 * D_MODEL), _mod_map(layer, ctx_tiles, seq_tiles)),
            lspec((1, D_MODEL)),
            lspec((D_MODEL, D_FF)), lspec((D_MODEL, D_FF)), lspec((8, D_FF)), lspec((1, D_FF)),
            lspec((D_FF, D_MODEL)), _const_spec((1, D_MODEL), 1),
        ],
        out_specs=out_specs,
        out_shape=out_shape,
        scratch_shapes=[pltpu.VMEM((tm + 2 * HALO, D_MODEL), BF16), pltpu.VMEM((tm, D_FF), BF16)],
        compiler_params=pltpu.CompilerParams(vmem_limit_bytes=VMEM_LIMIT),
        name="conv_ffn",
    )(x, x, x, mod4, n2g, wu, wg, cw, cb, wd, fg)


def _rope_tables():
    t = np.arange(DEC_SEQ)
    row = (t // GRID_W).astype(np.float32)
    col = (t % GRID_W).astype(np.float32)

    def tables(head_dim):
        r = head_dim // 2
        half = r // 2
        inv = (np.float32(ROPE_BASE) ** (-np.arange(half, dtype=np.float32) / np.float32(half))).astype(np.float32)
        lane = np.arange(128) % head_dim
        use_col = lane >= r
        e = lane % r
        first_half = e < half
        freq = inv[e % half]
        pos = np.where(use_col[None, :], col[:, None], row[:, None]).astype(np.float32)
        ang = (pos * freq[None, :]).astype(np.float32)
        cos = np.cos(ang).astype(np.float32)
        sin = np.sin(ang).astype(np.float32)
        sin_a = np.where(first_half[None, :], -sin, 0.0).astype(np.float32)
        sin_b = np.where(first_half[None, :], 0.0, sin).astype(np.float32)
        return cos, sin_a, sin_b

    return jnp.asarray(np.concatenate(list(tables(64)) + list(tables(32)), axis=1))


def kernel(x_prompt, x_sample, c, cache_a_k, cache_a_v, state_b_fwd, state_b_bwd, cache_c_k, cache_c_v, cache_d_ckv, cache_d_krope, c_ctx, w_mod, b_mod, norm1_g, norm2_g, w_in, a_qnorm_g, a_knorm_g, b_gate_w_fwd, b_gate_b_fwd, b_gate_w_bwd, b_gate_b_bwd, b_onorm_g, c_lq1, c_lk1, c_lq2, c_lk2, c_onorm_g, d_qnorm_g, d_w_uq, d_kvnorm_g, d_w_ukv, w_branch, w_out, w_ffu, w_ffg, conv_w, conv_b, w_ffd, final_g):
    depth = w_in.shape[0]
    nb, nd = x_prompt.shape[0], x_sample.shape[0]
    n_ctx = nb * SEQ
    assert n_ctx % (LAT_SEQ_PER_GLA_STEP * DEC_SEQ) == 0 and nd % LAT_SEQ_PER_GLA_STEP == 0
    assert nb % CTX_SEQ_PER_STEP == 0 and n_ctx % FFN_TILE == 0 and nd + 1 <= N_MOD_ROWS

    n = n_ctx + nd * DEC_SEQ
    cvec = jnp.zeros((N_MOD_ROWS, D_MODEL), F32).at[0].set(c_ctx).at[1:1 + nd].set(c)
    mod4 = _mod_call(cvec, w_mod, b_mod).reshape(depth, N_MOD_ROWS, 1, 6 * D_MODEL)

    rope_tab = _rope_tables()
    lane = np.arange(256)
    bd = jnp.asarray(lane[:, None] // HEAD_W == lane[None, :] // HEAD_W, BF16)
    tile = jnp.asarray(np.arange(32)[:, None] == np.arange(128)[None, :] % 32, BF16)
    tt = np.arange(GLA_BLOCK)
    same_chunk = tt[:, None] // GLA_CHUNK == tt[None, :] // GLA_CHUNK
    tri = jnp.asarray(np.stack([same_chunk & (tt[:, None] >= tt[None, :]),
                                same_chunk & (tt[:, None] <= tt[None, :])]), BF16)

    w_aq = (w_in[:, :, 0:IN_AQ_W].reshape(depth, D_MODEL, 2, 2, HEAD_W).transpose(0, 1, 3, 2, 4)
            .reshape(depth, D_MODEL, IN_AQ_W).astype(BF16))
    g0, g1 = IN_GATE_IN
    w_rest = jnp.concatenate([w_in[:, :, IN_AQ_W:g0], w_in[:, :, g1:GATES_OFF], w_in[:, :, g0:g1]],
                             axis=-1).astype(BF16)
    w_gates = w_in[:, :, GATES_OFF:].astype(BF16)
    wgate = jnp.zeros((depth, 128, 256), F32)
    wgate = wgate.at[:, 32:48, 0:128].set(b_gate_w_fwd).at[:, 48:64, 128:256].set(b_gate_w_bwd).astype(BF16)
    bgate = jnp.concatenate([b_gate_b_fwd, b_gate_b_bwd], axis=-1)[:, None, :]
    dqg = jnp.pad(d_qnorm_g, ((0, 0), (0, 256 - D_Q_RANK)))[:, None, :]
    uq = d_w_uq.reshape(depth, D_Q_RANK, 4, 96)
    wuq = jnp.concatenate([uq[..., :64].reshape(depth, D_Q_RANK, 256), uq[..., 64:].reshape(depth, D_Q_RANK, 128)], axis=-1)
    wuq = jnp.pad(wuq, ((0, 0), (0, 256 - D_Q_RANK), (0, 0))).astype(BF16)
    ukv = d_w_ukv.reshape(depth, 128, 4, 128)
    eye4 = jnp.eye(4, dtype=F32)
    wuk = (ukv[..., :64].transpose(0, 2, 3, 1)[:, :, :, None, :] * eye4[None, :, None, :, None])
    wuk = wuk.reshape(depth, 256, 512).astype(BF16)
    wuv = (ukv[..., 64:].transpose(0, 2, 1, 3)[:, :, :, None, :] * eye4[None, :, None, :, None])
    wuv = wuv.reshape(depth, 512, 256).astype(BF16)
    wb0 = w_branch[:, 0].reshape(depth, 2, 2, 64, D_MODEL).transpose(0, 2, 1, 3, 4).reshape(depth, 256, D_MODEL)
    wb = jnp.concatenate([wb0[:, None], w_branch[:, 1:]], axis=1).astype(BF16)
    wout = w_out.astype(BF16)
    wu, wg, wd = w_ffu.astype(BF16), w_ffg.astype(BF16), w_ffd.astype(BF16)
    cw = jnp.pad(conv_w, ((0, 0), (0, 8 - conv_w.shape[1]), (0, 0)))
    aqg = jnp.tile(a_qnorm_g, (1, 4))[:, None, :]
    akg = jnp.tile(a_knorm_g, (1, 2))[:, None, :]
    bong = jnp.tile(b_onorm_g, (1, 4))[:, None, :]
    cong = jnp.tile(c_onorm_g, (1, 4))[:, None, :]
    lamv = jnp.stack([c_lq1, c_lk1, c_lq2, c_lk2], axis=1)
    caches = (cache_a_k.reshape(nd, depth, PAST_LEN, 128), cache_a_v.reshape(nd, depth, PAST_LEN, 128),
              cache_c_k.reshape(nd, depth, PAST_LEN, 256), cache_c_v.reshape(nd, depth, PAST_LEN, 256),
              cache_d_ckv, cache_d_krope)
    expand = lambda s: (s.transpose(0, 1, 2, 4, 3)[:, :, :, :, None, :]
                        * eye4[None, None, :, None, :, None]).reshape(nd, depth, 256, 128)
    s0f_all, s0b_all = expand(state_b_fwd), expand(state_b_bwd)

    n1g, n2g = norm1_g[:, None, :], norm2_g[:, None, :]
    small = (aqg, akg, bd, wgate, bgate, dqg, wuq, wuk, d_kvnorm_g[:, None, :], tile)
    cb = conv_b[:, None, :]
    new_caches = None
    sfs, sbs = [], []
    x = (x_prompt, x_sample)
    for l in range(depth):
        lam_init = 0.8 - 0.6 * math.exp(-0.3 * l)
        fa, kva, qa, ha, new_caches = _inproj_calls(x, n, mod4, n1g, w_aq, w_rest, rope_tab, small, new_caches, l,
                                                    depth, n_ctx)
        oacd, ob, sf, sb = _ctx_mixers_call(fa, kva, qa, lamv, cong, wuv, bd, bong, tri, l, lam_init, n_ctx)
        oacd = _attn_lat_call(kva, qa, caches, lamv, cong, wuv, bd, tile, oacd, l, lam_init, n_ctx)
        ob = _gla_lat_call(fa, s0f_all, s0b_all, bong, bd, tri, ob, l, n_ctx)
        x1 = _merge_call(x, n, mod4, ha, w_gates, oacd, ob, wb, wout, l, n_ctx)
        res = _ffn_call(x1, mod4, n2g, wu, wg, cw, cb, wd, final_g[None], l, n_ctx, l == depth - 1)
        x = res[0]
        sfs.append(sf)
        sbs.append(sb)

    diag = lambda s: jnp.stack([s[:, :, hh * 64:(hh + 1) * 64, hh * 32:(hh + 1) * 32] for hh in range(4)],
                               axis=2).transpose(0, 1, 2, 4, 3)
    y_prompt, y_sample = res
    nak, nav, nck, ncv, nckv, nkr = new_caches
    return (y_prompt, y_sample, nak.reshape(nb, depth, SEQ, 2, 64), nav.reshape(nb, depth, SEQ, 2, 64),
            diag(jnp.stack(sfs, axis=1)), diag(jnp.stack(sbs, axis=1)), nck.reshape(nb, depth, SEQ, 4, 2, 32),
            ncv.reshape(nb, depth, SEQ, 4, 64), nckv, nkr)
```

```python
import functools
import math

import numpy as np
import jax
import jax.numpy as jnp
from jax import lax
from jax.experimental import pallas as pl
from jax.experimental.pallas import tpu as pltpu

F32 = jnp.float32
BF16 = jnp.bfloat16

D_MODEL = 1024
SEQ = 256
DEC_SEQ = 1024
PAST_LEN = 512
GRID_W = 64
ROPE_BASE = 10000.0
EPS = 1e-6
LOG2E = math.log2(math.e)
HEAD_W = 64
BRANCH_WIDTH = 256
GLA_KEY_W = 32
GLA_CHUNK = 64
GLA_SUB = 16
GLA_BLOCK = 256
GLA_GATE_NORM = 16.0
D_Q_RANK = 192
D_FF = 2816
FF_CHUNK = 256
N_MOD_ROWS = 16

ROW_TILE = 512
FFN_TILE = 1024
Q_TILE = 256
CTX_SEQ_PER_STEP = 4
LAT_SEQ_PER_GLA_STEP = 2
HALO = 16
VMEM_LIMIT = 56 * 1024 * 1024

IN_AQ_W = 256
IN_GATE_IN = (1280, 1312)
NZ = 2176
Z_AK, Z_AV, Z_BQ, Z_BK, Z_BV, Z_BR, Z_CQ, Z_CK, Z_CV, Z_DQ = 0, 128, 256, 384, 512, 768, 1024, 1280, 1536, 1792
GATES_OFF = 2432
F_W = 1024
F_BQ, F_BK, F_BV, F_BR, F_LGF, F_LGB = 0, 128, 256, 512, 768, 896
KV_W = 1024
KV_AK, KV_AV, KV_CK, KV_CV, KV_DKV, KV_KR = 0, 128, 256, 512, 768, 896
Q_W = 1152
Q_AQ, Q_CQ, Q_DLAT, Q_DQR = 0, 256, 512, 1024
ROPE_W = 768


def _dot(a, b):
    return jnp.dot(a, b, preferred_element_type=F32)


def _dot_nt(a, b):
    return lax.dot_general(a, b, (((1,), (1,)), ((), ())), preferred_element_type=F32)


def _dot_tn(a, b):
    return lax.dot_general(a, b, (((0,), (0,)), ((), ())), preferred_element_type=F32)


def _split3(a):
    hi = a.astype(BF16)
    r1 = a - hi.astype(F32)
    mid = r1.astype(BF16)
    lo = (r1 - mid.astype(F32)).astype(BF16)
    return hi, mid, lo


def _split_dot(a, b01):
    hi, mid, lo = _split3(a)
    return _dot(hi, b01) + _dot(mid, b01) + _dot(lo, b01)


def _split_dot_tn(b01, a):
    hi, mid, lo = _split3(a)
    return _dot(b01, hi) + _dot(b01, mid) + _dot(b01, lo)


def _group_rms(v, bd, width, gain):
    ss = _split_dot(v * v, bd)
    return v * lax.rsqrt(ss * (1.0 / width) + EPS) * gain


def _rope128(v, cos, sin_a, sin_b, shift):
    return v * cos + pltpu.roll(v, 128 - shift, 1) * sin_a + pltpu.roll(v, shift, 1) * sin_b


def _modulated_norm(x, gain, shift, scale):
    ms = jnp.mean(x * x, axis=-1, keepdims=True)
    return (x * lax.rsqrt(ms + EPS) * gain) * (1.0 + scale) + shift


def _lane_mask(width, lo, hi):
    lane = lax.broadcasted_iota(jnp.int32, (1, width), 1)
    return (lane >= lo) & (lane < hi)


def _const_spec(shape, grid_rank):
    zeros = tuple(0 for _ in shape)
    if grid_rank == 1:
        return pl.BlockSpec(shape, lambda i: zeros)
    return pl.BlockSpec(shape, lambda b, i: zeros)


def _layer_spec(shape, layer, grid_rank):
    idx = (layer,) + tuple(0 for _ in shape)
    once = pl.Buffered(1)
    if grid_rank == 1:
        return pl.BlockSpec((None,) + tuple(shape), lambda i: idx, pipeline_mode=once)
    return pl.BlockSpec((None,) + tuple(shape), lambda b, i: idx, pipeline_mode=once)


def _mod_kernel(cv_ref, w_ref, b_ref, o_ref):
    cv = cv_ref[...]
    a = cv * jax.nn.sigmoid(cv)
    o_ref[...] = _dot(a.astype(BF16), w_ref[...].astype(BF16)) + b_ref[...]


def _mod_call(cvec, w_mod, b_mod):
    depth = w_mod.shape[0]
    nblk = 4
    bw = 6 * D_MODEL // nblk
    return pl.pallas_call(
        _mod_kernel,
        grid=(depth, nblk),
        in_specs=[
            pl.BlockSpec((N_MOD_ROWS, D_MODEL), lambda l, j: (0, 0)),
            pl.BlockSpec((None, D_MODEL, bw), lambda l, j: (l, 0, j)),
            pl.BlockSpec((None, 1, bw), lambda l, j: (l, 0, j)),
        ],
        out_specs=pl.BlockSpec((None, N_MOD_ROWS, bw), lambda l, j: (l, 0, j)),
        out_shape=jax.ShapeDtypeStruct((depth, N_MOD_ROWS, 6 * D_MODEL), F32),
        compiler_params=pltpu.CompilerParams(vmem_limit_bytes=VMEM_LIMIT),
        name="mod_vectors",
    )(cvec, w_mod, b_mod.reshape(depth, 1, 6 * D_MODEL))


def _win_prep_kernel(w_ref, waq_ref, wrest_ref, wgates_ref):
    w = w_ref[...]
    c0, c1 = w[:, 0:128], w[:, 128:256]
    low = _lane_mask(128, 0, HEAD_W)
    waq_ref[:, 0:128] = jnp.where(low, c0, pltpu.roll(c1, HEAD_W, 1)).astype(BF16)
    waq_ref[:, 128:256] = jnp.where(low, pltpu.roll(c0, HEAD_W, 1), c1).astype(BF16)
    g0, g1 = IN_GATE_IN
    wrest_ref[:, 0:g0 - IN_AQ_W] = w[:, IN_AQ_W:g0].astype(BF16)
    slab = w[:, g0:GATES_OFF]
    wrest_ref[:, g0 - IN_AQ_W:NZ] = pltpu.roll(slab, slab.shape[1] - (g1 - g0), 1).astype(BF16)
    wgates_ref[...] = w[:, GATES_OFF:].astype(BF16)


def _win_prep_call(w_in):
    depth, d, width = w_in.shape
    rb = 256
    out = lambda w: pl.BlockSpec((None, rb, w), lambda l, i: (l, i, 0))
    return pl.pallas_call(
        _win_prep_kernel,
        grid=(depth, d // rb),
        in_specs=[pl.BlockSpec((None, rb, width), lambda l, i: (l, i, 0))],
        out_specs=[out(IN_AQ_W), out(NZ), out(width - GATES_OFF)],
        out_shape=[jax.ShapeDtypeStruct((depth, d, w), BF16) for w in (IN_AQ_W, NZ, width - GATES_OFF)],
        compiler_params=pltpu.CompilerParams(vmem_limit_bytes=VMEM_LIMIT),
        name="in_weight_layout",
    )(w_in)


def _inproj_body(x, mod_ref, n1g_ref, waq_ref, w_ref, rope_ref, aqg_ref, akg_ref, bd_ref, wgate_ref, bgate_ref,
                 dqg_ref, wuq_ref, wuk_ref, dkvg_ref, tile_ref, f_ref, kv_ref, q_ref, h_ref, cache_refs):
    mod = mod_ref[...]
    h = _modulated_norm(x, n1g_ref[...], mod[:, 0:D_MODEL], mod[:, D_MODEL:2 * D_MODEL]).astype(BF16)
    h_ref[...] = h
    z_aq = _dot(h, waq_ref[...])
    z = _dot(h, w_ref[...])
    zd = pltpu.roll(z[:, NZ - 256:NZ], 192, 1)
    bd = bd_ref[...]
    bd128 = bd[0:128, 0:128]
    if rope_ref is None:
        rope64 = rope32 = lambda v: v
    else:
        rope = rope_ref[...]
        cos64, sa64, sb64 = rope[:, 0:128], rope[:, 128:256], rope[:, 256:384]
        cos32, sa32, sb32 = rope[:, 384:512], rope[:, 512:640], rope[:, 640:768]
        rope64 = lambda v: _rope128(v, cos64, sa64, sb64, 16)
        rope32 = lambda v: _rope128(v, cos32, sa32, sb32, 8)

    def emit_kv(col, cache_idx, val, kv_val=None):
        kv_val = val if kv_val is None else kv_val
        kv_ref[:, col:col + kv_val.shape[1]] = kv_val.astype(BF16)
        if cache_refs is not None:
            ref = cache_refs[cache_idx]
            for j in range(val.shape[0] // SEQ):
                ref[j] = val[j * SEQ:(j + 1) * SEQ, 0:ref.shape[-1]]

    aq = _group_rms(z_aq, bd, HEAD_W, aqg_ref[...])
    a_scale = HEAD_W ** -0.5 * LOG2E
    q_ref[:, Q_AQ:Q_AQ + 128] = (rope64(aq[:, 0:128]) * a_scale).astype(BF16)
    q_ref[:, Q_AQ + 128:Q_AQ + 256] = (rope64(aq[:, 128:256]) * a_scale).astype(BF16)
    emit_kv(KV_AK, 0, rope64(_group_rms(z[:, Z_AK:Z_AK + 128], bd128, HEAD_W, akg_ref[...])))
    emit_kv(KV_AV, 1, z[:, Z_AV:Z_AV + 128])

    f_ref[:, F_BQ:F_BK] = z[:, Z_BQ:Z_BK] * (GLA_KEY_W ** -0.5)
    f_ref[:, F_BK:F_BV] = z[:, Z_BK:Z_BV]
    f_ref[:, F_BV:F_BR] = z[:, Z_BV:Z_BR]
    f_ref[:, F_BR:F_LGF] = z[:, Z_BR:Z_CQ]
    pre = _dot(zd[:, 128:256].astype(BF16), wgate_ref[...]) + bgate_ref[...]
    logsig = jnp.minimum(pre, 0.0) - jnp.log1p(jnp.exp(-jnp.abs(pre)))
    f_ref[:, F_LGF:F_W] = logsig / GLA_GATE_NORM

    c_scale = 32 ** -0.5 * LOG2E
    q_ref[:, Q_CQ:Q_CQ + 128] = (rope32(z[:, Z_CQ:Z_CQ + 128]) * c_scale).astype(BF16)
    q_ref[:, Q_CQ + 128:Q_CQ + 256] = (rope32(z[:, Z_CQ + 128:Z_CQ + 256]) * c_scale).astype(BF16)
    emit_kv(KV_CK, 2, jnp.concatenate([rope32(z[:, Z_CK:Z_CK + 128]), rope32(z[:, Z_CK + 128:Z_CK + 256])],
                                      axis=-1))
    emit_kv(KV_CV, 3, z[:, Z_CV:Z_DQ])

    dkv = zd[:, 0:128]
    ms = jnp.mean(dkv * dkv, axis=-1, keepdims=True)
    emit_kv(KV_DKV, 4, dkv * lax.rsqrt(ms + EPS) * dkvg_ref[...])
    kr = rope32(zd[:, 128:256])
    emit_kv(KV_KR, 5, kr, _dot(kr[:, 0:32].astype(BF16), tile_ref[...]))
    zq = jnp.where(_lane_mask(256, 0, D_Q_RANK), z[:, Z_DQ:Z_DQ + 256], 0.0)
    msq = jnp.sum(zq * zq, axis=-1, keepdims=True) * (1.0 / D_Q_RANK)
    qn = zq * lax.rsqrt(msq + EPS) * dqg_ref[...]
    dq = _dot(qn.astype(BF16), wuq_ref[...])
    d_scale = 96 ** -0.5 * LOG2E
    q_ref[:, Q_DLAT:Q_DQR] = (_dot(dq[:, 0:256].astype(BF16), wuk_ref[...]) * d_scale).astype(BF16)
    q_ref[:, Q_DQR:Q_W] = (rope32(dq[:, 256:384]) * d_scale).astype(BF16)


def _inproj_ctx_kernel(x_ref, mod_ref, n1g_ref, waq_ref, w_ref, aqg_ref, akg_ref, bd_ref, wgate_ref, bgate_ref,
                       dqg_ref, wuq_ref, wuk_ref, dkvg_ref, tile_ref, *rest):
    outs = rest[-10:]
    x = x_ref[...].reshape(-1, D_MODEL)
    _inproj_body(x, mod_ref, n1g_ref, waq_ref, w_ref, None, aqg_ref, akg_ref, bd_ref, wgate_ref, bgate_ref,
                 dqg_ref, wuq_ref, wuk_ref, dkvg_ref, tile_ref, outs[0], outs[1], outs[2], outs[3], outs[4:])


def _inproj_lat_kernel(x_ref, mod_ref, n1g_ref, waq_ref, w_ref, rope_ref, aqg_ref, akg_ref, bd_ref, wgate_ref,
                       bgate_ref, dqg_ref, wuq_ref, wuk_ref, dkvg_ref, tile_ref, pf_ref, pkv_ref, pq_ref, ph_ref,
                       f_ref, kv_ref, q_ref, h_ref):
    del pf_ref, pkv_ref, pq_ref, ph_ref
    _inproj_body(x_ref[...], mod_ref, n1g_ref, waq_ref, w_ref, rope_ref, aqg_ref, akg_ref, bd_ref, wgate_ref,
                 bgate_ref, dqg_ref, wuq_ref, wuk_ref, dkvg_ref, tile_ref, f_ref, kv_ref, q_ref, h_ref, None)


CACHE_WIDTHS = (128, 128, 256, 256, 128, 32)


def _inproj_calls(x, n, mod4, n1g, w_aq, w_rest, rope_tab, small, prev_caches, layer, depth, n_ctx_rows):
    tm = ROW_TILE
    ctx_tiles = n_ctx_rows // tm
    seq_tiles = DEC_SEQ // tm
    spt = tm // SEQ
    n_ctx_seq = n_ctx_rows // SEQ
    if isinstance(x, tuple):
        x_ctx, x_lat = x
        x_ctx_spec = pl.BlockSpec((spt, SEQ, D_MODEL), lambda i: (i, 0, 0))
        x_lat_spec = pl.BlockSpec((None, tm, D_MODEL), lambda i: (i // seq_tiles, i % seq_tiles, 0))
    else:
        x_ctx = x_lat = x
        x_ctx_spec = pl.BlockSpec((tm, D_MODEL), lambda i: (i, 0))
        x_lat_spec = pl.BlockSpec((tm, D_MODEL), lambda i: (i + ctx_tiles, 0))
    const = lambda shape: _layer_spec(shape, layer, 1)
    small_specs = [const((1, 256)), const((1, 128)), _const_spec((256, 256), 1), const((128, 256)),
                   const((1, 256)), const((1, 256)), const((256, 384)), const((256, 512)), const((1, 128)),
                   _const_spec((32, 128), 1)]
    act_shapes = [jax.ShapeDtypeStruct((n, F_W), F32), jax.ShapeDtypeStruct((n, KV_W), BF16),
                  jax.ShapeDtypeStruct((n, Q_W), BF16), jax.ShapeDtypeStruct((n, D_MODEL), BF16)]
    act_specs = lambda off: [pl.BlockSpec((tm, w), lambda i: (i + off, 0)) for w in (F_W, KV_W, Q_W, D_MODEL)]
    n_act = len(act_shapes)
    cparams = pltpu.CompilerParams(vmem_limit_bytes=VMEM_LIMIT)
    any_spec = pl.BlockSpec(memory_space=pl.ANY)

    n_in = 5 + len(small)
    alias = {} if prev_caches is None else {n_in + k: n_act + k for k in range(6)}
    res = pl.pallas_call(
        _inproj_ctx_kernel,
        grid=(ctx_tiles,),
        in_specs=[
            x_ctx_spec,
            pl.BlockSpec((None, None, 1, 6 * D_MODEL), lambda i: (layer, 0, 0, 0)),
            const((1, D_MODEL)),
            const((D_MODEL, IN_AQ_W)),
            const((D_MODEL, NZ)),
        ] + small_specs + ([] if prev_caches is None else [any_spec] * 6),
        out_specs=act_specs(0) + [pl.BlockSpec((spt, None, SEQ, w), lambda i: (i, layer, 0, 0))
                                  for w in CACHE_WIDTHS],
        out_shape=act_shapes + [jax.ShapeDtypeStruct((n_ctx_seq, depth, SEQ, w), F32) for w in CACHE_WIDTHS],
        input_output_aliases=alias,
        compiler_params=cparams,
        name="in_projection_context",
    )(x_ctx, mod4, n1g, w_aq, w_rest, *small, *([] if prev_caches is None else prev_caches))
    acts_c, caches = res[:n_act], tuple(res[n_act:])

    fa, kva, qa, ha = pl.pallas_call(
        _inproj_lat_kernel,
        grid=((n - n_ctx_rows) // tm,),
        in_specs=[
            x_lat_spec,
            pl.BlockSpec((None, None, 1, 6 * D_MODEL), lambda i: (layer, 1 + i // seq_tiles, 0, 0)),
            const((1, D_MODEL)),
            const((D_MODEL, IN_AQ_W)),
            const((D_MODEL, NZ)),
            pl.BlockSpec((tm, ROPE_W), lambda i: (i % seq_tiles, 0)),
        ] + small_specs + [any_spec] * n_act,
        out_specs=act_specs(ctx_tiles),
        out_shape=act_shapes,
        input_output_aliases={n_in + 1 + k: k for k in range(n_act)},
        compiler_params=cparams,
        name="in_projection_latent",
    )(x_lat, mod4, n1g, w_aq, w_rest, rope_tab, *small, *acts_c)
    return fa, kva, qa, ha, caches


def _exp2_parts(parts):
    m = parts[0].max(axis=-1, keepdims=True)
    for s in parts[1:]:
        m = jnp.maximum(m, s.max(axis=-1, keepdims=True))
    return [jnp.exp2(s - m) for s in parts]


def _value_variants(av, cv, dk):
    one = jnp.ones((), BF16)
    av1 = [jnp.where(_lane_mask(128, hh * HEAD_W, (hh + 1) * HEAD_W), av, one) for hh in range(2)]
    cv1 = [jnp.where(_lane_mask(256, hh * HEAD_W, (hh + 1) * HEAD_W), cv, one) for hh in range(4)]
    vlat = jnp.where(_lane_mask(256, 0, 128), dk, one)
    return av1, cv1, vlat


def _run(phases):
    for _ in phases:
        pass


def _interleave(*phase_generators):
    live = list(phase_generators)
    while live:
        live = [g for g in live if next(g, StopIteration) is not StopIteration]


def _attn_core(*args):
    _run(_attn_phases(*args))


def _attn_phases(q, ak, av1, ck, cv1, dk, vlat, lam_ref, cong_ref, wuv_ref, bd_ref, o_ref, lam_init):
    zero = jnp.zeros((), BF16)
    ak, ck, dk = [ak], [ck], [dk]
    tq = q.shape[0]

    masks_a = [_lane_mask(128, hh * HEAD_W, (hh + 1) * HEAD_W) for hh in range(2)]
    qa = jnp.concatenate([jnp.where(masks_a[hh], q[:, Q_AQ + g * 128:Q_AQ + (g + 1) * 128], zero)
                          for hh in range(2) for g in range(2)], axis=0)
    qcs = [jnp.concatenate([jnp.where(_lane_mask(128, u * 32, (u + 1) * 32),
                                      q[:, Q_CQ + c * 128:Q_CQ + (c + 1) * 128], zero) for u in range(4)], axis=0)
           for c in range(2)]
    qd = jnp.concatenate(
        [jnp.concatenate([q[:, Q_DLAT + hh * 128:Q_DLAT + (hh + 1) * 128],
                          jnp.where(_lane_mask(128, hh * 32, (hh + 1) * 32), q[:, Q_DQR:Q_W], zero)], axis=-1)
         for hh in range(4)], axis=0)
    s_a = [_dot_nt(qa, k) for k in ak]
    s_c = [[_dot_nt(qcs[c], k[:, c * 128:(c + 1) * 128]) for k in ck] for c in range(2)]
    yield

    ps = [p.astype(BF16) for p in _exp2_parts(s_a)]
    s_d = [_dot_nt(qd, k) for k in dk]
    oa = [None, None]
    for hh in range(2):
        rows = slice(hh * 2 * tq, (hh + 1) * 2 * tq)
        o = _dot(ps[0][rows], av1[hh])
        o = jnp.where(masks_a[hh], o / pltpu.roll(o, HEAD_W, 1), 0.0)
        for g in range(2):
            part = o[g * tq:(g + 1) * tq]
            oa[g] = part if oa[g] is None else oa[g] + part
    o_ref[:, 0:128] = oa[0]
    o_ref[:, 128:256] = oa[1]
    yield

    lv = lam_ref[...]
    lam = (jnp.exp(jnp.sum(lv[0:1] * lv[1:2], axis=-1, keepdims=True))
           - jnp.exp(jnp.sum(lv[2:3] * lv[3:4], axis=-1, keepdims=True)) + lam_init)
    oc = []
    for c in range(2):
        c0 = c * 128
        ps = [p.astype(BF16) for p in _exp2_parts(s_c[c])]
        och = None
        for h2 in range(2):
            hh = c * 2 + h2
            rows = slice(h2 * 2 * tq, (h2 + 1) * 2 * tq)
            o = _dot(ps[0][rows], cv1[hh])
            res = o[:, c0:c0 + 128] / o[:, 128 - c0:256 - c0]
            o = jnp.where(_lane_mask(128, h2 * HEAD_W, (h2 + 1) * HEAD_W), res[0:tq] - lam * res[tq:2 * tq], 0.0)
            och = o if och is None else och + o
            yield
        oc.append(och)
    oc = jnp.concatenate(oc, axis=-1)
    o_ref[:, 256:512] = _group_rms(oc, bd_ref[...], HEAD_W, cong_ref[...]) * (1.0 - lam_init)

    ps = [p.astype(BF16) for p in _exp2_parts(s_d)]
    o = _dot(ps[0], vlat)
    lat = (o[:, 0:128] / o[:, 128:256]).astype(BF16)
    o_ref[:, 512:768] = _dot(jnp.concatenate([lat[hh * tq:(hh + 1) * tq] for hh in range(4)], axis=-1),
                             wuv_ref[...])


def _attn_lat_kernel(q_ref, kva_ref, ck_ref, cv_ref, dkv_ref, cak_ref, cav_ref, cck_ref, ccv_ref,
                     cckv_ref, ckr_ref, lam_ref, cong_ref, wuv_ref, bd_ref, tile_ref, prev_ref, o_ref,
                     ak_s, ck_s, dk_s, av1_s, cv1_s, vlat_s, *, lam_init):
    del prev_ref

    @pl.when(pl.program_id(1) == 0)
    def _():
        p = PAST_LEN
        kva = kva_ref[...]
        kr4 = _dot(ckr_ref[...].astype(BF16), tile_ref[...]).astype(BF16)
        pieces = ((cak_ref[...].astype(BF16), cav_ref[...].astype(BF16), cck_ref[...].astype(BF16),
                   ccv_ref[...].astype(BF16), jnp.concatenate([cckv_ref[...].astype(BF16), kr4], axis=-1)),
                  (kva[:, 0:128], kva[:, 128:256], ck_ref[...], cv_ref[...], dkv_ref[...]))
        for rows, (ak, av, ck, cv, dk) in zip((slice(0, p), slice(p, p + DEC_SEQ)), pieces):
            av1, cv1, vlat = _value_variants(av, cv, dk)
            ak_s[rows] = ak
            ck_s[rows] = ck
            dk_s[rows] = dk
            vlat_s[rows] = vlat
            for hh in range(2):
                av1_s[hh, rows] = av1[hh]
            for hh in range(4):
                cv1_s[hh, rows] = cv1[hh]

    _attn_core(q_ref[...], ak_s[...], [av1_s[hh] for hh in range(2)], ck_s[...],
               [cv1_s[hh] for hh in range(4)], dk_s[...], vlat_s[...], lam_ref, cong_ref, wuv_ref, bd_ref,
               o_ref, lam_init)


def _attn_lat_call(kva, qa, caches, lamv, cong, wuv, bd, tile, o_ctx, layer, lam_init, n_ctx_rows):
    n = kva.shape[0]
    n_lat_seq = (n - n_ctx_rows) // DEC_SEQ
    param_specs = lambda rank: [_layer_spec((4, 32), layer, rank), _layer_spec((1, 256), layer, rank),
                                _layer_spec((512, 256), layer, rank), _const_spec((256, 256), rank),
                                _const_spec((32, 128), rank)]
    qt = Q_TILE
    q_per_seq = DEC_SEQ // qt
    q0 = n_ctx_rows // qt
    s0 = n_ctx_rows // DEC_SEQ
    cak, cav, cck, ccv, cckv, ckr = caches
    cache_spec = lambda w: pl.BlockSpec((None, None, PAST_LEN, w), lambda b, i: (b, layer, 0, 0))
    seq_cols = lambda idx: pl.BlockSpec((DEC_SEQ, 256), lambda b, i: (s0 + b, idx))
    n_keys = PAST_LEN + DEC_SEQ
    return pl.pallas_call(
        functools.partial(_attn_lat_kernel, lam_init=lam_init),
        grid=(n_lat_seq, q_per_seq),
        in_specs=[
            pl.BlockSpec((qt, Q_W), lambda b, i: (q0 + b * q_per_seq + i, 0)),
            seq_cols(KV_AK // 256), seq_cols(KV_CK // 256), seq_cols(KV_CV // 256), seq_cols(KV_DKV // 256),
            cache_spec(128), cache_spec(128), cache_spec(256), cache_spec(256), cache_spec(128),
            cache_spec(32),
        ] + param_specs(2) + [pl.BlockSpec(memory_space=pl.ANY)],
        out_specs=pl.BlockSpec((qt, 3 * BRANCH_WIDTH), lambda b, i: (q0 + b * q_per_seq + i, 0)),
        out_shape=jax.ShapeDtypeStruct((n, 3 * BRANCH_WIDTH), F32),
        scratch_shapes=[pltpu.VMEM((n_keys, 128), BF16), pltpu.VMEM((n_keys, 256), BF16),
                        pltpu.VMEM((n_keys, 256), BF16), pltpu.VMEM((2, n_keys, 128), BF16),
                        pltpu.VMEM((4, n_keys, 256), BF16), pltpu.VMEM((n_keys, 256), BF16)],
        input_output_aliases={16: 0},
        compiler_params=pltpu.CompilerParams(vmem_limit_bytes=VMEM_LIMIT),
        name="attention_latent",
    )(qa, kva, kva, kva, kva, cak, cav, cck, ccv, cckv, ckr, lamv, cong, wuv, bd, tile, o_ctx)


def _gla_chunk(q, k, v, bc, st, reverse, masks, result):
    c, sub, nsub = GLA_CHUNK, GLA_SUB, GLA_CHUNK // GLA_SUB
    keep, qmask, src_row, blk = masks
    edge = bc[0:1] if reverse else bc[c - 1:c]
    qs, kd = [], []
    for i in range(nsub):
        r0 = i * sub
        ref = bc[r0 + sub - 1:r0 + sub] if reverse else bc[r0:r0 + 1]
        qd = q[r0:r0 + sub] * jnp.exp(bc[r0:r0 + sub] - ref)
        qs.append(jnp.where(qmask, jnp.concatenate([qd] * 4, axis=0), 0.0))
        in_range = (src_row >= r0) if reverse else (src_row < r0 + sub)
        kd.append(k * jnp.exp(jnp.where(in_range, ref - bc, 0.0)))
    att = _dot_nt(jnp.concatenate(qs, axis=0).astype(BF16), jnp.concatenate(kd, axis=0).astype(BF16))
    vb = v.astype(BF16)
    q_dec = q * jnp.exp(bc)
    k_dec = k * jnp.exp(edge - bc)
    o_inter = _dot_nt(q_dec.astype(BF16), st.astype(BF16))
    upd = _dot_tn(vb, k_dec.astype(BF16))
    yield
    att = jnp.where(keep, att, 0.0).astype(BF16)
    pv = _dot(att, jnp.concatenate([vb] * nsub, axis=0))
    st_new = st * jnp.exp(edge) + jnp.where(blk, upd, 0.0)
    yield
    out_blocks = []
    for i in range(nsub):
        o = None
        for hh in range(4):
            r0 = i * c + hh * sub
            part = jnp.where(_lane_mask(256, hh * HEAD_W, (hh + 1) * HEAD_W), pv[r0:r0 + sub], 0.0)
            o = part if o is None else o + part
        out_blocks.append(o)
    o_intra = jnp.concatenate(out_blocks, axis=0)
    result.extend([o_intra + o_inter, st_new])


def _gla_masks(reverse):
    c, sub = GLA_CHUNK, GLA_SUB
    row = lax.broadcasted_iota(jnp.int32, (4 * c, 4 * c), 0)
    col = lax.broadcasted_iota(jnp.int32, (4 * c, 4 * c), 1)
    t = (row // c) * sub + row % sub
    s = col % c
    keep = (row // c == col // c) & ((s >= t) if reverse else (s <= t))
    qmask = (lax.broadcasted_iota(jnp.int32, (4 * sub, 128), 0) // sub
             == lax.broadcasted_iota(jnp.int32, (4 * sub, 128), 1) // GLA_KEY_W)
    src_row = lax.broadcasted_iota(jnp.int32, (c, 1), 0)
    blk = (lax.broadcasted_iota(jnp.int32, (256, 128), 0) // HEAD_W
           == lax.broadcasted_iota(jnp.int32, (256, 128), 1) // GLA_KEY_W)
    return keep, qmask, src_row, blk


def _gla_parts(qk_ref, v_ref, r_ref, lg_ref, g_ref, bd_ref, tri_ref, o_ref, accf_ref, accb_ref, bcf_ref, bcb_ref,
               seq_len):
    nc = seq_len // GLA_CHUNK
    masks_f, masks_b = _gla_masks(False), _gla_masks(True)

    def decays():
        for r0 in range(0, seq_len, GLA_BLOCK):
            lg = lg_ref[r0:r0 + GLA_BLOCK, :]
            for d, bc_ref in ((0, bcf_ref), (1, bcb_ref)):
                res = _dot(tri_ref[d], jnp.concatenate(_split3(lg[:, d * 128:(d + 1) * 128]), axis=-1))
                bc_ref[r0:r0 + GLA_BLOCK, :] = res[:, 0:128] + res[:, 128:256] + res[:, 256:384]

    def chunk_rows(c):
        start = c * GLA_CHUNK
        return pl.ds(start if isinstance(c, int) else pl.multiple_of(start, GLA_CHUNK), GLA_CHUNK)

    def pair_phases(ci, carry, new_carry):
        rf, rb = chunk_rows(ci), chunk_rows(nc - 1 - ci)
        res_f, res_b = [], []
        live = [_gla_chunk(qk_ref[rf, 0:128], qk_ref[rf, 128:256], v_ref[rf, :], bcf_ref[rf, :], carry[0], False,
                           masks_f, res_f),
                _gla_chunk(qk_ref[rb, 0:128], qk_ref[rb, 128:256], v_ref[rb, :], bcb_ref[rb, :], carry[1], True,
                           masks_b, res_b)]
        while live:
            live = [g for g in live if next(g, StopIteration) is not StopIteration]
            yield
        accf_ref[rf, :] = res_f[0]
        accb_ref[rb, :] = res_b[0]
        new_carry[:] = [res_f[1], res_b[1]]

    def finish():
        r = r_ref[...]
        o_ref[...] = (_group_rms(accf_ref[...] + accb_ref[...], bd_ref[...], HEAD_W, g_ref[...])
                      * (r * jax.nn.sigmoid(r)))

    return decays, pair_phases, finish


def _gla_seq_phases(parts, n_pairs, carry, states):
    decays, pair_phases, finish = parts
    decays()
    yield
    for ci in range(n_pairs):
        nxt = []
        yield from pair_phases(ci, carry, nxt)
        carry = tuple(nxt)
    finish()
    states.extend(carry)


def _ctx_mixers_kernel(q_ref, kva_ref, ck_ref, cv_ref, dkv_ref, qk_ref, v_ref, r_ref, lg_ref, lam_ref, cong_ref,
                       wuv_ref, bd_ref, bong_ref, tri_ref, o_ref, ob_ref, sf_ref, sb_ref, *scratch, lam_init):
    zero = jnp.zeros((256, 128), F32)
    nseq = q_ref.shape[0] // SEQ
    gens, states = [], [[] for _ in range(nseq)]
    for j in range(nseq):
        rows = slice(j * SEQ, (j + 1) * SEQ)
        kva = kva_ref[rows, :]
        dk = dkv_ref[rows, :]
        av1, cv1, vlat = _value_variants(kva[:, 128:256], cv_ref[rows, :], dk)
        gens.append(_attn_phases(q_ref[rows, :], kva[:, 0:128], av1, ck_ref[rows, :], cv1, dk, vlat, lam_ref,
                                 cong_ref, wuv_ref, bd_ref, o_ref.at[rows, :], lam_init))
        parts = _gla_parts(qk_ref.at[rows, :], v_ref.at[rows, :], r_ref.at[rows, :], lg_ref.at[rows, :], bong_ref,
                           bd_ref, tri_ref, ob_ref.at[rows, :], *[s.at[rows, :] for s in scratch], SEQ)
        gens.append(_gla_seq_phases(parts, SEQ // GLA_CHUNK, (zero, zero), states[j]))
    _interleave(*gens)
    for j in range(nseq):
        sf_ref[j], sb_ref[j] = states[j]


def _ctx_mixers_call(fa, kva, qa, lamv, cong, wuv, bd, bong, tri, layer, lam_init, n_ctx_rows):
    n = fa.shape[0]
    n_ctx_seq = n_ctx_rows // SEQ
    nseq = CTX_SEQ_PER_STEP
    crows = nseq * SEQ
    const = lambda shape: _const_spec(shape, 1)
    lspec = lambda shape: _layer_spec(shape, layer, 1)
    col = lambda w, off: pl.BlockSpec((crows, w), lambda b: (b, off // w))
    st_shape = jax.ShapeDtypeStruct((n_ctx_seq, 256, 128), F32)
    st_spec = pl.BlockSpec((nseq, 256, 128), lambda b: (b, 0, 0))
    return pl.pallas_call(
        functools.partial(_ctx_mixers_kernel, lam_init=lam_init),
        grid=(n_ctx_seq // nseq,),
        in_specs=[
            col(Q_W, 0), col(256, KV_AK), col(256, KV_CK), col(256, KV_CV), col(256, KV_DKV),
            col(256, F_BQ), col(256, F_BV), col(256, F_BR), col(256, F_LGF),
            lspec((4, 32)), lspec((1, 256)), lspec((512, 256)), const((256, 256)), lspec((1, 256)),
            const((2, GLA_BLOCK, GLA_BLOCK)),
        ],
        out_specs=[col(3 * BRANCH_WIDTH, 0), col(BRANCH_WIDTH, 0), st_spec, st_spec],
        out_shape=[jax.ShapeDtypeStruct((n, 3 * BRANCH_WIDTH), F32), jax.ShapeDtypeStruct((n, BRANCH_WIDTH), F32),
                   st_shape, st_shape],
        scratch_shapes=[pltpu.VMEM((crows, BRANCH_WIDTH), F32), pltpu.VMEM((crows, BRANCH_WIDTH), F32),
                        pltpu.VMEM((crows, 128), F32), pltpu.VMEM((crows, 128), F32)],
        compiler_params=pltpu.CompilerParams(vmem_limit_bytes=VMEM_LIMIT),
        name="mixers_context",
    )(qa, kva, kva, kva, kva, fa, fa, fa, fa, lamv, cong, wuv, bd, bong, tri)


def _gla_lat_kernel(qk_ref, v_ref, r_ref, lg_ref, s0f_ref, s0b_ref, g_ref, bd_ref, tri_ref, prev_ref,
                    o_ref, *scratch):
    del prev_ref
    nseq = qk_ref.shape[0] // DEC_SEQ
    parts = []
    for j in range(nseq):
        rows = slice(j * DEC_SEQ, (j + 1) * DEC_SEQ)
        parts.append(_gla_parts(qk_ref.at[rows, :], v_ref.at[rows, :], r_ref.at[rows, :], lg_ref.at[rows, :], g_ref,
                                bd_ref, tri_ref, o_ref.at[rows, :], *[s.at[rows, :] for s in scratch], DEC_SEQ))
    for decays, _, _ in parts:
        decays()

    def step(ci, carry):
        nxt = [[] for _ in range(nseq)]
        _interleave(*[parts[j][1](ci, carry[2 * j:2 * j + 2], nxt[j]) for j in range(nseq)])
        return tuple(s for pair in nxt for s in pair)

    init = tuple(s for j in range(nseq) for s in (s0f_ref[j], s0b_ref[j]))
    lax.fori_loop(0, DEC_SEQ // GLA_CHUNK, step, init)
    for _, _, finish in parts:
        finish()


def _gla_lat_call(fa, s0f_t, s0b_t, bong, bd, tri, o_ctx, layer, n_ctx_rows):
    n = fa.shape[0]
    n_lat_seq = (n - n_ctx_rows) // DEC_SEQ
    const = lambda shape: _const_spec(shape, 1)
    nseq = LAT_SEQ_PER_GLA_STEP
    rows = nseq * DEC_SEQ
    state_spec = pl.BlockSpec((nseq, None, 256, 128), lambda b: (b, layer, 0, 0))
    s0 = n_ctx_rows // rows
    seq_cols = lambda off: pl.BlockSpec((rows, 256), lambda b: (s0 + b, off // 256))
    return pl.pallas_call(
        _gla_lat_kernel,
        grid=(n_lat_seq // nseq,),
        in_specs=[
            seq_cols(F_BQ), seq_cols(F_BV), seq_cols(F_BR), seq_cols(F_LGF),
            state_spec, state_spec,
            _layer_spec((1, 256), layer, 1), const((256, 256)), const((2, GLA_BLOCK, GLA_BLOCK)),
            pl.BlockSpec(memory_space=pl.ANY),
        ],
        out_specs=pl.BlockSpec((rows, BRANCH_WIDTH), lambda b: (s0 + b, 0)),
        out_shape=jax.ShapeDtypeStruct((n, BRANCH_WIDTH), F32),
        scratch_shapes=[pltpu.VMEM((rows, BRANCH_WIDTH), F32), pltpu.VMEM((rows, BRANCH_WIDTH), F32),
                        pltpu.VMEM((rows, 128), F32), pltpu.VMEM((rows, 128), F32)],
        input_output_aliases={9: 0},
        compiler_params=pltpu.CompilerParams(vmem_limit_bytes=VMEM_LIMIT),
        name="gla_latent",
    )(fa, fa, fa, fa, s0f_t, s0b_t, bong, bd, tri, o_ctx)


def _merge_kernel(*refs, ctx_tiles):
    mod_ref, h_ref, wg_ref, oacd_ref, ob_ref, wb_ref, wout_ref, o_ref = refs[-8:]
    if len(refs) == 9:
        x = refs[0][...]
    else:
        x = jnp.where(pl.program_id(0) < ctx_tiles, refs[0][...].reshape(-1, D_MODEL), refs[1][...])
    mod = mod_ref[...]
    h = h_ref[...]
    oacd = oacd_ref[...]
    branches = [oacd[:, 0:256], ob_ref[...], oacd[:, 256:512], oacd[:, 512:768]]
    acc = None
    for j in range(4):
        gate = jax.nn.sigmoid(_dot(h, wg_ref[:, j * D_MODEL:(j + 1) * D_MODEL]))
        term = gate * _dot(branches[j].astype(BF16), wb_ref[j])
        acc = term if acc is None else acc + term
    mix = _dot(acc.astype(BF16), wout_ref[...])
    o_ref[...] = x + mod[:, 2 * D_MODEL:3 * D_MODEL] * mix


def _mod_map(layer, ctx_tiles, seq_tiles):
    return lambda i: (layer, jnp.where(i < ctx_tiles, 0, 1 + (i - ctx_tiles) // seq_tiles), 0, 0)


def _merge_call(x, n, mod4, ha, wgates, oacd, ob, wb, wout, layer, n_ctx_rows):
    tm = ROW_TILE
    ctx_tiles = n_ctx_rows // tm
    seq_tiles = DEC_SEQ // tm
    const = lambda shape: _layer_spec(shape, layer, 1)
    if isinstance(x, tuple):
        xs = list(x)
        x_specs = [
            pl.BlockSpec((tm // SEQ, SEQ, D_MODEL), lambda i: (jnp.minimum(i, ctx_tiles - 1), 0, 0)),
            pl.BlockSpec((None, tm, D_MODEL), lambda i: (jnp.maximum(i - ctx_tiles, 0) // seq_tiles,
                                                         jnp.maximum(i - ctx_tiles, 0) % seq_tiles, 0)),
        ]
    else:
        xs = [x]
        x_specs = [pl.BlockSpec((tm, D_MODEL), lambda i: (i, 0))]
    return pl.pallas_call(
        functools.partial(_merge_kernel, ctx_tiles=ctx_tiles),
        grid=(n // tm,),
        in_specs=x_specs + [
            pl.BlockSpec((None, None, 1, 6 * D_MODEL), _mod_map(layer, ctx_tiles, seq_tiles)),
            pl.BlockSpec((tm, D_MODEL), lambda i: (i, 0)),
            const((D_MODEL, 4 * D_MODEL)),
            pl.BlockSpec((tm, 3 * BRANCH_WIDTH), lambda i: (i, 0)),
            pl.BlockSpec((tm, BRANCH_WIDTH), lambda i: (i, 0)),
            const((4, BRANCH_WIDTH, D_MODEL)),
            const((D_MODEL, D_MODEL)),
        ],
        out_specs=pl.BlockSpec((tm, D_MODEL), lambda i: (i, 0)),
        out_shape=jax.ShapeDtypeStruct((n, D_MODEL), F32),
        compiler_params=pltpu.CompilerParams(vmem_limit_bytes=VMEM_LIMIT),
        name="merge",
    )(*xs, mod4, ha, wgates, oacd, ob, wb, wout)


def _ffn_kernel(x_ref, xp_ref, xn_ref, mod_ref, n2g_ref, wu_ref, wg_ref, cw_ref, cb_ref, wd_ref, fg_ref,
                *rest, ctx_tiles, final):
    out_refs, (hext_ref, act_ref) = rest[:-2], rest[-2:]
    tm = x_ref.shape[0]
    i = pl.program_id(0)
    x = x_ref[...]
    mod = mod_ref[...]
    sh2, sc2, g2 = mod[:, 3 * D_MODEL:4 * D_MODEL], mod[:, 4 * D_MODEL:5 * D_MODEL], mod[:, 5 * D_MODEL:]
    n2g = n2g_ref[...]
    hext_ref[0:HALO] = _modulated_norm(xp_ref[...], n2g, sh2, sc2).astype(BF16)
    hext_ref[HALO:HALO + tm] = _modulated_norm(x, n2g, sh2, sc2).astype(BF16)
    hext_ref[HALO + tm:] = _modulated_norm(xn_ref[...], n2g, sh2, sc2).astype(BF16)
    seq_len = jnp.where(i < ctx_tiles, SEQ, DEC_SEQ)
    row = lax.broadcasted_iota(jnp.int32, (tm, 1), 0)
    pos = (row + i * tm) & (seq_len - 1)
    first = pos == 0
    last = pos == seq_len - 1
    for c0 in range(0, D_FF, FF_CHUNK):
        u_ext = _dot(hext_ref[...], wu_ref[:, c0:c0 + FF_CHUNK])
        u = u_ext[HALO:HALO + tm]
        u_dn = jnp.where(first, 0.0, u_ext[HALO - 1:HALO - 1 + tm])
        u_up = jnp.where(last, 0.0, u_ext[HALO + 1:HALO + 1 + tm])
        cw = cw_ref[:, c0:c0 + FF_CHUNK]
        uc = u_dn * cw[0:1] + u * cw[1:2] + u_up * cw[2:3] + cb_ref[:, c0:c0 + FF_CHUNK]
        g = _dot(hext_ref[HALO:HALO + tm], wg_ref[:, c0:c0 + FF_CHUNK])
        act_ref[:, c0:c0 + FF_CHUNK] = (jax.nn.gelu(uc) * g).astype(BF16)
    out = x + g2 * _dot(act_ref[...], wd_ref[...])
    if final:
        yp_ref, ys_ref = out_refs
        ms = jnp.mean(out * out, axis=-1, keepdims=True)
        y = out * lax.rsqrt(ms + EPS) * fg_ref[...]

        @pl.when(i < ctx_tiles)
        def _():
            for j in range(tm // SEQ):
                yp_ref[j] = y[j * SEQ:(j + 1) * SEQ]

        @pl.when(i >= ctx_tiles)
        def _():
            ys_ref[...] = y
    else:
        out_refs[0][...] = out


def _ffn_call(x, mod4, n2g, wu, wg, cw, cb, wd, fg, layer, n_ctx_rows, final):
    n = x.shape[0]
    tm = FFN_TILE
    ctx_tiles = n_ctx_rows // tm
    seq_tiles = DEC_SEQ // tm
    hb = tm // HALO
    last_halo = n // HALO - 1
    lspec = lambda shape: _layer_spec(shape, layer, 1)
    if final:
        spt = tm // SEQ
        out_specs = [
            pl.BlockSpec((spt, SEQ, D_MODEL), lambda i: (jnp.minimum(i, ctx_tiles - 1), 0, 0)),
            pl.BlockSpec((None, tm, D_MODEL), lambda i: (jnp.maximum(i - ctx_tiles, 0) // seq_tiles,
                                                         jnp.maximum(i - ctx_tiles, 0) % seq_tiles, 0)),
        ]
        out_shape = [jax.ShapeDtypeStruct((n_ctx_rows // SEQ, SEQ, D_MODEL), F32),
                     jax.ShapeDtypeStruct(((n - n_ctx_rows) // DEC_SEQ, DEC_SEQ, D_MODEL), F32)]
    else:
        out_specs = [pl.BlockSpec((tm, D_MODEL), lambda i: (i, 0))]
        out_shape = [jax.ShapeDtypeStruct((n, D_MODEL), F32)]
    return pl.pallas_call(
        functools.partial(_ffn_kernel, ctx_tiles=ctx_tiles, final=final),
        grid=(n // tm,),
        in_specs=[
            pl.BlockSpec((tm, D_MODEL), lambda i: (i, 0)),
            pl.BlockSpec((HALO, D_MODEL), lambda i: (jnp.maximum(i * hb - 1, 0), 0)),
            pl.BlockSpec((HALO, D_MODEL), lambda i: (jnp.minimum((i + 1) * hb, last_halo), 0)),
            pl.BlockSpec((None, None, 1, 6 * D_MODEL), _mod_map(layer, ctx_tiles, seq_tiles)),
            lspec((1, D_MODEL)),
            lspec((D_MODEL, D_FF)), lspec((D_MODEL, D_FF)), lspec((8, D_FF)), lspec((1, D_FF)),
            lspec((D_FF, D_MODEL)), _const_spec((1, D_MODEL), 1),
        ],
        out_specs=out_specs,
        out_shape=out_shape,
        scratch_shapes=[pltpu.VMEM((tm + 2 * HALO, D_MODEL), BF16), pltpu.VMEM((tm, D_FF), BF16)],
        compiler_params=pltpu.CompilerParams(vmem_limit_bytes=VMEM_LIMIT),
        name="conv_ffn",
    )(x, x, x, mod4, n2g, wu, wg, cw, cb, wd, fg)


def _rope_tables():
    t = np.arange(DEC_SEQ)
    row = (t // GRID_W).astype(np.float32)
    col = (t % GRID_W).astype(np.float32)

    def tables(head_dim):
        r = head_dim // 2
        half = r // 2
        inv = (np.float32(ROPE_BASE) ** (-np.arange(half, dtype=np.float32) / np.float32(half))).astype(np.float32)
        lane = np.arange(128) % head_dim
        use_col = lane >= r
        e = lane % r
        first_half = e < half
        freq = inv[e % half]
        pos = np.where(use_col[None, :], col[:, None], row[:, None]).astype(np.float32)
        ang = (pos * freq[None, :]).astype(np.float32)
        cos = np.cos(ang).astype(np.float32)
        sin = np.sin(ang).astype(np.float32)
        sin_a = np.where(first_half[None, :], -sin, 0.0).astype(np.float32)
        sin_b = np.where(first_half[None, :], 0.0, sin).astype(np.float32)
        return cos, sin_a, sin_b

    return jnp.asarray(np.concatenate(list(tables(64)) + list(tables(32)), axis=1))


def kernel(x_prompt, x_sample, c, cache_a_k, cache_a_v, state_b_fwd, state_b_bwd, cache_c_k, cache_c_v, cache_d_ckv, cache_d_krope, c_ctx, w_mod, b_mod, norm1_g, norm2_g, w_in, a_qnorm_g, a_knorm_g, b_gate_w_fwd, b_gate_b_fwd, b_gate_w_bwd, b_gate_b_bwd, b_onorm_g, c_lq1, c_lk1, c_lq2, c_lk2, c_onorm_g, d_qnorm_g, d_w_uq, d_kvnorm_g, d_w_ukv, w_branch, w_out, w_ffu, w_ffg, conv_w, conv_b, w_ffd, final_g):
    depth = w_in.shape[0]
    nb, nd = x_prompt.shape[0], x_sample.shape[0]
    n_ctx = nb * SEQ
    assert n_ctx % (LAT_SEQ_PER_GLA_STEP * DEC_SEQ) == 0 and nd % LAT_SEQ_PER_GLA_STEP == 0
    assert nb % CTX_SEQ_PER_STEP == 0 and n_ctx % FFN_TILE == 0 and nd + 1 <= N_MOD_ROWS

    n = n_ctx + nd * DEC_SEQ
    cvec = jnp.zeros((N_MOD_ROWS, D_MODEL), F32).at[0].set(c_ctx).at[1:1 + nd].set(c)
    mod4 = _mod_call(cvec, w_mod, b_mod).reshape(depth, N_MOD_ROWS, 1, 6 * D_MODEL)

    rope_tab = _rope_tables()
    lane = np.arange(256)
    bd = jnp.asarray(lane[:, None] // HEAD_W == lane[None, :] // HEAD_W, BF16)
    tile = jnp.asarray(np.arange(32)[:, None] == np.arange(128)[None, :] % 32, BF16)
    tt = np.arange(GLA_BLOCK)
    same_chunk = tt[:, None] // GLA_CHUNK == tt[None, :] // GLA_CHUNK
    tri = jnp.asarray(np.stack([same_chunk & (tt[:, None] >= tt[None, :]),
                                same_chunk & (tt[:, None] <= tt[None, :])]), BF16)

    w_aq, w_rest, w_gates = _win_prep_call(w_in)
    wgate = jnp.zeros((depth, 128, 256), F32)
    wgate = wgate.at[:, 32:48, 0:128].set(b_gate_w_fwd).at[:, 48:64, 128:256].set(b_gate_w_bwd).astype(BF16)
    bgate = jnp.concatenate([b_gate_b_fwd, b_gate_b_bwd], axis=-1)[:, None, :]
    dqg = jnp.pad(d_qnorm_g, ((0, 0), (0, 256 - D_Q_RANK)))[:, None, :]
    uq = d_w_uq.reshape(depth, D_Q_RANK, 4, 96)
    wuq = jnp.concatenate([uq[..., :64].reshape(depth, D_Q_RANK, 256), uq[..., 64:].reshape(depth, D_Q_RANK, 128)], axis=-1)
    wuq = jnp.pad(wuq, ((0, 0), (0, 256 - D_Q_RANK), (0, 0))).astype(BF16)
    ukv = d_w_ukv.reshape(depth, 128, 4, 128)
    eye4 = jnp.eye(4, dtype=F32)
    wuk = (ukv[..., :64].transpose(0, 2, 3, 1)[:, :, :, None, :] * eye4[None, :, None, :, None])
    wuk = wuk.reshape(depth, 256, 512).astype(BF16)
    wuv = (ukv[..., 64:].transpose(0, 2, 1, 3)[:, :, :, None, :] * eye4[None, :, None, :, None])
    wuv = wuv.reshape(depth, 512, 256).astype(BF16)
    wb0 = w_branch[:, 0].reshape(depth, 2, 2, 64, D_MODEL).transpose(0, 2, 1, 3, 4).reshape(depth, 256, D_MODEL)
    wb = jnp.concatenate([wb0[:, None], w_branch[:, 1:]], axis=1).astype(BF16)
    wout = w_out.astype(BF16)
    wu, wg, wd = w_ffu.astype(BF16), w_ffg.astype(BF16), w_ffd.astype(BF16)
    cw = jnp.pad(conv_w, ((0, 0), (0, 8 - conv_w.shape[1]), (0, 0)))
    aqg = jnp.tile(a_qnorm_g, (1, 4))[:, None, :]
    akg = jnp.tile(a_knorm_g, (1, 2))[:, None, :]
    bong = jnp.tile(b_onorm_g, (1, 4))[:, None, :]
    cong = jnp.tile(c_onorm_g, (1, 4))[:, None, :]
    lamv = jnp.stack([c_lq1, c_lk1, c_lq2, c_lk2], axis=1)
    caches = (cache_a_k.reshape(nd, depth, PAST_LEN, 128), cache_a_v.reshape(nd, depth, PAST_LEN, 128),
              cache_c_k.reshape(nd, depth, PAST_LEN, 256), cache_c_v.reshape(nd, depth, PAST_LEN, 256),
              cache_d_ckv, cache_d_krope)
    expand = lambda s: (s.transpose(0, 1, 2, 4, 3)[:, :, :, :, None, :]
                        * eye4[None, None, :, None, :, None]).reshape(nd, depth, 256, 128)
    s0f_all, s0b_all = expand(state_b_fwd), expand(state_b_bwd)

    n1g, n2g = norm1_g[:, None, :], norm2_g[:, None, :]
    small = (aqg, akg, bd, wgate, bgate, dqg, wuq, wuk, d_kvnorm_g[:, None, :], tile)
    cb = conv_b[:, None, :]
    new_caches = None
    sfs, sbs = [], []
    x = (x_prompt, x_sample)
    for l in range(depth):
        lam_init = 0.8 - 0.6 * math.exp(-0.3 * l)
        fa, kva, qa, ha, new_caches = _inproj_calls(x, n, mod4, n1g, w_aq, w_rest, rope_tab, small, new_caches, l,
                                                    depth, n_ctx)
        oacd, ob, sf, sb = _ctx_mixers_call(fa, kva, qa, lamv, cong, wuv, bd, bong, tri, l, lam_init, n_ctx)
        oacd = _attn_lat_call(kva, qa, caches, lamv, cong, wuv, bd, tile, oacd, l, lam_init, n_ctx)
        ob = _gla_lat_call(fa, s0f_all, s0b_all, bong, bd, tri, ob, l, n_ctx)
        x1 = _merge_call(x, n, mod4, ha, w_gates, oacd, ob, wb, wout, l, n_ctx)
        res = _ffn_call(x1, mod4, n2g, wu, wg, cw, cb, wd, final_g[None], l, n_ctx, l == depth - 1)
        x = res[0]
        sfs.append(sf)
        sbs.append(sb)

    diag = lambda s: jnp.stack([s[:, :, hh * 64:(hh + 1) * 64, hh * 32:(hh + 1) * 32] for hh in range(4)],
                               axis=2).transpose(0, 1, 2, 4, 3)
    y_prompt, y_sample = res
    nak, nav, nck, ncv, nckv, nkr = new_caches
    return (y_prompt, y_sample, nak.reshape(nb, depth, SEQ, 2, 64), nav.reshape(nb, depth, SEQ, 2, 64),
            diag(jnp.stack(sfs, axis=1)), diag(jnp.stack(sbs, axis=1)), nck.reshape(nb, depth, SEQ, 4, 2, 32),
            ncv.reshape(nb, depth, SEQ, 4, 64), nckv, nkr)
```

```python
import functools
import math

import numpy as np
import jax
import jax.numpy as jnp
from jax import lax
from jax.experimental import pallas as pl
from jax.experimental.pallas import tpu as pltpu

F32 = jnp.float32
BF16 = jnp.bfloat16

D_MODEL = 1024
SEQ = 256
DEC_SEQ = 1024
PAST_LEN = 512
GRID_W = 64
ROPE_BASE = 10000.0
EPS = 1e-6
LOG2E = math.log2(math.e)
HEAD_W = 64
BRANCH_WIDTH = 256
GLA_KEY_W = 32
GLA_CHUNK = 64
GLA_SUB = 16
GLA_BLOCK = 256
GLA_GATE_NORM = 16.0
D_Q_RANK = 192
D_FF = 2816
FF_CHUNK = 256
N_MOD_ROWS = 16

ROW_TILE = 512
FFN_TILE = 1024
Q_TILE = 256
CTX_SEQ_PER_STEP = 4
LAT_SEQ_PER_GLA_STEP = 2
HALO = 16
VMEM_LIMIT = 56 * 1024 * 1024

IN_AQ_W = 256
IN_GATE_IN = (1280, 1312)
NZ = 2176
Z_AK, Z_AV, Z_BQ, Z_BK, Z_BV, Z_BR, Z_CQ, Z_CK, Z_CV, Z_DQ = 0, 128, 256, 384, 512, 768, 1024, 1280, 1536, 1792
GATES_OFF = 2432
F_W = 1024
F_BQ, F_BK, F_BV, F_BR, F_LGF, F_LGB = 0, 128, 256, 512, 768, 896
KV_W = 1024
KV_AK, KV_AV, KV_CK, KV_CV, KV_DKV, KV_KR = 0, 128, 256, 512, 768, 896
Q_W = 1152
Q_AQ, Q_CQ, Q_DLAT, Q_DQR = 0, 256, 512, 1024
ROPE_W = 768


def _dot(a, b):
    return jnp.dot(a, b, preferred_element_type=F32)


def _dot_nt(a, b):
    return lax.dot_general(a, b, (((1,), (1,)), ((), ())), preferred_element_type=F32)


def _dot_tn(a, b):
    return lax.dot_general(a, b, (((0,), (0,)), ((), ())), preferred_element_type=F32)


def _split3(a):
    hi = a.astype(BF16)
    r1 = a - hi.astype(F32)
    mid = r1.astype(BF16)
    lo = (r1 - mid.astype(F32)).astype(BF16)
    return hi, mid, lo


def _split_dot(a, b01):
    hi, mid, lo = _split3(a)
    return _dot(hi, b01) + _dot(mid, b01) + _dot(lo, b01)


def _group_rms(v, bd, width, gain):
    ss = _split_dot(v * v, bd)
    return v * lax.rsqrt(ss * (1.0 / width) + EPS) * gain


def _rope128(v, cos, sin_a, sin_b, shift):
    return v * cos + pltpu.roll(v, 128 - shift, 1) * sin_a + pltpu.roll(v, shift, 1) * sin_b


def _modulated_norm(x, gain, shift, scale):
    ms = jnp.mean(x * x, axis=-1, keepdims=True)
    return (x * lax.rsqrt(ms + EPS) * gain) * (1.0 + scale) + shift


def _lane_mask(width, lo, hi):
    lane = lax.broadcasted_iota(jnp.int32, (1, width), 1)
    return (lane >= lo) & (lane < hi)


def _const_spec(shape, grid_rank):
    zeros = tuple(0 for _ in shape)
    if grid_rank == 1:
        return pl.BlockSpec(shape, lambda i: zeros)
    return pl.BlockSpec(shape, lambda b, i: zeros)


def _layer_spec(shape, layer, grid_rank):
    idx = (layer,) + tuple(0 for _ in shape)
    once = pl.Buffered(1)
    if grid_rank == 1:
        return pl.BlockSpec((None,) + tuple(shape), lambda i: idx, pipeline_mode=once)
    return pl.BlockSpec((None,) + tuple(shape), lambda b, i: idx, pipeline_mode=once)


def _mod_kernel(cv_ref, w_ref, b_ref, o_ref):
    cv = cv_ref[...]
    a = cv * jax.nn.sigmoid(cv)
    o_ref[...] = _dot(a.astype(BF16), w_ref[...].astype(BF16)) + b_ref[...]


def _mod_call(cvec, w_mod, b_mod):
    depth = w_mod.shape[0]
    nblk = 4
    bw = 6 * D_MODEL // nblk
    return pl.pallas_call(
        _mod_kernel,
        grid=(depth, nblk),
        in_specs=[
            pl.BlockSpec((N_MOD_ROWS, D_MODEL), lambda l, j: (0, 0)),
            pl.BlockSpec((None, D_MODEL, bw), lambda l, j: (l, 0, j)),
            pl.BlockSpec((None, 1, bw), lambda l, j: (l, 0, j)),
        ],
        out_specs=pl.BlockSpec((None, N_MOD_ROWS, bw), lambda l, j: (l, 0, j)),
        out_shape=jax.ShapeDtypeStruct((depth, N_MOD_ROWS, 6 * D_MODEL), F32),
        compiler_params=pltpu.CompilerParams(vmem_limit_bytes=VMEM_LIMIT),
        name="mod_vectors",
    )(cvec, w_mod, b_mod.reshape(depth, 1, 6 * D_MODEL))


def _win_prep_kernel(w_ref, waq_ref, wrest_ref, wgates_ref):
    w = w_ref[...]
    c0, c1 = w[:, 0:128], w[:, 128:256]
    low = _lane_mask(128, 0, HEAD_W)
    waq_ref[:, 0:128] = jnp.where(low, c0, pltpu.roll(c1, HEAD_W, 1)).astype(BF16)
    waq_ref[:, 128:256] = jnp.where(low, pltpu.roll(c0, HEAD_W, 1), c1).astype(BF16)
    g0, g1 = IN_GATE_IN
    wrest_ref[:, 0:g0 - IN_AQ_W] = w[:, IN_AQ_W:g0].astype(BF16)
    slab = w[:, g0:GATES_OFF]
    wrest_ref[:, g0 - IN_AQ_W:NZ] = pltpu.roll(slab, slab.shape[1] - (g1 - g0), 1).astype(BF16)
    wgates_ref[...] = w[:, GATES_OFF:].astype(BF16)


def _win_prep_call(w_in):
    depth, d, width = w_in.shape
    rb = 256
    out = lambda w: pl.BlockSpec((None, rb, w), lambda l, i: (l, i, 0))
    return pl.pallas_call(
        _win_prep_kernel,
        grid=(depth, d // rb),
        in_specs=[pl.BlockSpec((None, rb, width), lambda l, i: (l, i, 0))],
        out_specs=[out(IN_AQ_W), out(NZ), out(width - GATES_OFF)],
        out_shape=[jax.ShapeDtypeStruct((depth, d, w), BF16) for w in (IN_AQ_W, NZ, width - GATES_OFF)],
        compiler_params=pltpu.CompilerParams(vmem_limit_bytes=VMEM_LIMIT),
        name="in_weight_layout",
    )(w_in)


def _inproj_body(x, mod_ref, n1g_ref, waq_ref, w_ref, rope_ref, aqg_ref, akg_ref, bd_ref, wgate_ref, bgate_ref,
                 dqg_ref, wuq_ref, wuk_ref, dkvg_ref, tile_ref, f_ref, kv_ref, q_ref, h_ref, cache_refs):
    mod = mod_ref[...]
    h = _modulated_norm(x, n1g_ref[...], mod[:, 0:D_MODEL], mod[:, D_MODEL:2 * D_MODEL]).astype(BF16)
    h_ref[...] = h
    z_aq = _dot(h, waq_ref[...])
    z = _dot(h, w_ref[...])
    zd = pltpu.roll(z[:, NZ - 256:NZ], 192, 1)
    bd = bd_ref[...]
    bd128 = bd[0:128, 0:128]
    if rope_ref is None:
        rope64 = rope32 = lambda v: v
    else:
        rope = rope_ref[...]
        cos64, sa64, sb64 = rope[:, 0:128], rope[:, 128:256], rope[:, 256:384]
        cos32, sa32, sb32 = rope[:, 384:512], rope[:, 512:640], rope[:, 640:768]
        rope64 = lambda v: _rope128(v, cos64, sa64, sb64, 16)
        rope32 = lambda v: _rope128(v, cos32, sa32, sb32, 8)

    def emit_kv(col, cache_idx, val, kv_val=None):
        kv_val = val if kv_val is None else kv_val
        kv_ref[:, col:col + kv_val.shape[1]] = kv_val.astype(BF16)
        if cache_refs is not None:
            ref = cache_refs[cache_idx]
            for j in range(val.shape[0] // SEQ):
                ref[j] = val[j * SEQ:(j + 1) * SEQ, 0:ref.shape[-1]]

    aq = _group_rms(z_aq, bd, HEAD_W, aqg_ref[...])
    a_scale = HEAD_W ** -0.5 * LOG2E
    q_ref[:, Q_AQ:Q_AQ + 128] = (rope64(aq[:, 0:128]) * a_scale).astype(BF16)
    q_ref[:, Q_AQ + 128:Q_AQ + 256] = (rope64(aq[:, 128:256]) * a_scale).astype(BF16)
    emit_kv(KV_AK, 0, rope64(_group_rms(z[:, Z_AK:Z_AK + 128], bd128, HEAD_W, akg_ref[...])))
    emit_kv(KV_AV, 1, z[:, Z_AV:Z_AV + 128])

    f_ref[:, F_BQ:F_BK] = z[:, Z_BQ:Z_BK] * (GLA_KEY_W ** -0.5)
    f_ref[:, F_BK:F_BV] = z[:, Z_BK:Z_BV]
    f_ref[:, F_BV:F_BR] = z[:, Z_BV:Z_BR]
    f_ref[:, F_BR:F_LGF] = z[:, Z_BR:Z_CQ]
    pre = _dot(zd[:, 128:256].astype(BF16), wgate_ref[...]) + bgate_ref[...]
    logsig = jnp.minimum(pre, 0.0) - jnp.log1p(jnp.exp(-jnp.abs(pre)))
    f_ref[:, F_LGF:F_W] = logsig / GLA_GATE_NORM

    c_scale = 32 ** -0.5 * LOG2E
    q_ref[:, Q_CQ:Q_CQ + 128] = (rope32(z[:, Z_CQ:Z_CQ + 128]) * c_scale).astype(BF16)
    q_ref[:, Q_CQ + 128:Q_CQ + 256] = (rope32(z[:, Z_CQ + 128:Z_CQ + 256]) * c_scale).astype(BF16)
    emit_kv(KV_CK, 2, jnp.concatenate([rope32(z[:, Z_CK:Z_CK + 128]), rope32(z[:, Z_CK + 128:Z_CK + 256])],
                                      axis=-1))
    emit_kv(KV_CV, 3, z[:, Z_CV:Z_DQ])

    dkv = zd[:, 0:128]
    ms = jnp.mean(dkv * dkv, axis=-1, keepdims=True)
    emit_kv(KV_DKV, 4, dkv * lax.rsqrt(ms + EPS) * dkvg_ref[...])
    kr = rope32(zd[:, 128:256])
    emit_kv(KV_KR, 5, kr, _dot(kr[:, 0:32].astype(BF16), tile_ref[...]))
    zq = jnp.where(_lane_mask(256, 0, D_Q_RANK), z[:, Z_DQ:Z_DQ + 256], 0.0)
    msq = jnp.sum(zq * zq, axis=-1, keepdims=True) * (1.0 / D_Q_RANK)
    qn = zq * lax.rsqrt(msq + EPS) * dqg_ref[...]
    dq = _dot(qn.astype(BF16), wuq_ref[...])
    d_scale = 96 ** -0.5 * LOG2E
    q_ref[:, Q_DLAT:Q_DQR] = (_dot(dq[:, 0:256].astype(BF16), wuk_ref[...]) * d_scale).astype(BF16)
    q_ref[:, Q_DQR:Q_W] = (rope32(dq[:, 256:384]) * d_scale).astype(BF16)


def _inproj_ctx_kernel(x_ref, mod_ref, n1g_ref, waq_ref, w_ref, aqg_ref, akg_ref, bd_ref, wgate_ref, bgate_ref,
                       dqg_ref, wuq_ref, wuk_ref, dkvg_ref, tile_ref, *rest):
    outs = rest[-10:]
    x = x_ref[...].reshape(-1, D_MODEL)
    _inproj_body(x, mod_ref, n1g_ref, waq_ref, w_ref, None, aqg_ref, akg_ref, bd_ref, wgate_ref, bgate_ref,
                 dqg_ref, wuq_ref, wuk_ref, dkvg_ref, tile_ref, outs[0], outs[1], outs[2], outs[3], outs[4:])


def _inproj_lat_kernel(x_ref, mod_ref, n1g_ref, waq_ref, w_ref, rope_ref, aqg_ref, akg_ref, bd_ref, wgate_ref,
                       bgate_ref, dqg_ref, wuq_ref, wuk_ref, dkvg_ref, tile_ref, pf_ref, pkv_ref, pq_ref, ph_ref,
                       f_ref, kv_ref, q_ref, h_ref):
    del pf_ref, pkv_ref, pq_ref, ph_ref
    _inproj_body(x_ref[...], mod_ref, n1g_ref, waq_ref, w_ref, rope_ref, aqg_ref, akg_ref, bd_ref, wgate_ref,
                 bgate_ref, dqg_ref, wuq_ref, wuk_ref, dkvg_ref, tile_ref, f_ref, kv_ref, q_ref, h_ref, None)


CACHE_WIDTHS = (128, 128, 256, 256, 128, 32)


def _inproj_calls(x, n, mod4, n1g, w_aq, w_rest, rope_tab, small, prev_caches, layer, depth, n_ctx_rows):
    tm = ROW_TILE
    ctx_tiles = n_ctx_rows // tm
    seq_tiles = DEC_SEQ // tm
    spt = tm // SEQ
    n_ctx_seq = n_ctx_rows // SEQ
    if isinstance(x, tuple):
        x_ctx, x_lat = x
        x_ctx_spec = pl.BlockSpec((spt, SEQ, D_MODEL), lambda i: (i, 0, 0))
        x_lat_spec = pl.BlockSpec((None, tm, D_MODEL), lambda i: (i // seq_tiles, i % seq_tiles, 0))
    else:
        x_ctx = x_lat = x
        x_ctx_spec = pl.BlockSpec((tm, D_MODEL), lambda i: (i, 0))
        x_lat_spec = pl.BlockSpec((tm, D_MODEL), lambda i: (i + ctx_tiles, 0))
    const = lambda shape: _layer_spec(shape, layer, 1)
    small_specs = [const((1, 256)), const((1, 128)), _const_spec((256, 256), 1), const((128, 256)),
                   const((1, 256)), const((1, 256)), const((256, 384)), const((256, 512)), const((1, 128)),
                   _const_spec((32, 128), 1)]
    act_shapes = [jax.ShapeDtypeStruct((n, F_W), F32), jax.ShapeDtypeStruct((n, KV_W), BF16),
                  jax.ShapeDtypeStruct((n, Q_W), BF16), jax.ShapeDtypeStruct((n, D_MODEL), BF16)]
    act_specs = lambda off: [pl.BlockSpec((tm, w), lambda i: (i + off, 0)) for w in (F_W, KV_W, Q_W, D_MODEL)]
    n_act = len(act_shapes)
    cparams = pltpu.CompilerParams(vmem_limit_bytes=VMEM_LIMIT)
    any_spec = pl.BlockSpec(memory_space=pl.ANY)

    n_in = 5 + len(small)
    alias = {} if prev_caches is None else {n_in + k: n_act + k for k in range(6)}
    res = pl.pallas_call(
        _inproj_ctx_kernel,
        grid=(ctx_tiles,),
        in_specs=[
            x_ctx_spec,
            pl.BlockSpec((None, None, 1, 6 * D_MODEL), lambda i: (layer, 0, 0, 0)),
            const((1, D_MODEL)),
            const((D_MODEL, IN_AQ_W)),
            const((D_MODEL, NZ)),
        ] + small_specs + ([] if prev_caches is None else [any_spec] * 6),
        out_specs=act_specs(0) + [pl.BlockSpec((spt, None, SEQ, w), lambda i: (i, layer, 0, 0))
                                  for w in CACHE_WIDTHS],
        out_shape=act_shapes + [jax.ShapeDtypeStruct((n_ctx_seq, depth, SEQ, w), F32) for w in CACHE_WIDTHS],
        input_output_aliases=alias,
        compiler_params=cparams,
        name="in_projection_context",
    )(x_ctx, mod4, n1g, w_aq, w_rest, *small, *([] if prev_caches is None else prev_caches))
    acts_c, caches = res[:n_act], tuple(res[n_act:])

    fa, kva, qa, ha = pl.pallas_call(
        _inproj_lat_kernel,
        grid=((n - n_ctx_rows) // tm,),
        in_specs=[
            x_lat_spec,
            pl.BlockSpec((None, None, 1, 6 * D_MODEL), lambda i: (layer, 1 + i // seq_tiles, 0, 0)),
            const((1, D_MODEL)),
            const((D_MODEL, IN_AQ_W)),
            const((D_MODEL, NZ)),
            pl.BlockSpec((tm, ROPE_W), lambda i: (i % seq_tiles, 0)),
        ] + small_specs + [any_spec] * n_act,
        out_specs=act_specs(ctx_tiles),
        out_shape=act_shapes,
        input_output_aliases={n_in + 1 + k: k for k in range(n_act)},
        compiler_params=cparams,
        name="in_projection_latent",
    )(x_lat, mod4, n1g, w_aq, w_rest, rope_tab, *small, *acts_c)
    return fa, kva, qa, ha, caches


def _exp2_parts(parts):
    m = parts[0].max(axis=-1, keepdims=True)
    for s in parts[1:]:
        m = jnp.maximum(m, s.max(axis=-1, keepdims=True))
    return [jnp.exp2(s - m) for s in parts]


def _value_variants(av, cv, dk):
    one = jnp.ones((), BF16)
    av1 = [jnp.where(_lane_mask(128, hh * HEAD_W, (hh + 1) * HEAD_W), av, one) for hh in range(2)]
    cv1 = [jnp.where(_lane_mask(256, hh * HEAD_W, (hh + 1) * HEAD_W), cv, one) for hh in range(4)]
    vlat = jnp.where(_lane_mask(256, 0, 128), dk, one)
    return av1, cv1, vlat


def _run(phases):
    for _ in phases:
        pass


def _interleave(*phase_generators):
    live = list(phase_generators)
    while live:
        live = [g for g in live if next(g, StopIteration) is not StopIteration]


def _attn_core(*args):
    _run(_attn_phases(*args))


def _attn_phases(q, ak, av1, ck, cv1, dk, vlat, lam_ref, cong_ref, wuv_ref, bd_ref, o_ref, lam_init):
    zero = jnp.zeros((), BF16)
    ak, ck, dk = [ak], [ck], [dk]
    tq = q.shape[0]

    masks_a = [_lane_mask(128, hh * HEAD_W, (hh + 1) * HEAD_W) for hh in range(2)]
    qa = jnp.concatenate([jnp.where(masks_a[hh], q[:, Q_AQ + g * 128:Q_AQ + (g + 1) * 128], zero)
                          for hh in range(2) for g in range(2)], axis=0)
    qcs = [jnp.concatenate([jnp.where(_lane_mask(128, u * 32, (u + 1) * 32),
                                      q[:, Q_CQ + c * 128:Q_CQ + (c + 1) * 128], zero) for u in range(4)], axis=0)
           for c in range(2)]
    qd = jnp.concatenate(
        [jnp.concatenate([q[:, Q_DLAT + hh * 128:Q_DLAT + (hh + 1) * 128],
                          jnp.where(_lane_mask(128, hh * 32, (hh + 1) * 32), q[:, Q_DQR:Q_W], zero)], axis=-1)
         for hh in range(4)], axis=0)
    s_a = [_dot_nt(qa, k) for k in ak]
    s_c = [[_dot_nt(qcs[c], k[:, c * 128:(c + 1) * 128]) for k in ck] for c in range(2)]
    yield

    ps = [p.astype(BF16) for p in _exp2_parts(s_a)]
    s_d = [_dot_nt(qd, k) for k in dk]
    oa = [None, None]
    for hh in range(2):
        rows = slice(hh * 2 * tq, (hh + 1) * 2 * tq)
        o = _dot(ps[0][rows], av1[hh])
        o = jnp.where(masks_a[hh], o / pltpu.roll(o, HEAD_W, 1), 0.0)
        for g in range(2):
            part = o[g * tq:(g + 1) * tq]
            oa[g] = part if oa[g] is None else oa[g] + part
    o_ref[:, 0:128] = oa[0]
    o_ref[:, 128:256] = oa[1]
    yield

    lv = lam_ref[...]
    lam = (jnp.exp(jnp.sum(lv[0:1] * lv[1:2], axis=-1, keepdims=True))
           - jnp.exp(jnp.sum(lv[2:3] * lv[3:4], axis=-1, keepdims=True)) + lam_init)
    oc = []
    for c in range(2):
        c0 = c * 128
        ps = [p.astype(BF16) for p in _exp2_parts(s_c[c])]
        och = None
        for h2 in range(2):
            hh = c * 2 + h2
            rows = slice(h2 * 2 * tq, (h2 + 1) * 2 * tq)
            o = _dot(ps[0][rows], cv1[hh])
            res = o[:, c0:c0 + 128] / o[:, 128 - c0:256 - c0]
            o = jnp.where(_lane_mask(128, h2 * HEAD_W, (h2 + 1) * HEAD_W), res[0:tq] - lam * res[tq:2 * tq], 0.0)
            och = o if och is None else och + o
            yield
        oc.append(och)
    oc = jnp.concatenate(oc, axis=-1)
    o_ref[:, 256:512] = _group_rms(oc, bd_ref[...], HEAD_W, cong_ref[...]) * (1.0 - lam_init)

    ps = [p.astype(BF16) for p in _exp2_parts(s_d)]
    o = _dot(ps[0], vlat)
    lat = (o[:, 0:128] / o[:, 128:256]).astype(BF16)
    o_ref[:, 512:768] = _dot(jnp.concatenate([lat[hh * tq:(hh + 1) * tq] for hh in range(4)], axis=-1),
                             wuv_ref[...])


def _attn_lat_kernel(q_ref, kva_ref, ck_ref, cv_ref, dkv_ref, cak_ref, cav_ref, cck_ref, ccv_ref,
                     cckv_ref, ckr_ref, lam_ref, cong_ref, wuv_ref, bd_ref, tile_ref, prev_ref, o_ref,
                     ak_s, ck_s, dk_s, av1_s, cv1_s, vlat_s, *, lam_init):
    del prev_ref

    @pl.when(pl.program_id(1) == 0)
    def _():
        p = PAST_LEN
        kva = kva_ref[...]
        kr4 = _dot(ckr_ref[...].astype(BF16), tile_ref[...]).astype(BF16)
        pieces = ((cak_ref[...].astype(BF16), cav_ref[...].astype(BF16), cck_ref[...].astype(BF16),
                   ccv_ref[...].astype(BF16), jnp.concatenate([cckv_ref[...].astype(BF16), kr4], axis=-1)),
                  (kva[:, 0:128], kva[:, 128:256], ck_ref[...], cv_ref[...], dkv_ref[...]))
        for rows, (ak, av, ck, cv, dk) in zip((slice(0, p), slice(p, p + DEC_SEQ)), pieces):
            av1, cv1, vlat = _value_variants(av, cv, dk)
            ak_s[rows] = ak
            ck_s[rows] = ck
            dk_s[rows] = dk
            vlat_s[rows] = vlat
            for hh in range(2):
                av1_s[hh, rows] = av1[hh]
            for hh in range(4):
                cv1_s[hh, rows] = cv1[hh]

    _attn_core(q_ref[...], ak_s[...], [av1_s[hh] for hh in range(2)], ck_s[...],
               [cv1_s[hh] for hh in range(4)], dk_s[...], vlat_s[...], lam_ref, cong_ref, wuv_ref, bd_ref,
               o_ref, lam_init)


def _attn_lat_call(kva, qa, caches, lamv, cong, wuv, bd, tile, o_ctx, layer, lam_init, n_ctx_rows):
    n = kva.shape[0]
    n_lat_seq = (n - n_ctx_rows) // DEC_SEQ
    param_specs = lambda rank: [_layer_spec((4, 32), layer, rank), _layer_spec((1, 256), layer, rank),
                                _layer_spec((512, 256), layer, rank), _const_spec((256, 256), rank),
                                _const_spec((32, 128), rank)]
    qt = Q_TILE
    q_per_seq = DEC_SEQ // qt
    q0 = n_ctx_rows // qt
    s0 = n_ctx_rows // DEC_SEQ
    cak, cav, cck, ccv, cckv, ckr = caches
    cache_spec = lambda w: pl.BlockSpec((None, None, PAST_LEN, w), lambda b, i: (b, layer, 0, 0))
    seq_cols = lambda idx: pl.BlockSpec((DEC_SEQ, 256), lambda b, i: (s0 + b, idx))
    n_keys = PAST_LEN + DEC_SEQ
    return pl.pallas_call(
        functools.partial(_attn_lat_kernel, lam_init=lam_init),
        grid=(n_lat_seq, q_per_seq),
        in_specs=[
            pl.BlockSpec((qt, Q_W), lambda b, i: (q0 + b * q_per_seq + i, 0)),
            seq_cols(KV_AK // 256), seq_cols(KV_CK // 256), seq_cols(KV_CV // 256), seq_cols(KV_DKV // 256),
            cache_spec(128), cache_spec(128), cache_spec(256), cache_spec(256), cache_spec(128),
            cache_spec(32),
        ] + param_specs(2) + [pl.BlockSpec(memory_space=pl.ANY)],
        out_specs=pl.BlockSpec((qt, 3 * BRANCH_WIDTH), lambda b, i: (q0 + b * q_per_seq + i, 0)),
        out_shape=jax.ShapeDtypeStruct((n, 3 * BRANCH_WIDTH), F32),
        scratch_shapes=[pltpu.VMEM((n_keys, 128), BF16), pltpu.VMEM((n_keys, 256), BF16),
                        pltpu.VMEM((n_keys, 256), BF16), pltpu.VMEM((2, n_keys, 128), BF16),
                        pltpu.VMEM((4, n_keys, 256), BF16), pltpu.VMEM((n_keys, 256), BF16)],
        input_output_aliases={16: 0},
        compiler_params=pltpu.CompilerParams(vmem_limit_bytes=VMEM_LIMIT),
        name="attention_latent",
    )(qa, kva, kva, kva, kva, cak, cav, cck, ccv, cckv, ckr, lamv, cong, wuv, bd, tile, o_ctx)


def _gla_chunk(q, k, v, bc, st, reverse, masks, result):
    c, sub, nsub = GLA_CHUNK, GLA_SUB, GLA_CHUNK // GLA_SUB
    keep, qmask, src_row, blk = masks
    edge = bc[0:1] if reverse else bc[c - 1:c]
    qs, kd = [], []
    for i in range(nsub):
        r0 = i * sub
        ref = bc[r0 + sub - 1:r0 + sub] if reverse else bc[r0:r0 + 1]
        qd = q[r0:r0 + sub] * jnp.exp(bc[r0:r0 + sub] - ref)
        qs.append(jnp.where(qmask, jnp.concatenate([qd] * 4, axis=0), 0.0))
        in_range = (src_row >= r0) if reverse else (src_row < r0 + sub)
        kd.append(k * jnp.exp(jnp.where(in_range, ref - bc, 0.0)))
    att = _dot_nt(jnp.concatenate(qs, axis=0).astype(BF16), jnp.concatenate(kd, axis=0).astype(BF16))
    vb = v.astype(BF16)
    q_dec = q * jnp.exp(bc)
    k_dec = k * jnp.exp(edge - bc)
    o_inter = _dot_nt(q_dec.astype(BF16), st.astype(BF16))
    upd = _dot_tn(vb, k_dec.astype(BF16))
    yield
    att = jnp.where(keep, att, 0.0).astype(BF16)
    pv = _dot(att, jnp.concatenate([vb] * nsub, axis=0))
    st_new = st * jnp.exp(edge) + jnp.where(blk, upd, 0.0)
    yield
    out_blocks = []
    for i in range(nsub):
        o = None
        for hh in range(4):
            r0 = i * c + hh * sub
            part = jnp.where(_lane_mask(256, hh * HEAD_W, (hh + 1) * HEAD_W), pv[r0:r0 + sub], 0.0)
            o = part if o is None else o + part
        out_blocks.append(o)
    o_intra = jnp.concatenate(out_blocks, axis=0)
    result.extend([o_intra + o_inter, st_new])


def _gla_masks(reverse):
    c, sub = GLA_CHUNK, GLA_SUB
    row = lax.broadcasted_iota(jnp.int32, (4 * c, 4 * c), 0)
    col = lax.broadcasted_iota(jnp.int32, (4 * c, 4 * c), 1)
    t = (row // c) * sub + row % sub
    s = col % c
    keep = (row // c == col // c) & ((s >= t) if reverse else (s <= t))
    qmask = (lax.broadcasted_iota(jnp.int32, (4 * sub, 128), 0) // sub
             == lax.broadcasted_iota(jnp.int32, (4 * sub, 128), 1) // GLA_KEY_W)
    src_row = lax.broadcasted_iota(jnp.int32, (c, 1), 0)
    blk = (lax.broadcasted_iota(jnp.int32, (256, 128), 0) // HEAD_W
           == lax.broadcasted_iota(jnp.int32, (256, 128), 1) // GLA_KEY_W)
    return keep, qmask, src_row, blk


def _gla_parts(qk_ref, v_ref, r_ref, lg_ref, g_ref, bd_ref, tri_ref, o_ref, accf_ref, accb_ref, bcf_ref, bcb_ref,
               seq_len):
    nc = seq_len // GLA_CHUNK
    masks_f, masks_b = _gla_masks(False), _gla_masks(True)

    def decays():
        for r0 in range(0, seq_len, GLA_BLOCK):
            lg = lg_ref[r0:r0 + GLA_BLOCK, :]
            for d, bc_ref in ((0, bcf_ref), (1, bcb_ref)):
                res = _dot(tri_ref[d], jnp.concatenate(_split3(lg[:, d * 128:(d + 1) * 128]), axis=-1))
                bc_ref[r0:r0 + GLA_BLOCK, :] = res[:, 0:128] + res[:, 128:256] + res[:, 256:384]

    def chunk_rows(c):
        start = c * GLA_CHUNK
        return pl.ds(start if isinstance(c, int) else pl.multiple_of(start, GLA_CHUNK), GLA_CHUNK)

    def pair_phases(ci, carry, new_carry):
        rf, rb = chunk_rows(ci), chunk_rows(nc - 1 - ci)
        res_f, res_b = [], []
        live = [_gla_chunk(qk_ref[rf, 0:128], qk_ref[rf, 128:256], v_ref[rf, :], bcf_ref[rf, :], carry[0], False,
                           masks_f, res_f),
                _gla_chunk(qk_ref[rb, 0:128], qk_ref[rb, 128:256], v_ref[rb, :], bcb_ref[rb, :], carry[1], True,
                           masks_b, res_b)]
        while live:
            live = [g for g in live if next(g, StopIteration) is not StopIteration]
            yield
        accf_ref[rf, :] = res_f[0]
        accb_ref[rb, :] = res_b[0]
        new_carry[:] = [res_f[1], res_b[1]]

    def finish():
        r = r_ref[...]
        o_ref[...] = (_group_rms(accf_ref[...] + accb_ref[...], bd_ref[...], HEAD_W, g_ref[...])
                      * (r * jax.nn.sigmoid(r)))

    return decays, pair_phases, finish


def _gla_seq_phases(parts, n_pairs, carry, states):
    decays, pair_phases, finish = parts
    decays()
    yield
    for ci in range(n_pairs):
        nxt = []
        yield from pair_phases(ci, carry, nxt)
        carry = tuple(nxt)
    finish()
    states.extend(carry)


def _ctx_mixers_kernel(q_ref, kva_ref, ck_ref, cv_ref, dkv_ref, qk_ref, v_ref, r_ref, lg_ref, lam_ref, cong_ref,
                       wuv_ref, bd_ref, bong_ref, tri_ref, o_ref, ob_ref, sf_ref, sb_ref, *scratch, lam_init):
    zero = jnp.zeros((256, 128), F32)
    nseq = q_ref.shape[0] // SEQ
    gens, states = [], [[] for _ in range(nseq)]
    for j in range(nseq):
        rows = slice(j * SEQ, (j + 1) * SEQ)
        kva = kva_ref[rows, :]
        dk = dkv_ref[rows, :]
        av1, cv1, vlat = _value_variants(kva[:, 128:256], cv_ref[rows, :], dk)
        gens.append(_attn_phases(q_ref[rows, :], kva[:, 0:128], av1, ck_ref[rows, :], cv1, dk, vlat, lam_ref,
                                 cong_ref, wuv_ref, bd_ref, o_ref.at[rows, :], lam_init))
        parts = _gla_parts(qk_ref.at[rows, :], v_ref.at[rows, :], r_ref.at[rows, :], lg_ref.at[rows, :], bong_ref,
                           bd_ref, tri_ref, ob_ref.at[rows, :], *[s.at[rows, :] for s in scratch], SEQ)
        gens.append(_gla_seq_phases(parts, SEQ // GLA_CHUNK, (zero, zero), states[j]))
    _interleave(*gens)
    for j in range(nseq):
        sf_ref[j], sb_ref[j] = states[j]


def _ctx_mixers_call(fa, kva, qa, lamv, cong, wuv, bd, bong, tri, layer, lam_init, n_ctx_rows):
    n = fa.shape[0]
    n_ctx_seq = n_ctx_rows // SEQ
    nseq = CTX_SEQ_PER_STEP
    crows = nseq * SEQ
    const = lambda shape: _const_spec(shape, 1)
    lspec = lambda shape: _layer_spec(shape, layer, 1)
    col = lambda w, off: pl.BlockSpec((crows, w), lambda b: (b, off // w))
    st_shape = jax.ShapeDtypeStruct((n_ctx_seq, 256, 128), F32)
    st_spec = pl.BlockSpec((nseq, 256, 128), lambda b: (b, 0, 0))
    return pl.pallas_call(
        functools.partial(_ctx_mixers_kernel, lam_init=lam_init),
        grid=(n_ctx_seq // nseq,),
        in_specs=[
            col(Q_W, 0), col(256, KV_AK), col(256, KV_CK), col(256, KV_CV), col(256, KV_DKV),
            col(256, F_BQ), col(256, F_BV), col(256, F_BR), col(256, F_LGF),
            lspec((4, 32)), lspec((1, 256)), lspec((512, 256)), const((256, 256)), lspec((1, 256)),
            const((2, GLA_BLOCK, GLA_BLOCK)),
        ],
        out_specs=[col(3 * BRANCH_WIDTH, 0), col(BRANCH_WIDTH, 0), st_spec, st_spec],
        out_shape=[jax.ShapeDtypeStruct((n, 3 * BRANCH_WIDTH), F32), jax.ShapeDtypeStruct((n, BRANCH_WIDTH), F32),
                   st_shape, st_shape],
        scratch_shapes=[pltpu.VMEM((crows, BRANCH_WIDTH), F32), pltpu.VMEM((crows, BRANCH_WIDTH), F32),
                        pltpu.VMEM((crows, 128), F32), pltpu.VMEM((crows, 128), F32)],
        compiler_params=pltpu.CompilerParams(vmem_limit_bytes=VMEM_LIMIT),
        name="mixers_context",
    )(qa, kva, kva, kva, kva, fa, fa, fa, fa, lamv, cong, wuv, bd, bong, tri)


def _gla_lat_kernel(qk_ref, v_ref, r_ref, lg_ref, s0f_ref, s0b_ref, g_ref, bd_ref, tri_ref, prev_ref,
                    o_ref, *scratch):
    del prev_ref
    nseq = qk_ref.shape[0] // DEC_SEQ
    parts = []
    for j in range(nseq):
        rows = slice(j * DEC_SEQ, (j + 1) * DEC_SEQ)
        parts.append(_gla_parts(qk_ref.at[rows, :], v_ref.at[rows, :], r_ref.at[rows, :], lg_ref.at[rows, :], g_ref,
                                bd_ref, tri_ref, o_ref.at[rows, :], *[s.at[rows, :] for s in scratch], DEC_SEQ))
    for decays, _, _ in parts:
        decays()

    def step(ci, carry):
        nxt = [[] for _ in range(nseq)]
        _interleave(*[parts[j][1](ci, carry[2 * j:2 * j + 2], nxt[j]) for j in range(nseq)])
        return tuple(s for pair in nxt for s in pair)

    init = tuple(s for j in range(nseq) for s in (s0f_ref[j], s0b_ref[j]))
    lax.fori_loop(0, DEC_SEQ // GLA_CHUNK, step, init)
    for _, _, finish in parts:
        finish()


def _gla_lat_call(fa, s0f_t, s0b_t, bong, bd, tri, o_ctx, layer, n_ctx_rows):
    n = fa.shape[0]
    n_lat_seq = (n - n_ctx_rows) // DEC_SEQ
    const = lambda shape: _const_spec(shape, 1)
    nseq = LAT_SEQ_PER_GLA_STEP
    rows = nseq * DEC_SEQ
    state_spec = pl.BlockSpec((nseq, None, 256, 128), lambda b: (b, layer, 0, 0))
    s0 = n_ctx_rows // rows
    seq_cols = lambda off: pl.BlockSpec((rows, 256), lambda b: (s0 + b, off // 256))
    return pl.pallas_call(
        _gla_lat_kernel,
        grid=(n_lat_seq // nseq,),
        in_specs=[
            seq_cols(F_BQ), seq_cols(F_BV), seq_cols(F_BR), seq_cols(F_LGF),
            state_spec, state_spec,
            _layer_spec((1, 256), layer, 1), const((256, 256)), const((2, GLA_BLOCK, GLA_BLOCK)),
            pl.BlockSpec(memory_space=pl.ANY),
        ],
        out_specs=pl.BlockSpec((rows, BRANCH_WIDTH), lambda b: (s0 + b, 0)),
        out_shape=jax.ShapeDtypeStruct((n, BRANCH_WIDTH), F32),
        scratch_shapes=[pltpu.VMEM((rows, BRANCH_WIDTH), F32), pltpu.VMEM((rows, BRANCH_WIDTH), F32),
                        pltpu.VMEM((rows, 128), F32), pltpu.VMEM((rows, 128), F32)],
        input_output_aliases={9: 0},
        compiler_params=pltpu.CompilerParams(vmem_limit_bytes=VMEM_LIMIT),
        name="gla_latent",
    )(fa, fa, fa, fa, s0f_t, s0b_t, bong, bd, tri, o_ctx)


def _merge_kernel(*refs, ctx_tiles):
    mod_ref, h_ref, wg_ref, oacd_ref, ob_ref, wb_ref, wout_ref, o_ref = refs[-8:]
    if len(refs) == 9:
        x = refs[0][...]
    else:
        x = jnp.where(pl.program_id(0) < ctx_tiles, refs[0][...].reshape(-1, D_MODEL), refs[1][...])
    mod = mod_ref[...]
    h = h_ref[...]
    oacd = oacd_ref[...]
    branches = [oacd[:, 0:256], ob_ref[...], oacd[:, 256:512], oacd[:, 512:768]]
    acc = None
    for j in range(4):
        gate = jax.nn.sigmoid(_dot(h, wg_ref[:, j * D_MODEL:(j + 1) * D_MODEL]))
        term = gate * _dot(branches[j].astype(BF16), wb_ref[j])
        acc = term if acc is None else acc + term
    mix = _dot(acc.astype(BF16), wout_ref[...])
    o_ref[...] = x + mod[:, 2 * D_MODEL:3 * D_MODEL] * mix


def _mod_map(layer, ctx_tiles, seq_tiles):
    return lambda i: (layer, jnp.where(i < ctx_tiles, 0, 1 + (i - ctx_tiles) // seq_tiles), 0, 0)


def _merge_call(x, n, mod4, ha, wgates, oacd, ob, wb, wout, layer, n_ctx_rows):
    tm = ROW_TILE
    ctx_tiles = n_ctx_rows // tm
    seq_tiles = DEC_SEQ // tm
    const = lambda shape: _layer_spec(shape, layer, 1)
    if isinstance(x, tuple):
        xs = list(x)
        x_specs = [
            pl.BlockSpec((tm // SEQ, SEQ, D_MODEL), lambda i: (jnp.minimum(i, ctx_tiles - 1), 0, 0)),
            pl.BlockSpec((None, tm, D_MODEL), lambda i: (jnp.maximum(i - ctx_tiles, 0) // seq_tiles,
                                                         jnp.maximum(i - ctx_tiles, 0) % seq_tiles, 0)),
        ]
    else:
        xs = [x]
        x_specs = [pl.BlockSpec((tm, D_MODEL), lambda i: (i, 0))]
    return pl.pallas_call(
        functools.partial(_merge_kernel, ctx_tiles=ctx_tiles),
        grid=(n // tm,),
        in_specs=x_specs + [
            pl.BlockSpec((None, None, 1, 6 * D_MODEL), _mod_map(layer, ctx_tiles, seq_tiles)),
            pl.BlockSpec((tm, D_MODEL), lambda i: (i, 0)),
            const((D_MODEL, 4 * D_MODEL)),
            pl.BlockSpec((tm, 3 * BRANCH_WIDTH), lambda i: (i, 0)),
            pl.BlockSpec((tm, BRANCH_WIDTH), lambda i: (i, 0)),
            const((4, BRANCH_WIDTH, D_MODEL)),
            const((D_MODEL, D_MODEL)),
        ],
        out_specs=pl.BlockSpec((tm, D_MODEL), lambda i: (i, 0)),
        out_shape=jax.ShapeDtypeStruct((n, D_MODEL), F32),
        compiler_params=pltpu.CompilerParams(vmem_limit_bytes=VMEM_LIMIT),
        name="merge",
    )(*xs, mod4, ha, wgates, oacd, ob, wb, wout)


def _ffn_kernel(x_ref, xp_ref, xn_ref, mod_ref, n2g_ref, wu_ref, wg_ref, cw_ref, cb_ref, wd_ref, fg_ref,
                *rest, ctx_tiles, final):
    out_refs, (hext_ref, act_ref) = rest[:-2], rest[-2:]
    tm = x_ref.shape[0]
    i = pl.program_id(0)
    x = x_ref[...]
    mod = mod_ref[...]
    sh2, sc2, g2 = mod[:, 3 * D_MODEL:4 * D_MODEL], mod[:, 4 * D_MODEL:5 * D_MODEL], mod[:, 5 * D_MODEL:]
    n2g = n2g_ref[...]
    hext_ref[0:HALO] = _modulated_norm(xp_ref[...], n2g, sh2, sc2).astype(BF16)
    hext_ref[HALO:HALO + tm] = _modulated_norm(x, n2g, sh2, sc2).astype(BF16)
    hext_ref[HALO + tm:] = _modulated_norm(xn_ref[...], n2g, sh2, sc2).astype(BF16)
    seq_len = jnp.where(i < ctx_tiles, SEQ, DEC_SEQ)
    row = lax.broadcasted_iota(jnp.int32, (tm, 1), 0)
    pos = (row + i * tm) & (seq_len - 1)
    first = pos == 0
    last = pos == seq_len - 1
    for c0 in range(0, D_FF, FF_CHUNK):
        u_ext = _dot(hext_ref[...], wu_ref[:, c0:c0 + FF_CHUNK])
        u = u_ext[HALO:HALO + tm]
        u_dn = jnp.where(first, 0.0, u_ext[HALO - 1:HALO - 1 + tm])
        u_up = jnp.where(last, 0.0, u_ext[HALO + 1:HALO + 1 + tm])
        cw = cw_ref[:, c0:c0 + FF_CHUNK]
        uc = u_dn * cw[0:1] + u * cw[1:2] + u_up * cw[2:3] + cb_ref[:, c0:c0 + FF_CHUNK]
        g = _dot(hext_ref[HALO:HALO + tm], wg_ref[:, c0:c0 + FF_CHUNK])
        act_ref[:, c0:c0 + FF_CHUNK] = (jax.nn.gelu(uc) * g).astype(BF16)
    out = x + g2 * _dot(act_ref[...], wd_ref[...])
    if final:
        yp_ref, ys_ref = out_refs
        ms = jnp.mean(out * out, axis=-1, keepdims=True)
        y = out * lax.rsqrt(ms + EPS) * fg_ref[...]

        @pl.when(i < ctx_tiles)
        def _():
            for j in range(tm // SEQ):
                yp_ref[j] = y[j * SEQ:(j + 1) * SEQ]

        @pl.when(i >= ctx_tiles)
        def _():
            ys_ref[...] = y
    else:
        out_refs[0][...] = out


def _ffn_call(x, mod4, n2g, wu, wg, cw, cb, wd, fg, layer, n_ctx_rows, final):
    n = x.shape[0]
    tm = FFN_TILE
    ctx_tiles = n_ctx_rows // tm
    seq_tiles = DEC_SEQ // tm
    hb = tm // HALO
    last_halo = n // HALO - 1
    lspec = lambda shape: _layer_spec(shape, layer, 1)
    if final:
        spt = tm // SEQ
        out_specs = [
            pl.BlockSpec((spt, SEQ, D_MODEL), lambda i: (jnp.minimum(i, ctx_tiles - 1), 0, 0)),
            pl.BlockSpec((None, tm, D_MODEL), lambda i: (jnp.maximum(i - ctx_tiles, 0) // seq_tiles,
                                                         jnp.maximum(i - ctx_tiles, 0) % seq_tiles, 0)),
        ]
        out_shape = [jax.ShapeDtypeStruct((n_ctx_rows // SEQ, SEQ, D_MODEL), F32),
                     jax.ShapeDtypeStruct(((n - n_ctx_rows) // DEC_SEQ, DEC_SEQ, D_MODEL), F32)]
    else:
        out_specs = [pl.BlockSpec((tm, D_MODEL), lambda i: (i, 0))]
        out_shape = [jax.ShapeDtypeStruct((n, D_MODEL), F32)]
    return pl.pallas_call(
        functools.partial(_ffn_kernel, ctx_tiles=ctx_tiles, final=final),
        grid=(n // tm,),
        in_specs=[
            pl.BlockSpec((tm, D_MODEL), lambda i: (i, 0)),
            pl.BlockSpec((HALO, D_MODEL), lambda i: (jnp.maximum(i * hb - 1, 0), 0)),
            pl.BlockSpec((HALO, D_MODEL), lambda i: (jnp.minimum((i + 1) * hb, last_halo), 0)),
            pl.BlockSpec((None, None, 1, 6 * D_MODEL), _mod_map(layer, ctx_tiles, seq_tiles)),
            lspec((1, D_MODEL)),
            lspec((D_MODEL, D_FF)), lspec((D_MODEL, D_FF)), lspec((8, D_FF)), lspec((1, D_FF)),
            lspec((D_FF, D_MODEL)), _const_spec((1, D_MODEL), 1),
        ],
        out_specs=out_specs,
        out_shape=out_shape,
        scratch_shapes=[pltpu.VMEM((tm + 2 * HALO, D_MODEL), BF16), pltpu.VMEM((tm, D_FF), BF16)],
        compiler_params=pltpu.CompilerParams(vmem_limit_bytes=VMEM_LIMIT),
        name="conv_ffn",
    )(x, x, x, mod4, n2g, wu, wg, cw, cb, wd, fg)


def _rope_tables():
    t = np.arange(DEC_SEQ)
    row = (t // GRID_W).astype(np.float32)
    col = (t % GRID_W).astype(np.float32)

    def tables(head_dim):
        r = head_dim // 2
        half = r // 2
        inv = (np.float32(ROPE_BASE) ** (-np.arange(half, dtype=np.float32) / np.float32(half))).astype(np.float32)
        lane = np.arange(128) % head_dim
        use_col = lane >= r
        e = lane % r
        first_half = e < half
        freq = inv[e % half]
        pos = np.where(use_col[None, :], col[:, None], row[:, None]).astype(np.float32)
        ang = (pos * freq[None, :]).astype(np.float32)
        cos = np.cos(ang).astype(np.float32)
        sin = np.sin(ang).astype(np.float32)
        sin_a = np.where(first_half[None, :], -sin, 0.0).astype(np.float32)
        sin_b = np.where(first_half[None, :], 0.0, sin).astype(np.float32)
        return cos, sin_a, sin_b

    return jnp.asarray(np.concatenate(list(tables(64)) + list(tables(32)), axis=1))


def kernel(x_prompt, x_sample, c, cache_a_k, cache_a_v, state_b_fwd, state_b_bwd, cache_c_k, cache_c_v, cache_d_ckv, cache_d_krope, c_ctx, w_mod, b_mod, norm1_g, norm2_g, w_in, a_qnorm_g, a_knorm_g, b_gate_w_fwd, b_gate_b_fwd, b_gate_w_bwd, b_gate_b_bwd, b_onorm_g, c_lq1, c_lk1, c_lq2, c_lk2, c_onorm_g, d_qnorm_g, d_w_uq, d_kvnorm_g, d_w_ukv, w_branch, w_out, w_ffu, w_ffg, conv_w, conv_b, w_ffd, final_g):
    depth = w_in.shape[0]
    nb, nd = x_prompt.shape[0], x_sample.shape[0]
    n_ctx = nb * SEQ
    assert n_ctx % (LAT_SEQ_PER_GLA_STEP * DEC_SEQ) == 0 and nd % LAT_SEQ_PER_GLA_STEP == 0
    assert nb % CTX_SEQ_PER_STEP == 0 and n_ctx % FFN_TILE == 0 and nd + 1 <= N_MOD_ROWS

    n = n_ctx + nd * DEC_SEQ
    cvec = jnp.zeros((N_MOD_ROWS, D_MODEL), F32).at[0].set(c_ctx).at[1:1 + nd].set(c)
    mod4 = _mod_call(cvec, w_mod, b_mod).reshape(depth, N_MOD_ROWS, 1, 6 * D_MODEL)

    rope_tab = _rope_tables()
    lane = np.arange(256)
    bd = jnp.asarray(lane[:, None] // HEAD_W == lane[None, :] // HEAD_W, BF16)
    tile = jnp.asarray(np.arange(32)[:, None] == np.arange(128)[None, :] % 32, BF16)
    tt = np.arange(GLA_BLOCK)
    same_chunk = tt[:, None] // GLA_CHUNK == tt[None, :] // GLA_CHUNK
    tri = jnp.asarray(np.stack([same_chunk & (tt[:, None] >= tt[None, :]),
                                same_chunk & (tt[:, None] <= tt[None, :])]), BF16)

    w_aq, w_rest, w_gates = _win_prep_call(w_in)
    wgate = jnp.zeros((depth, 128, 256), F32)
    wgate = wgate.at[:, 32:48, 0:128].set(b_gate_w_fwd).at[:, 48:64, 128:256].set(b_gate_w_bwd).astype(BF16)
    bgate = jnp.concatenate([b_gate_b_fwd, b_gate_b_bwd], axis=-1)[:, None, :]
    dqg = jnp.pad(d_qnorm_g, ((0, 0), (0, 256 - D_Q_RANK)))[:, None, :]
    uq = d_w_uq.reshape(depth, D_Q_RANK, 4, 96)
    wuq = jnp.concatenate([uq[..., :64].reshape(depth, D_Q_RANK, 256), uq[..., 64:].reshape(depth, D_Q_RANK, 128)], axis=-1)
    wuq = jnp.pad(wuq, ((0, 0), (0, 256 - D_Q_RANK), (0, 0))).astype(BF16)
    ukv = d_w_ukv.reshape(depth, 128, 4, 128)
    eye4 = jnp.eye(4, dtype=F32)
    wuk = (ukv[..., :64].transpose(0, 2, 3, 1)[:, :, :, None, :] * eye4[None, :, None, :, None])
    wuk = wuk.reshape(depth, 256, 512).astype(BF16)
    wuv = (ukv[..., 64:].transpose(0, 2, 1, 3)[:, :, :, None, :] * eye4[None, :, None, :, None])
    wuv = wuv.reshape(depth, 512, 256).astype(BF16)
    wb0 = w_branch[:, 0].reshape(depth, 2, 2, 64, D_MODEL).transpose(0, 2, 1, 3, 4).reshape(depth, 256, D_MODEL)
    wb = jnp.concatenate([wb0[:, None], w_branch[:, 1:]], axis=1).astype(BF16)
    wout = w_out.astype(BF16)
    wu, wg, wd = w_ffu.astype(BF16), w_ffg.astype(BF16), w_ffd.astype(BF16)
    cw = jnp.pad(conv_w, ((0, 0), (0, 8 - conv_w.shape[1]), (0, 0)))
    aqg = jnp.tile(a_qnorm_g, (1, 4))[:, None, :]
    akg = jnp.tile(a_knorm_g, (1, 2))[:, None, :]
    bong = jnp.tile(b_onorm_g, (1, 4))[:, None, :]
    cong = jnp.tile(c_onorm_g, (1, 4))[:, None, :]
    lamv = jnp.stack([c_lq1, c_lk1, c_lq2, c_lk2], axis=1)
    caches = (cache_a_k.reshape(nd, depth, PAST_LEN, 128), cache_a_v.reshape(nd, depth, PAST_LEN, 128),
              cache_c_k.reshape(nd, depth, PAST_LEN, 256), cache_c_v.reshape(nd, depth, PAST_LEN, 256),
              cache_d_ckv, cache_d_krope)
    expand = lambda s: (s.transpose(0, 1, 2, 4, 3)[:, :, :, :, None, :]
                        * eye4[None, None, :, None, :, None]).reshape(nd, depth, 256, 128)
    s0f_all, s0b_all = expand(state_b_fwd), expand(state_b_bwd)

    n1g, n2g = norm1_g[:, None, :], norm2_g[:, None, :]
    small = (aqg, akg, bd, wgate, bgate, dqg, wuq, wuk, d_kvnorm_g[:, None, :], tile)
    cb = conv_b[:, None, :]
    new_caches = None
    sfs, sbs = [], []
    x = (x_prompt, x_sample)
    for l in range(depth):
        lam_init = 0.8 - 0.6 * math.exp(-0.3 * l)
        fa, kva, qa, ha, new_caches = _inproj_calls(x, n, mod4, n1g, w_aq, w_rest, rope_tab, small, new_caches, l,
                                                    depth, n_ctx)
        oacd, ob, sf, sb = _ctx_mixers_call(fa, kva, qa, lamv, cong, wuv, bd, bong, tri, l, lam_init, n_ctx)
        oacd = _attn_lat_call(kva, qa, caches, lamv, cong, wuv, bd, tile, oacd, l, lam_init, n_ctx)
        ob = _gla_lat_call(fa, s0f_all, s0b_all, bong, bd, tri, ob, l, n_ctx)
        x1 = _merge_call(x, n, mod4, ha, w_gates, oacd, ob, wb, wout, l, n_ctx)
        res = _ffn_call(x1, mod4, n2g, wu, wg, cw, cb, wd, final_g[None], l, n_ctx, l == depth - 1)
        x = res[0]
        sfs.append(sf)
        sbs.append(sb)

    diag = lambda s: jnp.stack([s[:, :, hh * 64:(hh + 1) * 64, hh * 32:(hh + 1) * 32] for hh in range(4)],
                               axis=2).transpose(0, 1, 2, 4, 3)
    y_prompt, y_sample = res
    nak, nav, nck, ncv, nckv, nkr = new_caches
    return (y_prompt, y_sample, nak.reshape(nb, depth, SEQ, 2, 64), nav.reshape(nb, depth, SEQ, 2, 64),
            diag(jnp.stack(sfs, axis=1)), diag(jnp.stack(sbs, axis=1)), nck.reshape(nb, depth, SEQ, 4, 2, 32),
            ncv.reshape(nb, depth, SEQ, 4, 64), nckv, nkr)
```

```python
import functools
import math

import numpy as np
import jax
import jax.numpy as jnp
from jax import lax
from jax.experimental import pallas as pl
from jax.experimental.pallas import tpu as pltpu

F32 = jnp.float32
BF16 = jnp.bfloat16

D_MODEL = 1024
SEQ = 256
DEC_SEQ = 1024
PAST_LEN = 512
GRID_W = 64
ROPE_BASE = 10000.0
EPS = 1e-6
LOG2E = math.log2(math.e)
HEAD_W = 64
BRANCH_WIDTH = 256
GLA_KEY_W = 32
GLA_CHUNK = 64
GLA_SUB = 16
GLA_BLOCK = 256
GLA_GATE_NORM = 16.0
D_Q_RANK = 192
D_FF = 2816
FF_CHUNK = 256
N_MOD_ROWS = 16

ROW_TILE = 512
FFN_TILE = 1024
Q_TILE = 128
CTX_SEQ_PER_STEP = 4
LAT_SEQ_PER_GLA_STEP = 2
HALO = 16
VMEM_LIMIT = 56 * 1024 * 1024

IN_AQ_W = 256
IN_GATE_IN = (1280, 1312)
NZ = 2176
Z_AK, Z_AV, Z_BQ, Z_BK, Z_BV, Z_BR, Z_CQ, Z_CK, Z_CV, Z_DQ = 0, 128, 256, 384, 512, 768, 1024, 1280, 1536, 1792
GATES_OFF = 2432
F_W = 1024
F_BQ, F_BK, F_BV, F_BR, F_LGF, F_LGB = 0, 128, 256, 512, 768, 896
KV_W = 1024
KV_AK, KV_AV, KV_CK, KV_CV, KV_DKV, KV_KR = 0, 128, 256, 512, 768, 896
Q_W = 1152
Q_AQ, Q_CQ, Q_DLAT, Q_DQR = 0, 256, 512, 1024
ROPE_W = 768


def _dot(a, b):
    return jnp.dot(a, b, preferred_element_type=F32)


def _dot_nt(a, b):
    return lax.dot_general(a, b, (((1,), (1,)), ((), ())), preferred_element_type=F32)


def _dot_tn(a, b):
    return lax.dot_general(a, b, (((0,), (0,)), ((), ())), preferred_element_type=F32)


def _split3(a):
    hi = a.astype(BF16)
    r1 = a - hi.astype(F32)
    mid = r1.astype(BF16)
    lo = (r1 - mid.astype(F32)).astype(BF16)
    return hi, mid, lo


def _split_dot(a, b01):
    hi, mid, lo = _split3(a)
    return _dot(hi, b01) + _dot(mid, b01) + _dot(lo, b01)


def _group_rms(v, bd, width, gain):
    ss = _split_dot(v * v, bd)
    return v * lax.rsqrt(ss * (1.0 / width) + EPS) * gain


def _rope128(v, cos, sin_a, sin_b, shift):
    return v * cos + pltpu.roll(v, 128 - shift, 1) * sin_a + pltpu.roll(v, shift, 1) * sin_b


def _modulated_norm(x, gain, shift, scale):
    ms = jnp.mean(x * x, axis=-1, keepdims=True)
    return (x * lax.rsqrt(ms + EPS) * gain) * (1.0 + scale) + shift


def _lane_mask(width, lo, hi):
    lane = lax.broadcasted_iota(jnp.int32, (1, width), 1)
    return (lane >= lo) & (lane < hi)


def _const_spec(shape, grid_rank):
    zeros = tuple(0 for _ in shape)
    if grid_rank == 1:
        return pl.BlockSpec(shape, lambda i: zeros)
    return pl.BlockSpec(shape, lambda b, i: zeros)


def _layer_spec(shape, layer, grid_rank):
    idx = (layer,) + tuple(0 for _ in shape)
    once = pl.Buffered(1)
    if grid_rank == 1:
        return pl.BlockSpec((None,) + tuple(shape), lambda i: idx, pipeline_mode=once)
    return pl.BlockSpec((None,) + tuple(shape), lambda b, i: idx, pipeline_mode=once)


def _mod_kernel(cv_ref, w_ref, b_ref, o_ref):
    cv = cv_ref[...]
    a = cv * jax.nn.sigmoid(cv)
    o_ref[...] = _dot(a.astype(BF16), w_ref[...].astype(BF16)) + b_ref[...]


def _mod_call(cvec, w_mod, b_mod):
    depth = w_mod.shape[0]
    nblk = 4
    bw = 6 * D_MODEL // nblk
    return pl.pallas_call(
        _mod_kernel,
        grid=(depth, nblk),
        in_specs=[
            pl.BlockSpec((N_MOD_ROWS, D_MODEL), lambda l, j: (0, 0)),
            pl.BlockSpec((None, D_MODEL, bw), lambda l, j: (l, 0, j)),
            pl.BlockSpec((None, 1, bw), lambda l, j: (l, 0, j)),
        ],
        out_specs=pl.BlockSpec((None, N_MOD_ROWS, bw), lambda l, j: (l, 0, j)),
        out_shape=jax.ShapeDtypeStruct((depth, N_MOD_ROWS, 6 * D_MODEL), F32),
        compiler_params=pltpu.CompilerParams(vmem_limit_bytes=VMEM_LIMIT),
        name="mod_vectors",
    )(cvec, w_mod, b_mod.reshape(depth, 1, 6 * D_MODEL))


def _win_prep_kernel(w_ref, waq_ref, wrest_ref, wgates_ref):
    w = w_ref[...]
    c0, c1 = w[:, 0:128], w[:, 128:256]
    low = _lane_mask(128, 0, HEAD_W)
    waq_ref[:, 0:128] = jnp.where(low, c0, pltpu.roll(c1, HEAD_W, 1)).astype(BF16)
    waq_ref[:, 128:256] = jnp.where(low, pltpu.roll(c0, HEAD_W, 1), c1).astype(BF16)
    g0, g1 = IN_GATE_IN
    wrest_ref[:, 0:g0 - IN_AQ_W] = w[:, IN_AQ_W:g0].astype(BF16)
    slab = w[:, g0:GATES_OFF]
    wrest_ref[:, g0 - IN_AQ_W:NZ] = pltpu.roll(slab, slab.shape[1] - (g1 - g0), 1).astype(BF16)
    wgates_ref[...] = w[:, GATES_OFF:].astype(BF16)


def _win_prep_call(w_in):
    depth, d, width = w_in.shape
    rb = 256
    out = lambda w: pl.BlockSpec((None, rb, w), lambda l, i: (l, i, 0))
    return pl.pallas_call(
        _win_prep_kernel,
        grid=(depth, d // rb),
        in_specs=[pl.BlockSpec((None, rb, width), lambda l, i: (l, i, 0))],
        out_specs=[out(IN_AQ_W), out(NZ), out(width - GATES_OFF)],
        out_shape=[jax.ShapeDtypeStruct((depth, d, w), BF16) for w in (IN_AQ_W, NZ, width - GATES_OFF)],
        compiler_params=pltpu.CompilerParams(vmem_limit_bytes=VMEM_LIMIT),
        name="in_weight_layout",
    )(w_in)


def _inproj_body(x, mod_ref, n1g_ref, waq_ref, w_ref, rope_ref, aqg_ref, akg_ref, bd_ref, wgate_ref, bgate_ref,
                 dqg_ref, wuq_ref, wuk_ref, dkvg_ref, tile_ref, f_ref, kv_ref, q_ref, h_ref, cache_refs):
    mod = mod_ref[...]
    h = _modulated_norm(x, n1g_ref[...], mod[:, 0:D_MODEL], mod[:, D_MODEL:2 * D_MODEL]).astype(BF16)
    h_ref[...] = h
    z_aq = _dot(h, waq_ref[...])
    z = _dot(h, w_ref[...])
    zd = pltpu.roll(z[:, NZ - 256:NZ], 192, 1)
    bd = bd_ref[...]
    bd128 = bd[0:128, 0:128]
    if rope_ref is None:
        rope64 = rope32 = lambda v: v
    else:
        rope = rope_ref[...]
        cos64, sa64, sb64 = rope[:, 0:128], rope[:, 128:256], rope[:, 256:384]
        cos32, sa32, sb32 = rope[:, 384:512], rope[:, 512:640], rope[:, 640:768]
        rope64 = lambda v: _rope128(v, cos64, sa64, sb64, 16)
        rope32 = lambda v: _rope128(v, cos32, sa32, sb32, 8)

    def emit_kv(col, cache_idx, val, kv_val=None):
        kv_val = val if kv_val is None else kv_val
        kv_ref[:, col:col + kv_val.shape[1]] = kv_val.astype(BF16)
        if cache_refs is not None:
            ref = cache_refs[cache_idx]
            for j in range(val.shape[0] // SEQ):
                ref[j] = val[j * SEQ:(j + 1) * SEQ, 0:ref.shape[-1]]

    aq = _group_rms(z_aq, bd, HEAD_W, aqg_ref[...])
    a_scale = HEAD_W ** -0.5 * LOG2E
    q_ref[:, Q_AQ:Q_AQ + 128] = (rope64(aq[:, 0:128]) * a_scale).astype(BF16)
    q_ref[:, Q_AQ + 128:Q_AQ + 256] = (rope64(aq[:, 128:256]) * a_scale).astype(BF16)
    emit_kv(KV_AK, 0, rope64(_group_rms(z[:, Z_AK:Z_AK + 128], bd128, HEAD_W, akg_ref[...])))
    emit_kv(KV_AV, 1, z[:, Z_AV:Z_AV + 128])

    f_ref[:, F_BQ:F_BK] = z[:, Z_BQ:Z_BK] * (GLA_KEY_W ** -0.5)
    f_ref[:, F_BK:F_BV] = z[:, Z_BK:Z_BV]
    f_ref[:, F_BV:F_BR] = z[:, Z_BV:Z_BR]
    f_ref[:, F_BR:F_LGF] = z[:, Z_BR:Z_CQ]
    pre = _dot(zd[:, 128:256].astype(BF16), wgate_ref[...]) + bgate_ref[...]
    logsig = jnp.minimum(pre, 0.0) - jnp.log1p(jnp.exp(-jnp.abs(pre)))
    f_ref[:, F_LGF:F_W] = logsig / GLA_GATE_NORM

    c_scale = 32 ** -0.5 * LOG2E
    q_ref[:, Q_CQ:Q_CQ + 128] = (rope32(z[:, Z_CQ:Z_CQ + 128]) * c_scale).astype(BF16)
    q_ref[:, Q_CQ + 128:Q_CQ + 256] = (rope32(z[:, Z_CQ + 128:Z_CQ + 256]) * c_scale).astype(BF16)
    emit_kv(KV_CK, 2, jnp.concatenate([rope32(z[:, Z_CK:Z_CK + 128]), rope32(z[:, Z_CK + 128:Z_CK + 256])],
                                      axis=-1))
    emit_kv(KV_CV, 3, z[:, Z_CV:Z_DQ])

    dkv = zd[:, 0:128]
    ms = jnp.mean(dkv * dkv, axis=-1, keepdims=True)
    emit_kv(KV_DKV, 4, dkv * lax.rsqrt(ms + EPS) * dkvg_ref[...])
    kr = rope32(zd[:, 128:256])
    emit_kv(KV_KR, 5, kr, _dot(kr[:, 0:32].astype(BF16), tile_ref[...]))
    zq = jnp.where(_lane_mask(256, 0, D_Q_RANK), z[:, Z_DQ:Z_DQ + 256], 0.0)
    msq = jnp.sum(zq * zq, axis=-1, keepdims=True) * (1.0 / D_Q_RANK)
    qn = zq * lax.rsqrt(msq + EPS) * dqg_ref[...]
    dq = _dot(qn.astype(BF16), wuq_ref[...])
    d_scale = 96 ** -0.5 * LOG2E
    q_ref[:, Q_DLAT:Q_DQR] = (_dot(dq[:, 0:256].astype(BF16), wuk_ref[...]) * d_scale).astype(BF16)
    q_ref[:, Q_DQR:Q_W] = (rope32(dq[:, 256:384]) * d_scale).astype(BF16)


def _inproj_ctx_kernel(x_ref, mod_ref, n1g_ref, waq_ref, w_ref, aqg_ref, akg_ref, bd_ref, wgate_ref, bgate_ref,
                       dqg_ref, wuq_ref, wuk_ref, dkvg_ref, tile_ref, *rest):
    outs = rest[-10:]
    x = x_ref[...].reshape(-1, D_MODEL)
    _inproj_body(x, mod_ref, n1g_ref, waq_ref, w_ref, None, aqg_ref, akg_ref, bd_ref, wgate_ref, bgate_ref,
                 dqg_ref, wuq_ref, wuk_ref, dkvg_ref, tile_ref, outs[0], outs[1], outs[2], outs[3], outs[4:])


def _inproj_lat_kernel(x_ref, mod_ref, n1g_ref, waq_ref, w_ref, rope_ref, aqg_ref, akg_ref, bd_ref, wgate_ref,
                       bgate_ref, dqg_ref, wuq_ref, wuk_ref, dkvg_ref, tile_ref, pf_ref, pkv_ref, pq_ref, ph_ref,
                       f_ref, kv_ref, q_ref, h_ref):
    del pf_ref, pkv_ref, pq_ref, ph_ref
    _inproj_body(x_ref[...], mod_ref, n1g_ref, waq_ref, w_ref, rope_ref, aqg_ref, akg_ref, bd_ref, wgate_ref,
                 bgate_ref, dqg_ref, wuq_ref, wuk_ref, dkvg_ref, tile_ref, f_ref, kv_ref, q_ref, h_ref, None)


CACHE_WIDTHS = (128, 128, 256, 256, 128, 32)


def _inproj_calls(x, n, mod4, n1g, w_aq, w_rest, rope_tab, small, prev_caches, layer, depth, n_ctx_rows):
    tm = ROW_TILE
    ctx_tiles = n_ctx_rows // tm
    seq_tiles = DEC_SEQ // tm
    spt = tm // SEQ
    n_ctx_seq = n_ctx_rows // SEQ
    if isinstance(x, tuple):
        x_ctx, x_lat = x
        x_ctx_spec = pl.BlockSpec((spt, SEQ, D_MODEL), lambda i: (i, 0, 0))
        x_lat_spec = pl.BlockSpec((None, tm, D_MODEL), lambda i: (i // seq_tiles, i % seq_tiles, 0))
    else:
        x_ctx = x_lat = x
        x_ctx_spec = pl.BlockSpec((tm, D_MODEL), lambda i: (i, 0))
        x_lat_spec = pl.BlockSpec((tm, D_MODEL), lambda i: (i + ctx_tiles, 0))
    const = lambda shape: _layer_spec(shape, layer, 1)
    small_specs = [const((1, 256)), const((1, 128)), _const_spec((256, 256), 1), const((128, 256)),
                   const((1, 256)), const((1, 256)), const((256, 384)), const((256, 512)), const((1, 128)),
                   _const_spec((32, 128), 1)]
    act_shapes = [jax.ShapeDtypeStruct((n, F_W), F32), jax.ShapeDtypeStruct((n, KV_W), BF16),
                  jax.ShapeDtypeStruct((n, Q_W), BF16), jax.ShapeDtypeStruct((n, D_MODEL), BF16)]
    act_specs = lambda off: [pl.BlockSpec((tm, w), lambda i: (i + off, 0)) for w in (F_W, KV_W, Q_W, D_MODEL)]
    n_act = len(act_shapes)
    cparams = pltpu.CompilerParams(vmem_limit_bytes=VMEM_LIMIT)
    any_spec = pl.BlockSpec(memory_space=pl.ANY)

    n_in = 5 + len(small)
    alias = {} if prev_caches is None else {n_in + k: n_act + k for k in range(6)}
    res = pl.pallas_call(
        _inproj_ctx_kernel,
        grid=(ctx_tiles,),
        in_specs=[
            x_ctx_spec,
            pl.BlockSpec((None, None, 1, 6 * D_MODEL), lambda i: (layer, 0, 0, 0)),
            const((1, D_MODEL)),
            const((D_MODEL, IN_AQ_W)),
            const((D_MODEL, NZ)),
        ] + small_specs + ([] if prev_caches is None else [any_spec] * 6),
        out_specs=act_specs(0) + [pl.BlockSpec((spt, None, SEQ, w), lambda i: (i, layer, 0, 0))
                                  for w in CACHE_WIDTHS],
        out_shape=act_shapes + [jax.ShapeDtypeStruct((n_ctx_seq, depth, SEQ, w), F32) for w in CACHE_WIDTHS],
        input_output_aliases=alias,
        compiler_params=cparams,
        name="in_projection_context",
    )(x_ctx, mod4, n1g, w_aq, w_rest, *small, *([] if prev_caches is None else prev_caches))
    acts_c, caches = res[:n_act], tuple(res[n_act:])

    fa, kva, qa, ha = pl.pallas_call(
        _inproj_lat_kernel,
        grid=((n - n_ctx_rows) // tm,),
        in_specs=[
            x_lat_spec,
            pl.BlockSpec((None, None, 1, 6 * D_MODEL), lambda i: (layer, 1 + i // seq_tiles, 0, 0)),
            const((1, D_MODEL)),
            const((D_MODEL, IN_AQ_W)),
            const((D_MODEL, NZ)),
            pl.BlockSpec((tm, ROPE_W), lambda i: (i % seq_tiles, 0)),
        ] + small_specs + [any_spec] * n_act,
        out_specs=act_specs(ctx_tiles),
        out_shape=act_shapes,
        input_output_aliases={n_in + 1 + k: k for k in range(n_act)},
        compiler_params=cparams,
        name="in_projection_latent",
    )(x_lat, mod4, n1g, w_aq, w_rest, rope_tab, *small, *acts_c)
    return fa, kva, qa, ha, caches


def _exp2_parts(parts):
    m = parts[0].max(axis=-1, keepdims=True)
    for s in parts[1:]:
        m = jnp.maximum(m, s.max(axis=-1, keepdims=True))
    return [jnp.exp2(s - m) for s in parts]


def _value_variants(av, cv, dk):
    one = jnp.ones((), BF16)
    av1 = [jnp.where(_lane_mask(128, hh * HEAD_W, (hh + 1) * HEAD_W), av, one) for hh in range(2)]
    cv1 = [jnp.where(_lane_mask(256, hh * HEAD_W, (hh + 1) * HEAD_W), cv, one) for hh in range(4)]
    vlat = jnp.where(_lane_mask(256, 0, 128), dk, one)
    return av1, cv1, vlat


def _run(phases):
    for _ in phases:
        pass


def _interleave(*phase_generators):
    live = list(phase_generators)
    while live:
        live = [g for g in live if next(g, StopIteration) is not StopIteration]


def _attn_core(*args):
    _run(_attn_phases(*args))


def _attn_phases(q, ak, av1, ck, cv1, dk, vlat, lam_ref, cong_ref, wuv_ref, bd_ref, o_ref, lam_init):
    zero = jnp.zeros((), BF16)
    ak, ck, dk = [ak], [ck], [dk]
    tq = q.shape[0]

    masks_a = [_lane_mask(128, hh * HEAD_W, (hh + 1) * HEAD_W) for hh in range(2)]
    qa = jnp.concatenate([jnp.where(masks_a[hh], q[:, Q_AQ + g * 128:Q_AQ + (g + 1) * 128], zero)
                          for hh in range(2) for g in range(2)], axis=0)
    qcs = [jnp.concatenate([jnp.where(_lane_mask(128, u * 32, (u + 1) * 32),
                                      q[:, Q_CQ + c * 128:Q_CQ + (c + 1) * 128], zero) for u in range(4)], axis=0)
           for c in range(2)]
    qd = jnp.concatenate(
        [jnp.concatenate([q[:, Q_DLAT + hh * 128:Q_DLAT + (hh + 1) * 128],
                          jnp.where(_lane_mask(128, hh * 32, (hh + 1) * 32), q[:, Q_DQR:Q_W], zero)], axis=-1)
         for hh in range(4)], axis=0)
    s_a = [_dot_nt(qa, k) for k in ak]
    s_c = [[_dot_nt(qcs[c], k[:, c * 128:(c + 1) * 128]) for k in ck] for c in range(2)]
    yield

    ps = [p.astype(BF16) for p in _exp2_parts(s_a)]
    s_d = [_dot_nt(qd, k) for k in dk]
    oa = [None, None]
    for hh in range(2):
        rows = slice(hh * 2 * tq, (hh + 1) * 2 * tq)
        o = _dot(ps[0][rows], av1[hh])
        o = jnp.where(masks_a[hh], o / pltpu.roll(o, HEAD_W, 1), 0.0)
        for g in range(2):
            part = o[g * tq:(g + 1) * tq]
            oa[g] = part if oa[g] is None else oa[g] + part
    o_ref[:, 0:128] = oa[0]
    o_ref[:, 128:256] = oa[1]
    yield

    lv = lam_ref[...]
    lam = (jnp.exp(jnp.sum(lv[0:1] * lv[1:2], axis=-1, keepdims=True))
           - jnp.exp(jnp.sum(lv[2:3] * lv[3:4], axis=-1, keepdims=True)) + lam_init)
    oc = []
    for c in range(2):
        c0 = c * 128
        ps = [p.astype(BF16) for p in _exp2_parts(s_c[c])]
        och = None
        for h2 in range(2):
            hh = c * 2 + h2
            rows = slice(h2 * 2 * tq, (h2 + 1) * 2 * tq)
            o = _dot(ps[0][rows], cv1[hh])
            res = o[:, c0:c0 + 128] / o[:, 128 - c0:256 - c0]
            o = jnp.where(_lane_mask(128, h2 * HEAD_W, (h2 + 1) * HEAD_W), res[0:tq] - lam * res[tq:2 * tq], 0.0)
            och = o if och is None else och + o
            yield
        oc.append(och)
    oc = jnp.concatenate(oc, axis=-1)
    o_ref[:, 256:512] = _group_rms(oc, bd_ref[...], HEAD_W, cong_ref[...]) * (1.0 - lam_init)

    ps = [p.astype(BF16) for p in _exp2_parts(s_d)]
    o = _dot(ps[0], vlat)
    lat = (o[:, 0:128] / o[:, 128:256]).astype(BF16)
    o_ref[:, 512:768] = _dot(jnp.concatenate([lat[hh * tq:(hh + 1) * tq] for hh in range(4)], axis=-1),
                             wuv_ref[...])


def _attn_lat_kernel(q_ref, kva_ref, ck_ref, cv_ref, dkv_ref, cak_ref, cav_ref, cck_ref, ccv_ref,
                     cckv_ref, ckr_ref, lam_ref, cong_ref, wuv_ref, bd_ref, tile_ref, prev_ref, o_ref,
                     ak_s, ck_s, dk_s, av1_s, cv1_s, vlat_s, *, lam_init):
    del prev_ref

    @pl.when(pl.program_id(1) == 0)
    def _():
        p = PAST_LEN
        kva = kva_ref[...]
        kr4 = _dot(ckr_ref[...].astype(BF16), tile_ref[...]).astype(BF16)
        pieces = ((cak_ref[...].astype(BF16), cav_ref[...].astype(BF16), cck_ref[...].astype(BF16),
                   ccv_ref[...].astype(BF16), jnp.concatenate([cckv_ref[...].astype(BF16), kr4], axis=-1)),
                  (kva[:, 0:128], kva[:, 128:256], ck_ref[...], cv_ref[...], dkv_ref[...]))
        for rows, (ak, av, ck, cv, dk) in zip((slice(0, p), slice(p, p + DEC_SEQ)), pieces):
            av1, cv1, vlat = _value_variants(av, cv, dk)
            ak_s[rows] = ak
            ck_s[rows] = ck
            dk_s[rows] = dk
            vlat_s[rows] = vlat
            for hh in range(2):
                av1_s[hh, rows] = av1[hh]
            for hh in range(4):
                cv1_s[hh, rows] = cv1[hh]

    _attn_core(q_ref[...], ak_s[...], [av1_s[hh] for hh in range(2)], ck_s[...],
               [cv1_s[hh] for hh in range(4)], dk_s[...], vlat_s[...], lam_ref, cong_ref, wuv_ref, bd_ref,
               o_ref, lam_init)


def _attn_lat_call(kva, qa, caches, lamv, cong, wuv, bd, tile, o_ctx, layer, lam_init, n_ctx_rows):
    n = kva.shape[0]
    n_lat_seq = (n - n_ctx_rows) // DEC_SEQ
    param_specs = lambda rank: [_layer_spec((4, 32), layer, rank), _layer_spec((1, 256), layer, rank),
                                _layer_spec((512, 256), layer, rank), _const_spec((256, 256), rank),
                                _const_spec((32, 128), rank)]
    qt = Q_TILE
    q_per_seq = DEC_SEQ // qt
    q0 = n_ctx_rows // qt
    s0 = n_ctx_rows // DEC_SEQ
    cak, cav, cck, ccv, cckv, ckr = caches
    cache_spec = lambda w: pl.BlockSpec((None, None, PAST_LEN, w), lambda b, i: (b, layer, 0, 0))
    seq_cols = lambda idx: pl.BlockSpec((DEC_SEQ, 256), lambda b, i: (s0 + b, idx))
    n_keys = PAST_LEN + DEC_SEQ
    return pl.pallas_call(
        functools.partial(_attn_lat_kernel, lam_init=lam_init),
        grid=(n_lat_seq, q_per_seq),
        in_specs=[
            pl.BlockSpec((qt, Q_W), lambda b, i: (q0 + b * q_per_seq + i, 0)),
            seq_cols(KV_AK // 256), seq_cols(KV_CK // 256), seq_cols(KV_CV // 256), seq_cols(KV_DKV // 256),
            cache_spec(128), cache_spec(128), cache_spec(256), cache_spec(256), cache_spec(128),
            cache_spec(32),
        ] + param_specs(2) + [pl.BlockSpec(memory_space=pl.ANY)],
        out_specs=pl.BlockSpec((qt, 3 * BRANCH_WIDTH), lambda b, i: (q0 + b * q_per_seq + i, 0)),
        out_shape=jax.ShapeDtypeStruct((n, 3 * BRANCH_WIDTH), F32),
        scratch_shapes=[pltpu.VMEM((n_keys, 128), BF16), pltpu.VMEM((n_keys, 256), BF16),
                        pltpu.VMEM((n_keys, 256), BF16), pltpu.VMEM((2, n_keys, 128), BF16),
                        pltpu.VMEM((4, n_keys, 256), BF16), pltpu.VMEM((n_keys, 256), BF16)],
        input_output_aliases={16: 0},
        compiler_params=pltpu.CompilerParams(vmem_limit_bytes=VMEM_LIMIT),
        name="attention_latent",
    )(qa, kva, kva, kva, kva, cak, cav, cck, ccv, cckv, ckr, lamv, cong, wuv, bd, tile, o_ctx)


def _gla_chunk(q, k, v, bc, st, reverse, masks, result):
    c, sub, nsub = GLA_CHUNK, GLA_SUB, GLA_CHUNK // GLA_SUB
    keep, qmask, src_row, blk = masks
    edge = bc[0:1] if reverse else bc[c - 1:c]
    qs, kd = [], []
    for i in range(nsub):
        r0 = i * sub
        ref = bc[r0 + sub - 1:r0 + sub] if reverse else bc[r0:r0 + 1]
        qd = q[r0:r0 + sub] * jnp.exp(bc[r0:r0 + sub] - ref)
        qs.append(jnp.where(qmask, jnp.concatenate([qd] * 4, axis=0), 0.0))
        in_range = (src_row >= r0) if reverse else (src_row < r0 + sub)
        kd.append(k * jnp.exp(jnp.where(in_range, ref - bc, 0.0)))
    att = _dot_nt(jnp.concatenate(qs, axis=0).astype(BF16), jnp.concatenate(kd, axis=0).astype(BF16))
    vb = v.astype(BF16)
    q_dec = q * jnp.exp(bc)
    k_dec = k * jnp.exp(edge - bc)
    o_inter = _dot_nt(q_dec.astype(BF16), st.astype(BF16))
    upd = _dot_tn(vb, k_dec.astype(BF16))
    yield
    att = jnp.where(keep, att, 0.0).astype(BF16)
    pv = _dot(att, jnp.concatenate([vb] * nsub, axis=0))
    st_new = st * jnp.exp(edge) + jnp.where(blk, upd, 0.0)
    yield
    out_blocks = []
    for i in range(nsub):
        o = None
        for hh in range(4):
            r0 = i * c + hh * sub
            part = jnp.where(_lane_mask(256, hh * HEAD_W, (hh + 1) * HEAD_W), pv[r0:r0 + sub], 0.0)
            o = part if o is None else o + part
        out_blocks.append(o)
    o_intra = jnp.concatenate(out_blocks, axis=0)
    result.extend([o_intra + o_inter, st_new])


def _gla_masks(reverse):
    c, sub = GLA_CHUNK, GLA_SUB
    row = lax.broadcasted_iota(jnp.int32, (4 * c, 4 * c), 0)
    col = lax.broadcasted_iota(jnp.int32, (4 * c, 4 * c), 1)
    t = (row // c) * sub + row % sub
    s = col % c
    keep = (row // c == col // c) & ((s >= t) if reverse else (s <= t))
    qmask = (lax.broadcasted_iota(jnp.int32, (4 * sub, 128), 0) // sub
             == lax.broadcasted_iota(jnp.int32, (4 * sub, 128), 1) // GLA_KEY_W)
    src_row = lax.broadcasted_iota(jnp.int32, (c, 1), 0)
    blk = (lax.broadcasted_iota(jnp.int32, (256, 128), 0) // HEAD_W
           == lax.broadcasted_iota(jnp.int32, (256, 128), 1) // GLA_KEY_W)
    return keep, qmask, src_row, blk


def _gla_parts(qk_ref, v_ref, r_ref, lg_ref, g_ref, bd_ref, tri_ref, o_ref, accf_ref, accb_ref, bcf_ref, bcb_ref,
               seq_len):
    nc = seq_len // GLA_CHUNK
    masks_f, masks_b = _gla_masks(False), _gla_masks(True)

    def decays():
        for r0 in range(0, seq_len, GLA_BLOCK):
            lg = lg_ref[r0:r0 + GLA_BLOCK, :]
            for d, bc_ref in ((0, bcf_ref), (1, bcb_ref)):
                res = _dot(tri_ref[d], jnp.concatenate(_split3(lg[:, d * 128:(d + 1) * 128]), axis=-1))
                bc_ref[r0:r0 + GLA_BLOCK, :] = res[:, 0:128] + res[:, 128:256] + res[:, 256:384]

    def chunk_rows(c):
        start = c * GLA_CHUNK
        return pl.ds(start if isinstance(c, int) else pl.multiple_of(start, GLA_CHUNK), GLA_CHUNK)

    def pair_phases(ci, carry, new_carry):
        rf, rb = chunk_rows(ci), chunk_rows(nc - 1 - ci)
        res_f, res_b = [], []
        live = [_gla_chunk(qk_ref[rf, 0:128], qk_ref[rf, 128:256], v_ref[rf, :], bcf_ref[rf, :], carry[0], False,
                           masks_f, res_f),
                _gla_chunk(qk_ref[rb, 0:128], qk_ref[rb, 128:256], v_ref[rb, :], bcb_ref[rb, :], carry[1], True,
                           masks_b, res_b)]
        while live:
            live = [g for g in live if next(g, StopIteration) is not StopIteration]
            yield
        accf_ref[rf, :] = res_f[0]
        accb_ref[rb, :] = res_b[0]
        new_carry[:] = [res_f[1], res_b[1]]

    def finish():
        r = r_ref[...]
        o_ref[...] = (_group_rms(accf_ref[...] + accb_ref[...], bd_ref[...], HEAD_W, g_ref[...])
                      * (r * jax.nn.sigmoid(r)))

    return decays, pair_phases, finish


def _gla_seq_phases(parts, n_pairs, carry, states):
    decays, pair_phases, finish = parts
    decays()
    yield
    for ci in range(n_pairs):
        nxt = []
        yield from pair_phases(ci, carry, nxt)
        carry = tuple(nxt)
    finish()
    states.extend(carry)


def _ctx_mixers_kernel(q_ref, kva_ref, ck_ref, cv_ref, dkv_ref, qk_ref, v_ref, r_ref, lg_ref, lam_ref, cong_ref,
                       wuv_ref, bd_ref, bong_ref, tri_ref, o_ref, ob_ref, sf_ref, sb_ref, *scratch, lam_init):
    zero = jnp.zeros((256, 128), F32)
    nseq = q_ref.shape[0] // SEQ
    gens, states = [], [[] for _ in range(nseq)]
    for j in range(nseq):
        rows = slice(j * SEQ, (j + 1) * SEQ)
        kva = kva_ref[rows, :]
        dk = dkv_ref[rows, :]
        av1, cv1, vlat = _value_variants(kva[:, 128:256], cv_ref[rows, :], dk)
        gens.append(_attn_phases(q_ref[rows, :], kva[:, 0:128], av1, ck_ref[rows, :], cv1, dk, vlat, lam_ref,
                                 cong_ref, wuv_ref, bd_ref, o_ref.at[rows, :], lam_init))
        parts = _gla_parts(qk_ref.at[rows, :], v_ref.at[rows, :], r_ref.at[rows, :], lg_ref.at[rows, :], bong_ref,
                           bd_ref, tri_ref, ob_ref.at[rows, :], *[s.at[rows, :] for s in scratch], SEQ)
        gens.append(_gla_seq_phases(parts, SEQ // GLA_CHUNK, (zero, zero), states[j]))
    _interleave(*gens)
    for j in range(nseq):
        sf_ref[j], sb_ref[j] = states[j]


def _ctx_mixers_call(fa, kva, qa, lamv, cong, wuv, bd, bong, tri, layer, lam_init, n_ctx_rows):
    n = fa.shape[0]
    n_ctx_seq = n_ctx_rows // SEQ
    nseq = CTX_SEQ_PER_STEP
    crows = nseq * SEQ
    const = lambda shape: _const_spec(shape, 1)
    lspec = lambda shape: _layer_spec(shape, layer, 1)
    col = lambda w, off: pl.BlockSpec((crows, w), lambda b: (b, off // w))
    st_shape = jax.ShapeDtypeStruct((n_ctx_seq, 256, 128), F32)
    st_spec = pl.BlockSpec((nseq, 256, 128), lambda b: (b, 0, 0))
    return pl.pallas_call(
        functools.partial(_ctx_mixers_kernel, lam_init=lam_init),
        grid=(n_ctx_seq // nseq,),
        in_specs=[
            col(Q_W, 0), col(256, KV_AK), col(256, KV_CK), col(256, KV_CV), col(256, KV_DKV),
            col(256, F_BQ), col(256, F_BV), col(256, F_BR), col(256, F_LGF),
            lspec((4, 32)), lspec((1, 256)), lspec((512, 256)), const((256, 256)), lspec((1, 256)),
            const((2, GLA_BLOCK, GLA_BLOCK)),
        ],
        out_specs=[col(3 * BRANCH_WIDTH, 0), col(BRANCH_WIDTH, 0), st_spec, st_spec],
        out_shape=[jax.ShapeDtypeStruct((n, 3 * BRANCH_WIDTH), F32), jax.ShapeDtypeStruct((n, BRANCH_WIDTH), F32),
                   st_shape, st_shape],
        scratch_shapes=[pltpu.VMEM((crows, BRANCH_WIDTH), F32), pltpu.VMEM((crows, BRANCH_WIDTH), F32),
                        pltpu.VMEM((crows, 128), F32), pltpu.VMEM((crows, 128), F32)],
        compiler_params=pltpu.CompilerParams(vmem_limit_bytes=VMEM_LIMIT),
        name="mixers_context",
    )(qa, kva, kva, kva, kva, fa, fa, fa, fa, lamv, cong, wuv, bd, bong, tri)


def _gla_lat_kernel(qk_ref, v_ref, r_ref, lg_ref, s0f_ref, s0b_ref, g_ref, bd_ref, tri_ref, prev_ref,
                    o_ref, *scratch):
    del prev_ref
    nseq = qk_ref.shape[0] // DEC_SEQ
    parts = []
    for j in range(nseq):
        rows = slice(j * DEC_SEQ, (j + 1) * DEC_SEQ)
        parts.append(_gla_parts(qk_ref.at[rows, :], v_ref.at[rows, :], r_ref.at[rows, :], lg_ref.at[rows, :], g_ref,
                                bd_ref, tri_ref, o_ref.at[rows, :], *[s.at[rows, :] for s in scratch], DEC_SEQ))
    for decays, _, _ in parts:
        decays()

    def step(ci, carry):
        nxt = [[] for _ in range(nseq)]
        _interleave(*[parts[j][1](ci, carry[2 * j:2 * j + 2], nxt[j]) for j in range(nseq)])
        return tuple(s for pair in nxt for s in pair)

    init = tuple(s for j in range(nseq) for s in (s0f_ref[j], s0b_ref[j]))
    lax.fori_loop(0, DEC_SEQ // GLA_CHUNK, step, init)
    for _, _, finish in parts:
        finish()


def _gla_lat_call(fa, s0f_t, s0b_t, bong, bd, tri, o_ctx, layer, n_ctx_rows):
    n = fa.shape[0]
    n_lat_seq = (n - n_ctx_rows) // DEC_SEQ
    const = lambda shape: _const_spec(shape, 1)
    nseq = LAT_SEQ_PER_GLA_STEP
    rows = nseq * DEC_SEQ
    state_spec = pl.BlockSpec((nseq, None, 256, 128), lambda b: (b, layer, 0, 0))
    s0 = n_ctx_rows // rows
    seq_cols = lambda off: pl.BlockSpec((rows, 256), lambda b: (s0 + b, off // 256))
    return pl.pallas_call(
        _gla_lat_kernel,
        grid=(n_lat_seq // nseq,),
        in_specs=[
            seq_cols(F_BQ), seq_cols(F_BV), seq_cols(F_BR), seq_cols(F_LGF),
            state_spec, state_spec,
            _layer_spec((1, 256), layer, 1), const((256, 256)), const((2, GLA_BLOCK, GLA_BLOCK)),
            pl.BlockSpec(memory_space=pl.ANY),
        ],
        out_specs=pl.BlockSpec((rows, BRANCH_WIDTH), lambda b: (s0 + b, 0)),
        out_shape=jax.ShapeDtypeStruct((n, BRANCH_WIDTH), F32),
        scratch_shapes=[pltpu.VMEM((rows, BRANCH_WIDTH), F32), pltpu.VMEM((rows, BRANCH_WIDTH), F32),
                        pltpu.VMEM((rows, 128), F32), pltpu.VMEM((rows, 128), F32)],
        input_output_aliases={9: 0},
        compiler_params=pltpu.CompilerParams(vmem_limit_bytes=VMEM_LIMIT),
        name="gla_latent",
    )(fa, fa, fa, fa, s0f_t, s0b_t, bong, bd, tri, o_ctx)


def _merge_kernel(*refs, ctx_tiles):
    mod_ref, h_ref, wg_ref, oacd_ref, ob_ref, wb_ref, wout_ref, o_ref = refs[-8:]
    if len(refs) == 9:
        x = refs[0][...]
    else:
        x = jnp.where(pl.program_id(0) < ctx_tiles, refs[0][...].reshape(-1, D_MODEL), refs[1][...])
    mod = mod_ref[...]
    h = h_ref[...]
    oacd = oacd_ref[...]
    branches = [oacd[:, 0:256], ob_ref[...], oacd[:, 256:512], oacd[:, 512:768]]
    acc = None
    for j in range(4):
        gate = jax.nn.sigmoid(_dot(h, wg_ref[:, j * D_MODEL:(j + 1) * D_MODEL]))
        term = gate * _dot(branches[j].astype(BF16), wb_ref[j])
        acc = term if acc is None else acc + term
    mix = _dot(acc.astype(BF16), wout_ref[...])
    o_ref[...] = x + mod[:, 2 * D_MODEL:3 * D_MODEL] * mix


def _mod_map(layer, ctx_tiles, seq_tiles):
    return lambda i: (layer, jnp.where(i < ctx_tiles, 0, 1 + (i - ctx_tiles) // seq_tiles), 0, 0)


def _merge_call(x, n, mod4, ha, wgates, oacd, ob, wb, wout, layer, n_ctx_rows):
    tm = ROW_TILE
    ctx_tiles = n_ctx_rows // tm
    seq_tiles = DEC_SEQ // tm
    const = lambda shape: _layer_spec(shape, layer, 1)
    if isinstance(x, tuple):
        xs = list(x)
        x_specs = [
            pl.BlockSpec((tm // SEQ, SEQ, D_MODEL), lambda i: (jnp.minimum(i, ctx_tiles - 1), 0, 0)),
            pl.BlockSpec((None, tm, D_MODEL), lambda i: (jnp.maximum(i - ctx_tiles, 0) // seq_tiles,
                                                         jnp.maximum(i - ctx_tiles, 0) % seq_tiles, 0)),
        ]
    else:
        xs = [x]
        x_specs = [pl.BlockSpec((tm, D_MODEL), lambda i: (i, 0))]
    return pl.pallas_call(
        functools.partial(_merge_kernel, ctx_tiles=ctx_tiles),
        grid=(n // tm,),
        in_specs=x_specs + [
            pl.BlockSpec((None, None, 1, 6 * D_MODEL), _mod_map(layer, ctx_tiles, seq_tiles)),
            pl.BlockSpec((tm, D_MODEL), lambda i: (i, 0)),
            const((D_MODEL, 4 * D_MODEL)),
            pl.BlockSpec((tm, 3 * BRANCH_WIDTH), lambda i: (i, 0)),
            pl.BlockSpec((tm, BRANCH_WIDTH), lambda i: (i, 0)),
            const((4, BRANCH_WIDTH, D_MODEL)),
            const((D_MODEL, D_MODEL)),
        ],
        out_specs=pl.BlockSpec((tm, D_MODEL), lambda i: (i, 0)),
        out_shape=jax.ShapeDtypeStruct((n, D_MODEL), F32),
        compiler_params=pltpu.CompilerParams(vmem_limit_bytes=VMEM_LIMIT),
        name="merge",
    )(*xs, mod4, ha, wgates, oacd, ob, wb, wout)


def _ffn_kernel(x_ref, xp_ref, xn_ref, mod_ref, n2g_ref, wu_ref, wg_ref, cw_ref, cb_ref, wd_ref, fg_ref,
                *rest, ctx_tiles, final):
    out_refs, (hext_ref, act_ref) = rest[:-2], rest[-2:]
    tm = x_ref.shape[0]
    i = pl.program_id(0)
    x = x_ref[...]
    mod = mod_ref[...]
    sh2, sc2, g2 = mod[:, 3 * D_MODEL:4 * D_MODEL], mod[:, 4 * D_MODEL:5 * D_MODEL], mod[:, 5 * D_MODEL:]
    n2g = n2g_ref[...]
    hext_ref[0:HALO] = _modulated_norm(xp_ref[...], n2g, sh2, sc2).astype(BF16)
    hext_ref[HALO:HALO + tm] = _modulated_norm(x, n2g, sh2, sc2).astype(BF16)
    hext_ref[HALO + tm:] = _modulated_norm(xn_ref[...], n2g, sh2, sc2).astype(BF16)
    seq_len = jnp.where(i < ctx_tiles, SEQ, DEC_SEQ)
    row = lax.broadcasted_iota(jnp.int32, (tm, 1), 0)
    pos = (row + i * tm) & (seq_len - 1)
    first = pos == 0
    last = pos == seq_len - 1
    for c0 in range(0, D_FF, FF_CHUNK):
        u_ext = _dot(hext_ref[...], wu_ref[:, c0:c0 + FF_CHUNK])
        u = u_ext[HALO:HALO + tm]
        u_dn = jnp.where(first, 0.0, u_ext[HALO - 1:HALO - 1 + tm])
        u_up = jnp.where(last, 0.0, u_ext[HALO + 1:HALO + 1 + tm])
        cw = cw_ref[:, c0:c0 + FF_CHUNK]
        uc = u_dn * cw[0:1] + u * cw[1:2] + u_up * cw[2:3] + cb_ref[:, c0:c0 + FF_CHUNK]
        g = _dot(hext_ref[HALO:HALO + tm], wg_ref[:, c0:c0 + FF_CHUNK])
        act_ref[:, c0:c0 + FF_CHUNK] = (jax.nn.gelu(uc) * g).astype(BF16)
    out = x + g2 * _dot(act_ref[...], wd_ref[...])
    if final:
        yp_ref, ys_ref = out_refs
        ms = jnp.mean(out * out, axis=-1, keepdims=True)
        y = out * lax.rsqrt(ms + EPS) * fg_ref[...]

        @pl.when(i < ctx_tiles)
        def _():
            for j in range(tm // SEQ):
                yp_ref[j] = y[j * SEQ:(j + 1) * SEQ]

        @pl.when(i >= ctx_tiles)
        def _():
            ys_ref[...] = y
    else:
        out_refs[0][...] = out


def _ffn_call(x, mod4, n2g, wu, wg, cw, cb, wd, fg, layer, n_ctx_rows, final):
    n = x.shape[0]
    tm = FFN_TILE
    ctx_tiles = n_ctx_rows // tm
    seq_tiles = DEC_SEQ // tm
    hb = tm // HALO
    last_halo = n // HALO - 1
    lspec = lambda shape: _layer_spec(shape, layer, 1)
    if final:
        spt = tm // SEQ
        out_specs = [
            pl.BlockSpec((spt, SEQ, D_MODEL), lambda i: (jnp.minimum(i, ctx_tiles - 1), 0, 0)),
            pl.BlockSpec((None, tm, D_MODEL), lambda i: (jnp.maximum(i - ctx_tiles, 0) // seq_tiles,
                                                         jnp.maximum(i - ctx_tiles, 0) % seq_tiles, 0)),
        ]
        out_shape = [jax.ShapeDtypeStruct((n_ctx_rows // SEQ, SEQ, D_MODEL), F32),
                     jax.ShapeDtypeStruct(((n - n_ctx_rows) // DEC_SEQ, DEC_SEQ, D_MODEL), F32)]
    else:
        out_specs = [pl.BlockSpec((tm, D_MODEL), lambda i: (i, 0))]
        out_shape = [jax.ShapeDtypeStruct((n, D_MODEL), F32)]
    return pl.pallas_call(
        functools.partial(_ffn_kernel, ctx_tiles=ctx_tiles, final=final),
        grid=(n // tm,),
        in_specs=[
            pl.BlockSpec((tm, D_MODEL), lambda i: (i, 0)),
            pl.BlockSpec((HALO, D_MODEL), lambda i: (jnp.maximum(i * hb - 1, 0), 0)),
            pl.BlockSpec((HALO, D_MODEL), lambda i: (jnp.minimum((i + 1) * hb, last_halo), 0)),
            pl.BlockSpec((None, None, 1, 6 * D_MODEL), _mod_map(layer, ctx_tiles, seq_tiles)),
            lspec((1, D_MODEL)),
            lspec((D_MODEL, D_FF)), lspec((D_MODEL, D_FF)), lspec((8, D_FF)), lspec((1, D_FF)),
            lspec((D_FF, D_MODEL)), _const_spec((1, D_MODEL), 1),
        ],
        out_specs=out_specs,
        out_shape=out_shape,
        scratch_shapes=[pltpu.VMEM((tm + 2 * HALO, D_MODEL), BF16), pltpu.VMEM((tm, D_FF), BF16)],
        compiler_params=pltpu.CompilerParams(vmem_limit_bytes=VMEM_LIMIT),
        name="conv_ffn",
    )(x, x, x, mod4, n2g, wu, wg, cw, cb, wd, fg)


def _rope_tables():
    t = np.arange(DEC_SEQ)
    row = (t // GRID_W).astype(np.float32)
    col = (t % GRID_W).astype(np.float32)

    def tables(head_dim):
        r = head_dim // 2
        half = r // 2
        inv = (np.float32(ROPE_BASE) ** (-np.arange(half, dtype=np.float32) / np.float32(half))).astype(np.float32)
        lane = np.arange(128) % head_dim
        use_col = lane >= r
        e = lane % r
        first_half = e < half
        freq = inv[e % half]
        pos = np.where(use_col[None, :], col[:, None], row[:, None]).astype(np.float32)
        ang = (pos * freq[None, :]).astype(np.float32)
        cos = np.cos(ang).astype(np.float32)
        sin = np.sin(ang).astype(np.float32)
        sin_a = np.where(first_half[None, :], -sin, 0.0).astype(np.float32)
        sin_b = np.where(first_half[None, :], 0.0, sin).astype(np.float32)
        return cos, sin_a, sin_b

    return jnp.asarray(np.concatenate(list(tables(64)) + list(tables(32)), axis=1))


def kernel(x_prompt, x_sample, c, cache_a_k, cache_a_v, state_b_fwd, state_b_bwd, cache_c_k, cache_c_v, cache_d_ckv, cache_d_krope, c_ctx, w_mod, b_mod, norm1_g, norm2_g, w_in, a_qnorm_g, a_knorm_g, b_gate_w_fwd, b_gate_b_fwd, b_gate_w_bwd, b_gate_b_bwd, b_onorm_g, c_lq1, c_lk1, c_lq2, c_lk2, c_onorm_g, d_qnorm_g, d_w_uq, d_kvnorm_g, d_w_ukv, w_branch, w_out, w_ffu, w_ffg, conv_w, conv_b, w_ffd, final_g):
    depth = w_in.shape[0]
    nb, nd = x_prompt.shape[0], x_sample.shape[0]
    n_ctx = nb * SEQ
    assert n_ctx % (LAT_SEQ_PER_GLA_STEP * DEC_SEQ) == 0 and nd % LAT_SEQ_PER_GLA_STEP == 0
    assert nb % CTX_SEQ_PER_STEP == 0 and n_ctx % FFN_TILE == 0 and nd + 1 <= N_MOD_ROWS

    n = n_ctx + nd * DEC_SEQ
    cvec = jnp.zeros((N_MOD_ROWS, D_MODEL), F32).at[0].set(c_ctx).at[1:1 + nd].set(c)
    mod4 = _mod_call(cvec, w_mod, b_mod).reshape(depth, N_MOD_ROWS, 1, 6 * D_MODEL)

    rope_tab = _rope_tables()
    lane = np.arange(256)
    bd = jnp.asarray(lane[:, None] // HEAD_W == lane[None, :] // HEAD_W, BF16)
    tile = jnp.asarray(np.arange(32)[:, None] == np.arange(128)[None, :] % 32, BF16)
    tt = np.arange(GLA_BLOCK)
    same_chunk = tt[:, None] // GLA_CHUNK == tt[None, :] // GLA_CHUNK
    tri = jnp.asarray(np.stack([same_chunk & (tt[:, None] >= tt[None, :]),
                                same_chunk & (tt[:, None] <= tt[None, :])]), BF16)

    w_aq, w_rest, w_gates = _win_prep_call(w_in)
    wgate = jnp.zeros((depth, 128, 256), F32)
    wgate = wgate.at[:, 32:48, 0:128].set(b_gate_w_fwd).at[:, 48:64, 128:256].set(b_gate_w_bwd).astype(BF16)
    bgate = jnp.concatenate([b_gate_b_fwd, b_gate_b_bwd], axis=-1)[:, None, :]
    dqg = jnp.pad(d_qnorm_g, ((0, 0), (0, 256 - D_Q_RANK)))[:, None, :]
    uq = d_w_uq.reshape(depth, D_Q_RANK, 4, 96)
    wuq = jnp.concatenate([uq[..., :64].reshape(depth, D_Q_RANK, 256), uq[..., 64:].reshape(depth, D_Q_RANK, 128)], axis=-1)
    wuq = jnp.pad(wuq, ((0, 0), (0, 256 - D_Q_RANK), (0, 0))).astype(BF16)
    ukv = d_w_ukv.reshape(depth, 128, 4, 128)
    eye4 = jnp.eye(4, dtype=F32)
    wuk = (ukv[..., :64].transpose(0, 2, 3, 1)[:, :, :, None, :] * eye4[None, :, None, :, None])
    wuk = wuk.reshape(depth, 256, 512).astype(BF16)
    wuv = (ukv[..., 64:].transpose(0, 2, 1, 3)[:, :, :, None, :] * eye4[None, :, None, :, None])
    wuv = wuv.reshape(depth, 512, 256).astype(BF16)
    wb0 = w_branch[:, 0].reshape(depth, 2, 2, 64, D_MODEL).transpose(0, 2, 1, 3, 4).reshape(depth, 256, D_MODEL)
    wb = jnp.concatenate([wb0[:, None], w_branch[:, 1:]], axis=1).astype(BF16)
    wout = w_out.astype(BF16)
    wu, wg, wd = w_ffu.astype(BF16), w_ffg.astype(BF16), w_ffd.astype(BF16)
    cw = jnp.pad(conv_w, ((0, 0), (0, 8 - conv_w.shape[1]), (0, 0)))
    aqg = jnp.tile(a_qnorm_g, (1, 4))[:, None, :]
    akg = jnp.tile(a_knorm_g, (1, 2))[:, None, :]
    bong = jnp.tile(b_onorm_g, (1, 4))[:, None, :]
    cong = jnp.tile(c_onorm_g, (1, 4))[:, None, :]
    lamv = jnp.stack([c_lq1, c_lk1, c_lq2, c_lk2], axis=1)
    caches = (cache_a_k.reshape(nd, depth, PAST_LEN, 128), cache_a_v.reshape(nd, depth, PAST_LEN, 128),
              cache_c_k.reshape(nd, depth, PAST_LEN, 256), cache_c_v.reshape(nd, depth, PAST_LEN, 256),
              cache_d_ckv, cache_d_krope)
    expand = lambda s: (s.transpose(0, 1, 2, 4, 3)[:, :, :, :, None, :]
                        * eye4[None, None, :, None, :, None]).reshape(nd, depth, 256, 128)
    s0f_all, s0b_all = expand(state_b_fwd), expand(state_b_bwd)

    n1g, n2g = norm1_g[:, None, :], norm2_g[:, None, :]
    small = (aqg, akg, bd, wgate, bgate, dqg, wuq, wuk, d_kvnorm_g[:, None, :], tile)
    cb = conv_b[:, None, :]
    new_caches = None
    sfs, sbs = [], []
    x = (x_prompt, x_sample)
    for l in range(depth):
        lam_init = 0.8 - 0.6 * math.exp(-0.3 * l)
        fa, kva, qa, ha, new_caches = _inproj_calls(x, n, mod4, n1g, w_aq, w_rest, rope_tab, small, new_caches, l,
                                                    depth, n_ctx)
        oacd, ob, sf, sb = _ctx_mixers_call(fa, kva, qa, lamv, cong, wuv, bd, bong, tri, l, lam_init, n_ctx)
        oacd = _attn_lat_call(kva, qa, caches, lamv, cong, wuv, bd, tile, oacd, l, lam_init, n_ctx)
        ob = _gla_lat_call(fa, s0f_all, s0b_all, bong, bd, tri, ob, l, n_ctx)
        x1 = _merge_call(x, n, mod4, ha, w_gates, oacd, ob, wb, wout, l, n_ctx)
        res = _ffn_call(x1, mod4, n2g, wu, wg, cw, cb, wd, final_g[None], l, n_ctx, l == depth - 1)
        x = res[0]
        sfs.append(sf)
        sbs.append(sb)

    diag = lambda s: jnp.stack([s[:, :, hh * 64:(hh + 1) * 64, hh * 32:(hh + 1) * 32] for hh in range(4)],
                               axis=2).transpose(0, 1, 2, 4, 3)
    y_prompt, y_sample = res
    nak, nav, nck, ncv, nckv, nkr = new_caches
    return (y_prompt, y_sample, nak.reshape(nb, depth, SEQ, 2, 64), nav.reshape(nb, depth, SEQ, 2, 64),
            diag(jnp.stack(sfs, axis=1)), diag(jnp.stack(sbs, axis=1)), nck.reshape(nb, depth, SEQ, 4, 2, 32),
            ncv.reshape(nb, depth, SEQ, 4, 64), nckv, nkr)
```
